```python
import jax, jax.numpy as jnp
from jax import lax
import numpy as np

D_MODEL = 1024
BATCH = 4
SEQ = 4096
DEPTH = 1

HEAD_DIM = 64
ATTN_Q_HEADS = D_MODEL // (2 * HEAD_DIM)
ATTN_KV_HEADS = ATTN_Q_HEADS // 4
ATTN_GROUP = ATTN_Q_HEADS // ATTN_KV_HEADS
ATTN_WIDTH = ATTN_Q_HEADS * HEAD_DIM
KV_WIDTH = ATTN_KV_HEADS * HEAD_DIM
WINDOW = 128
BLOCK = 128
ROPE_THETA = 500000.0
ROT_DIM = HEAD_DIM // 4
RWKV_HEADS = D_MODEL // (2 * HEAD_DIM)
RWKV_WIDTH = RWKV_HEADS * HEAD_DIM
DECAY_RANK = 64
ICLR_RANK = 64
GATE_RANK = 128
N_BRANCH = 2
D_FF = 4 * D_MODEL
PLE_DIM = 256
NORM_EPS = 1e-6
GN_EPS = 64e-5
O_Q = 0
O_K = O_Q + ATTN_WIDTH
O_V = O_K + KV_WIDTH
O_RWKV = O_V + KV_WIDTH
RWKV_COLS = 3 * RWKV_WIDTH + DECAY_RANK + ICLR_RANK + GATE_RANK
O_GATE = O_RWKV + RWKV_COLS
IN_COLS = O_GATE + N_BRANCH * D_MODEL

kernel_name = 'hybrid_swa_rwkv7_encoder'


def rms_norm(x, g):
    xf = x.astype(jnp.float32)
    y = xf * lax.rsqrt(jnp.mean(xf * xf, axis=-1, keepdims=True) + NORM_EPS)
    return (y * g.astype(jnp.float32)).astype(x.dtype)


def rotary_partial(x, positions):
    half = ROT_DIM // 2
    inv_freq = jnp.power(jnp.float32(ROPE_THETA), -jnp.arange(half, dtype=jnp.float32) * 2.0 / ROT_DIM)
    ang = positions.astype(jnp.float32)[:, None] * inv_freq[None, :]
    cos = jnp.cos(ang)[None, :, None, :]
    sin = jnp.sin(ang)[None, :, None, :]
    xf = x[..., :ROT_DIM].astype(jnp.float32)
    x1, x2 = xf[..., :half], xf[..., half:]
    rot = jnp.concatenate([x1 * cos - x2 * sin, x2 * cos + x1 * sin], axis=-1).astype(x.dtype)
    return jnp.concatenate([rot, x[..., ROT_DIM:]], axis=-1)


def band_windows(t, n_blocks):
    b, _, h, d = t.shape
    tp = jnp.pad(t, ((0, 0), (BLOCK, BLOCK), (0, 0), (0, 0))).reshape(b, n_blocks + 2, BLOCK, h, d)
    return jnp.concatenate([tp[:, :-2], tp[:, 1:-1], tp[:, 2:]], axis=2)


def windowed_gqa(q, k, v, sink):
    b, s = q.shape[0], q.shape[1]
    nb = s // BLOCK
    qb = q.reshape(b, nb, BLOCK, ATTN_KV_HEADS, ATTN_GROUP, HEAD_DIM)
    kw = band_windows(k, nb)
    vw = band_windows(v, nb)
    scores = jnp.einsum('bnqhgd,bnkhd->bnhgqk', qb, kw).astype(jnp.float32) * (HEAD_DIM ** -0.5)
    qpos = jnp.arange(nb)[:, None] * BLOCK + jnp.arange(BLOCK)[None, :]
    kpos = (jnp.arange(nb)[:, None] - 1) * BLOCK + jnp.arange(3 * BLOCK)[None, :]
    valid = ((jnp.abs(qpos[:, :, None] - kpos[:, None, :]) <= WINDOW)
             & (kpos[:, None, :] >= 0) & (kpos[:, None, :] < s))
    scores = jnp.where(valid[None, :, None, None], scores, -1e30)
    sink_logit = jnp.broadcast_to(
        sink.astype(jnp.float32).reshape(ATTN_KV_HEADS, ATTN_GROUP)[None, None, :, :, None, None],
        scores.shape[:-1] + (1,))
    probs = jax.nn.softmax(jnp.concatenate([scores, sink_logit], axis=-1), axis=-1)[..., :-1]
    out = jnp.einsum('bnhgqk,bnkhd->bnqhgd', probs.astype(v.dtype), vw)
    return out.reshape(b, s, ATTN_WIDTH)


def centred_shift(z):
    prev = jnp.pad(z[:, :-1], ((0, 0), (1, 0), (0, 0)))
    nxt = jnp.pad(z[:, 1:], ((0, 0), (0, 1), (0, 0)))
    return 0.5 * (prev + nxt)


def rwkv7_bidir_scan(r, w, k, v, a, b):
    xs = (jnp.moveaxis(r, 2, 0), jnp.moveaxis(w, 2, 0), jnp.moveaxis(k, 2, 0),
          jnp.moveaxis(v, 2, 0), jnp.moveaxis(a, 2, 0), jnp.moveaxis(b, 2, 0))

    def step(state, inp):
        r_t, w_t, k_t, v_t, a_t, b_t = inp
        sa = jnp.einsum('dbhij,dbhj->dbhi', state, a_t)
        state = (state * w_t[..., None, :] + sa[..., :, None] * b_t[..., None, :]
                 + v_t[..., :, None] * k_t[..., None, :])
        y = jnp.einsum('dbhij,dbhj->dbhi', state, r_t)
        return state, y

    d, bsz, _, h, n = r.shape
    s0 = jnp.zeros((d, bsz, h, n, n), jnp.float32)
    _, ys = lax.scan(step, s0, xs)
    return jnp.moveaxis(ys, 0, 2)


def rwkv7_time_mix(z, w0, w2, a0, a2, g2, k_k, k_a, r_k, lnx_w, lnx_b):
    f32 = jnp.float32
    bsz, t = z.shape[0], z.shape[1]
    c = RWKV_WIDTH
    hn = (RWKV_HEADS, HEAD_DIM)
    r = z[..., :c]
    k = z[..., c:2 * c]
    v = z[..., 2 * c:3 * c]
    o = 3 * c
    wl = z[..., o:o + DECAY_RANK]
    o += DECAY_RANK
    al = z[..., o:o + ICLR_RANK]
    o += ICLR_RANK
    gl = z[..., o:o + GATE_RANK]
    w_raw = (w0[:, None, None, :] + jnp.einsum('btr,drc->dbtc', jnp.tanh(wl), w2)).astype(f32)
    decay = jnp.exp(-jnp.exp(-jax.nn.softplus(-w_raw) - 0.5))
    a = jax.nn.sigmoid((a0[:, None, None, :] + jnp.einsum('btr,drc->dbtc', al, a2)).astype(f32))
    g = (jax.nn.sigmoid(gl) @ g2).astype(f32)
    kk = (k * k_k).astype(f32).reshape(bsz, t, *hn)
    kk = kk / jnp.maximum(jnp.sqrt(jnp.sum(kk * kk, axis=-1, keepdims=True)), 1e-12)
    k_dir = (k.astype(f32)[None] * (1.0 + (a - 1.0) * k_a.astype(f32))).reshape(2, bsz, t, *hn)
    a_h = a.reshape(2, bsz, t, *hn)
    rh = r.astype(f32).reshape(bsz, t, *hn)
    vh = v.astype(f32).reshape(bsz, t, *hn)

    def orient(u):
        return jnp.stack([u[0], jnp.flip(u[1], axis=1)])

    def both(u):
        return jnp.stack([u, jnp.flip(u, axis=1)])

    ys = rwkv7_bidir_scan(both(rh), orient(decay.reshape(2, bsz, t, *hn)), orient(k_dir),
                          both(vh), both(-kk), orient(kk[None] * a_h))
    y = ys[0] + jnp.flip(ys[1], axis=1)
    mu = jnp.mean(y, axis=-1, keepdims=True)
    var = jnp.mean(jnp.square(y - mu), axis=-1, keepdims=True)
    y = ((y - mu) * lax.rsqrt(var + GN_EPS)).reshape(bsz, t, c) * lnx_w.astype(f32) + lnx_b.astype(f32)
    bonus = jnp.einsum('bthn,dbthn,dhn->bth', rh, k_dir, r_k.astype(f32))[..., None] * vh
    out = (y + bonus.reshape(bsz, t, c)) * g
    return out.astype(z.dtype)


def setup_inputs(seed: int = 0) -> dict:
    key = jax.random.key(seed)
    ks = jax.random.split(key, 32)
    f32 = jnp.float32
    L = DEPTH

    def nrm(k, shape, scale):
        return jax.random.normal(k, shape, f32) * scale

    def gain(k, shape):
        return 1.0 + 0.05 * jax.random.normal(k, shape, f32)

    return {
        'x': nrm(ks[0], (BATCH, SEQ, D_MODEL), 1.0),
        'p': nrm(ks[1], (DEPTH, BATCH, SEQ, PLE_DIM), 1.0),
        'norm_mix': gain(ks[2], (L, D_MODEL)),
        'w_in': nrm(ks[3], (L, D_MODEL, IN_COLS), D_MODEL ** -0.5),
        'shift_mu': jax.random.uniform(ks[4], (L, RWKV_COLS), f32),
        'q_norm': gain(ks[5], (L, HEAD_DIM)),
        'k_norm': gain(ks[6], (L, HEAD_DIM)),
        'sink': nrm(ks[7], (L, ATTN_Q_HEADS), 0.5),
        'w0': jax.random.uniform(ks[8], (L, 2, RWKV_WIDTH), f32, minval=-6.0, maxval=1.0),
        'w2': nrm(ks[9], (L, 2, DECAY_RANK, RWKV_WIDTH), 0.1),
        'a0': nrm(ks[10], (L, 2, RWKV_WIDTH), 0.5),
        'a2': nrm(ks[11], (L, 2, ICLR_RANK, RWKV_WIDTH), ICLR_RANK ** -0.5),
        'g2': nrm(ks[12], (L, GATE_RANK, RWKV_WIDTH), GATE_RANK ** -0.5),
        'k_k': 0.85 + 0.05 * jax.random.normal(ks[13], (L, RWKV_WIDTH), f32),
        'k_a': gain(ks[14], (L, RWKV_WIDTH)),
        'r_k': nrm(ks[15], (L, 2, RWKV_HEADS, HEAD_DIM), 0.1),
        'lnx_w': gain(ks[16], (L, RWKV_WIDTH)),
        'lnx_b': nrm(ks[17], (L, RWKV_WIDTH), 0.02),
        'w_up_attn': nrm(ks[18], (L, ATTN_WIDTH, D_MODEL), ATTN_WIDTH ** -0.5),
        'w_up_rwkv': nrm(ks[19], (L, RWKV_WIDTH, D_MODEL), RWKV_WIDTH ** -0.5),
        'w_out': nrm(ks[20], (L, D_MODEL, D_MODEL), D_MODEL ** -0.5),
        'norm_ffn': gain(ks[21], (L, D_MODEL)),
        'w_ff1': nrm(ks[22], (L, D_MODEL, D_FF), D_MODEL ** -0.5),
        'w_ff2': nrm(ks[23], (L, D_FF, D_MODEL), D_FF ** -0.5),
        'norm_ple': gain(ks[24], (L, D_MODEL)),
        'w_ple_gate': nrm(ks[25], (L, D_MODEL, D_MODEL), D_MODEL ** -0.5),
        'w_ple': nrm(ks[26], (L, PLE_DIM, D_MODEL), PLE_DIM ** -0.5),
    }


def reference(x, p, norm_mix, w_in, shift_mu, q_norm, k_norm, sink, w0, w2, a0, a2, g2,
              k_k, k_a, r_k, lnx_w, lnx_b, w_up_attn, w_up_rwkv, w_out, norm_ffn,
              w_ff1, w_ff2, norm_ple, w_ple_gate, w_ple):
    bsz, s = x.shape[0], x.shape[1]
    positions = jnp.arange(s)
    for i in range(DEPTH):
        h = rms_norm(x, norm_mix[i])
        proj = h @ w_in[i]
        q = proj[..., O_Q:O_K].reshape(bsz, s, ATTN_Q_HEADS, HEAD_DIM)
        k = proj[..., O_K:O_V].reshape(bsz, s, ATTN_KV_HEADS, HEAD_DIM)
        v = proj[..., O_V:O_RWKV].reshape(bsz, s, ATTN_KV_HEADS, HEAD_DIM)
        q = rotary_partial(rms_norm(q, q_norm[i]), positions)
        k = rotary_partial(rms_norm(k, k_norm[i]), positions)
        attn = windowed_gqa(q, k, v, sink[i])
        zr = proj[..., O_RWKV:O_GATE]
        zr = zr + shift_mu[i] * (centred_shift(zr) - zr)
        rw = rwkv7_time_mix(zr, w0[i], w2[i], a0[i], a2[i], g2[i], k_k[i], k_a[i],
                            r_k[i], lnx_w[i], lnx_b[i])
        gates = jax.nn.sigmoid(proj[..., O_GATE:].reshape(bsz, s, N_BRANCH, D_MODEL))
        merged = gates[..., 0, :] * (attn @ w_up_attn[i]) + gates[..., 1, :] * (rw @ w_up_rwkv[i])
        x = x + merged @ w_out[i]
        hf = rms_norm(x, norm_ffn[i])
        x = x + jnp.square(jax.nn.relu(hf @ w_ff1[i])) @ w_ff2[i]
        hp = rms_norm(x, norm_ple[i])
        x = x + (p[i] @ w_ple[i]) * jax.nn.sigmoid(hp @ w_ple_gate[i])
    return x
```

```python
import functools
import math

import jax
import jax.numpy as jnp
from jax import lax
from jax.experimental import pallas as pl
from jax.experimental.pallas import tpu as pltpu

F32 = jnp.float32
BF16 = jnp.bfloat16

LANES = 128
HEAD_DIM = 64
Q_HEADS = 8
KV_HEADS = 2
ATTN_WIDTH = Q_HEADS * HEAD_DIM
KV_WIDTH = KV_HEADS * HEAD_DIM
WINDOW = 128
ATTN_BLOCK = 128
ROPE_THETA = 500000.0
ROT_DIM = HEAD_DIM // 4
RWKV_WIDTH = 512
DECAY_RANK = 64
ICLR_RANK = 64
GATE_RANK = 128
RWKV_COLS = 3 * RWKV_WIDTH + DECAY_RANK + ICLR_RANK + GATE_RANK
NORM_EPS = 1e-6
GN_EPS = 64e-5
CHUNK = 64
GROUP_W = 128
VMEM_LIMIT = 56 * 1024 * 1024


def _cparams(sem):
    return pltpu.CompilerParams(dimension_semantics=sem, vmem_limit_bytes=VMEM_LIMIT)


def _dot(a, b):
    return jnp.dot(a, b, preferred_element_type=F32)


def _dot_nt(a, b):
    return lax.dot_general(a, b, (((1,), (1,)), ((), ())), preferred_element_type=F32)


def _dot_tn(a, b):
    return lax.dot_general(a, b, (((0,), (0,)), ((), ())), preferred_element_type=F32)


def _split_dot(x, w_bf16):
    hi = x.astype(BF16)
    lo = (x - hi.astype(F32)).astype(BF16)
    return _dot(hi, w_bf16) + _dot(lo, w_bf16)


def _head_sum(x, bd):
    cols = [_split_dot(x[:, i:i + LANES], bd) for i in range(0, x.shape[1], LANES)]
    return cols[0] if len(cols) == 1 else jnp.concatenate(cols, axis=1)


def _rms(x, gain):
    return x * lax.rsqrt(jnp.mean(x * x, axis=-1, keepdims=True) + NORM_EPS) * gain


def _sigmoid(x):
    return 1.0 / (1.0 + jnp.exp(-x))


QK_W = ATTN_WIDTH + 2 * KV_WIDTH
PROJ_W = QK_W + 2 * KV_WIDTH + RWKV_COLS


def _proj_kernel(x_ref, g_ref, w_ref, cos_ref, s1_ref, s2_ref, qkg_ref, bd_ref,
                 q_ref, k_ref, v_ref, zr_ref):
    h = _rms(x_ref[0], g_ref[...]).astype(BF16)
    proj = _dot(h, w_ref[...])
    qk = proj[:, :QK_W]
    ms = _head_sum(qk * qk, bd_ref[...]) * (1.0 / HEAD_DIM)
    qk = qk * lax.rsqrt(ms + NORM_EPS) * qkg_ref[...]
    cos, s1, s2 = cos_ref[...], s1_ref[...], s2_ref[...]
    tiles = []
    for i in range(0, QK_W, LANES):
        t = qk[:, i:i + LANES]
        t = t * cos + pltpu.roll(t, LANES - ROT_DIM // 2, 1) * s1 + pltpu.roll(t, ROT_DIM // 2, 1) * s2
        tiles.append(t)
    q_ref[0] = (jnp.concatenate(tiles[:4], axis=1) * (HEAD_DIM ** -0.5)).astype(BF16)
    k_ref[0] = jnp.concatenate(tiles[4:], axis=1).astype(BF16)
    v_ref[0] = proj[:, QK_W:QK_W + 2 * KV_WIDTH].astype(BF16)
    zr_ref[0] = proj[:, QK_W + 2 * KV_WIDTH:]


def _proj(x, gain, w, cos, s1, s2, qkg, bd, tm):
    b, s, d = x.shape
    const = lambda shape: pl.BlockSpec(shape, lambda bi, i: (0,) * len(shape))
    tok = lambda width: pl.BlockSpec((1, tm, width), lambda bi, i: (bi, i, 0))
    tab = pl.BlockSpec((tm, LANES), lambda bi, i: (i, 0))
    return pl.pallas_call(
        _proj_kernel,
        grid=(b, s // tm),
        in_specs=[tok(d), const((1, d)), const((d, PROJ_W)), tab, tab, tab, const((1, QK_W)),
                  const((LANES, LANES))],
        out_specs=[tok(ATTN_WIDTH), tok(2 * KV_WIDTH), tok(2 * KV_WIDTH), tok(RWKV_COLS)],
        out_shape=[jax.ShapeDtypeStruct((b, s, ATTN_WIDTH), BF16),
                   jax.ShapeDtypeStruct((b, s, 2 * KV_WIDTH), BF16),
                   jax.ShapeDtypeStruct((b, s, 2 * KV_WIDTH), BF16),
                   jax.ShapeDtypeStruct((b, s, RWKV_COLS), F32)],
        compiler_params=_cparams(("parallel", "parallel")),
        name="proj",
    )(x, gain, w, cos, s1, s2, qkg, bd)


def _attn_kernel(sink_ref, q_ref, kp_ref, kc_ref, kn_ref, vp_ref, vc_ref, vn_ref, o_ref):
    n = pl.program_id(1)
    nb = pl.num_programs(1)
    blk = ATTN_BLOCK
    k = jnp.concatenate([kp_ref[0], kc_ref[0], kn_ref[0]], axis=0)
    v = jnp.concatenate([vp_ref[0], vc_ref[0], vn_ref[0]], axis=0)
    qi = lax.broadcasted_iota(jnp.int32, (blk, 3 * blk), 0)
    ki = lax.broadcasted_iota(jnp.int32, (blk, 3 * blk), 1)
    valid = (ki >= qi) & (ki <= qi + 2 * WINDOW)
    valid = valid & ((ki >= blk) | (n > 0)) & ((ki < 2 * blk) | (n < nb - 1))
    left = lax.broadcasted_iota(jnp.int32, (1, LANES), 1) < HEAD_DIM
    zero = jnp.zeros((), BF16)
    outs = []
    for p in range(Q_HEADS // 2):
        g = p // 2
        qp = q_ref[0, :, p * LANES:(p + 1) * LANES]
        lo, hi = (0, LANES) if g == 0 else (LANES, 0)
        acc = None
        for side in range(2):
            off = lo if side == 0 else hi
            keep = left if side == 0 else jnp.logical_not(left)
            qh = jnp.where(keep, qp, zero)
            sc = _dot_nt(qh, k[:, off:off + LANES])
            sc = jnp.where(valid, sc, -1e30)
            sk = sink_ref[2 * p + side]
            m = jnp.maximum(jnp.max(sc, axis=-1, keepdims=True), sk)
            e = jnp.exp(sc - m)
            den = jnp.sum(e, axis=-1, keepdims=True) + jnp.exp(sk - m)
            vh = jnp.where(keep, v[:, off:off + LANES], zero)
            part = _dot(e.astype(BF16), vh) / den
            acc = part if acc is None else acc + part
        outs.append(acc)
    o_ref[0] = jnp.concatenate(outs, axis=1).astype(BF16)


def _attn(sink, q, k2, v2):
    b, s, _ = q.shape
    nb = s // ATTN_BLOCK
    kv = lambda fn: pl.BlockSpec((1, ATTN_BLOCK, 2 * KV_WIDTH), fn)
    prev = lambda bi, n: (bi, jnp.maximum(n - 1, 0), 0)
    cur = lambda bi, n: (bi, n, 0)
    nxt = lambda bi, n: (bi, jnp.minimum(n + 1, nb - 1), 0)
    return pl.pallas_call(
        _attn_kernel,
        grid=(b, nb),
        in_specs=[pl.BlockSpec(memory_space=pltpu.SMEM),
                  pl.BlockSpec((1, ATTN_BLOCK, ATTN_WIDTH), cur),
                  kv(prev), kv(cur), kv(nxt), kv(prev), kv(cur), kv(nxt)],
        out_specs=pl.BlockSpec((1, ATTN_BLOCK, ATTN_WIDTH), cur),
        out_shape=jax.ShapeDtypeStruct((b, s, ATTN_WIDTH), BF16),
        compiler_params=_cparams(("parallel", "parallel")),
        name="attn",
    )(sink, q, k2, k2, k2, v2, v2, v2)


HALO = 8


def _prep_kernel(z_ref, zp_ref, zn_ref, mu_ref, wlo_ref, g2_ref, w0_ref, a0_ref, kk_ref, ka_ref, rk_ref,
                 bd_ref, r_ref, v_ref, a_ref, lw0_ref, lw1_ref, k0_ref, k1_ref, b0_ref, b1_ref,
                 g_ref, bonus_ref):
    i = pl.program_id(1)
    nt = pl.num_programs(1)
    z = z_ref[0]
    tt = z.shape[0]
    row = lax.broadcasted_iota(jnp.int32, (tt, 1), 0)
    prev_row = jnp.where(i > 0, zp_ref[0, HALO - 1:HALO, :], 0.0)
    next_row = jnp.where(i < nt - 1, zn_ref[0, 0:1, :], 0.0)
    prev = jnp.where(row == 0, prev_row, pltpu.roll(z, 1, 0))
    nxt = jnp.where(row == tt - 1, next_row, pltpu.roll(z, tt - 1, 0))
    z = z + mu_ref[...] * (0.5 * (prev + nxt) - z)

    c = RWKV_WIDTH
    r, k, v = z[:, :c], z[:, c:2 * c], z[:, 2 * c:3 * c]
    lowrank = z[:, 3 * c:3 * c + LANES]
    left = lax.broadcasted_iota(jnp.int32, (1, LANES), 1) < DECAY_RANK
    lowrank = jnp.where(left, jnp.tanh(lowrank), lowrank).astype(BF16)
    wa = _dot(lowrank, wlo_ref[...])
    g_ref[0] = _dot(_sigmoid(z[:, 3 * c + LANES:]).astype(BF16), g2_ref[...])

    bd = bd_ref[...]
    kk = k * kk_ref[...]
    nrm = jnp.sqrt(_head_sum(kk * kk, bd))
    kk = kk / jnp.maximum(nrm, 1e-12)
    r_ref[0] = r
    v_ref[0] = v
    a_ref[0] = -kk
    dot_rk = None
    for d, (lw_ref, k_ref, b_ref) in enumerate(((lw0_ref, k0_ref, b0_ref), (lw1_ref, k1_ref, b1_ref))):
        w_raw = w0_ref[d:d + 1, :] + wa[:, d * c:(d + 1) * c]
        lw_ref[0] = -math.exp(-0.5) * _sigmoid(w_raw)
        rate = _sigmoid(a0_ref[d:d + 1, :] + wa[:, (2 + d) * c:(3 + d) * c])
        k_dir = k * (1.0 + (rate - 1.0) * ka_ref[...])
        k_ref[0] = k_dir
        b_ref[0] = kk * rate
        term = k_dir * rk_ref[d:d + 1, :]
        dot_rk = term if dot_rk is None else dot_rk + term
    bonus_ref[0] = _head_sum(r * dot_rk, bd) * v


def _prep(zr, mu, wlo, g2, w0, a0, k_k, k_a, r_k, bd, tt):
    b, s, _ = zr.shape
    nh = tt // HALO
    const = lambda shape: pl.BlockSpec(shape, lambda bi, i: (0,) * len(shape))
    tok = lambda width: pl.BlockSpec((1, tt, width), lambda bi, i: (bi, i, 0))
    halo_p = pl.BlockSpec((1, HALO, RWKV_COLS), lambda bi, i: (bi, jnp.maximum(i * nh - 1, 0), 0))
    halo_n = pl.BlockSpec((1, HALO, RWKV_COLS), lambda bi, i: (bi, jnp.minimum((i + 1) * nh, s // HALO - 1), 0))
    c = RWKV_WIDTH
    return pl.pallas_call(
        _prep_kernel,
        grid=(b, s // tt),
        in_specs=[tok(RWKV_COLS), halo_p, halo_n, const((1, RWKV_COLS)), const((LANES, 4 * c)),
                  const((GATE_RANK, c)), const((2, c)), const((2, c)), const((1, c)), const((1, c)),
                  const((2, c)), const((LANES, LANES))],
        out_specs=[tok(c)] * 11,
        out_shape=[jax.ShapeDtypeStruct((b, s, c), F32)] * 11,
        compiler_params=_cparams(("parallel", "parallel")),
        name="prep",
    )(zr, zr, zr, mu, wlo, g2, w0, a0, k_k, k_a, r_k, bd)


def _scan_kernel(r0_ref, v0_ref, a0_ref, lw0_ref, k0_ref, b0_ref,
                 r1_ref, v1_ref, a1_ref, lw1_ref, k1_ref, b1_ref,
                 y0_ref, y1_ref, ht_ref):
    L, gw = CHUNK, GROUP_W
    rep = gw // L

    @pl.when(pl.program_id(1) == 0)
    def _():
        ht_ref[...] = jnp.zeros_like(ht_ref)

    tau = lax.broadcasted_iota(jnp.int32, (L, gw), 0)
    sig = lax.broadcasted_iota(jnp.int32, (L, gw), 1) % L
    bd = (lax.broadcasted_iota(jnp.int32, (gw, gw), 0) // L) == (lax.broadcasted_iota(jnp.int32, (gw, gw), 1) // L)
    ti = lax.broadcasted_iota(jnp.int32, (L, L), 0)
    si = lax.broadcasted_iota(jnp.int32, (L, L), 1)

    def bdtile(x):
        xb = x.astype(BF16)
        return jnp.where(bd, jnp.concatenate([xb] * rep, axis=0), jnp.zeros((), BF16))

    dirs = ((r0_ref, v0_ref, a0_ref, lw0_ref, k0_ref, b0_ref, y0_ref),
            (r1_ref, v1_ref, a1_ref, lw1_ref, k1_ref, b1_ref, y1_ref))
    for d, (r_ref, v_ref, a_ref, lw_ref, k_ref, b_ref, y_ref) in enumerate(dirs):
        if d == 0:
            tri = (si <= ti).astype(BF16)
            strict, incl = sig < tau, sig <= tau
        else:
            tri = (si >= ti).astype(BF16)
            strict, incl = sig > tau, sig >= tau
        lw = lw_ref[0]
        cl = _split_dot_left(tri, lw)
        e_in = jnp.exp(cl)
        e_inv = jnp.exp(-cl)
        e_ex = jnp.exp(cl - lw)
        d_end = jnp.exp(cl[L - 1:L, :] if d == 0 else cl[0:1, :])
        a_t = a_ref[0] * e_ex
        r_t = r_ref[0] * e_in
        b_t = b_ref[0] * e_inv
        k_t = k_ref[0] * e_inv
        b_h = b_t * d_end
        k_h = k_t * d_end
        v_all = v_ref[0]
        ys = []
        for g in range(RWKV_WIDTH // gw):
            sl = slice(g * gw, (g + 1) * gw)
            at, rt, vv = a_t[:, sl], r_t[:, sl], v_all[:, sl]
            lhs = jnp.concatenate([at, rt], axis=0).astype(BF16)
            gb = _dot_nt(lhs, bdtile(b_t[:, sl]))
            gk = _dot_nt(lhs, bdtile(k_t[:, sl]))
            m_ab = jnp.where(strict, gb[:L], 0.0)
            m_rb = jnp.where(incl, gb[L:], 0.0)
            m_ak = jnp.where(strict, gk[:L], 0.0)
            m_rk = jnp.where(incl, gk[L:], 0.0)
            v_bd = bdtile(vv)
            x1 = at
            x2 = _dot(m_ak.astype(BF16), v_bd)
            p = m_ab
            levels = int(math.log2(L))
            for lvl in range(levels):
                ops = [bdtile(x1), bdtile(x2)] + ([bdtile(p)] if lvl < levels - 1 else [])
                res = _dot(p.astype(BF16), jnp.concatenate(ops, axis=1))
                x1 = x1 + res[:, :gw]
                x2 = x2 + res[:, gw:2 * gw]
                if lvl < levels - 1:
                    p = res[:, 2 * gw:]
            ht = ht_ref[d, g]
            htb = ht.astype(BF16)
            u = x2 + _dot_nt(x1.astype(BF16), htb)
            y = (_dot_nt(rt.astype(BF16), htb) + _dot(m_rb.astype(BF16), bdtile(u))
                 + _dot(m_rk.astype(BF16), v_bd))
            upd = _dot_tn(jnp.concatenate([u, vv], axis=0).astype(BF16),
                          jnp.concatenate([b_h[:, sl], k_h[:, sl]], axis=0).astype(BF16))
            ht_ref[d, g] = ht * d_end[:, sl] + jnp.where(bd, upd, 0.0)
            ys.append(y)
        y_ref[0] = jnp.concatenate(ys, axis=1)


def _split_dot_left(w_bf16, x):
    hi = x.astype(BF16)
    lo = (x - hi.astype(F32)).astype(BF16)
    return _dot(w_bf16, hi) + _dot(w_bf16, lo)


def _scan(r, v, a, lw0, lw1, k0, k1, b0, b1):
    b, s, c = r.shape
    nc = s // CHUNK
    fwd = pl.BlockSpec((1, CHUNK, c), lambda bi, i: (bi, i, 0))
    bwd = pl.BlockSpec((1, CHUNK, c), lambda bi, i: (bi, nc - 1 - i, 0))
    ng = c // GROUP_W
    return pl.pallas_call(
        _scan_kernel,
        grid=(b, nc),
        in_specs=[fwd] * 6 + [bwd] * 6,
        out_specs=[fwd, bwd],
        out_shape=[jax.ShapeDtypeStruct((b, s, c), F32)] * 2,
        scratch_shapes=[pltpu.VMEM((2, ng, GROUP_W, GROUP_W), F32)],
        compiler_params=_cparams(("parallel", "arbitrary")),
        name="scan",
    )(r, v, a, lw0, k0, b0, r, v, a, lw1, k1, b1)


def _merge_kernel(x_ref, gain_ref, wg_ref, attn_ref, y0_ref, y1_ref, bonus_ref, g_ref, lnw_ref, lnb_ref,
                  bd_ref, wua_ref, wur_ref, wout_ref, o_ref):
    x = x_ref[0]
    d = x.shape[1]
    h = _rms(x, gain_ref[...]).astype(BF16)
    gates = _sigmoid(_dot(h, wg_ref[...]))
    bd = bd_ref[...]
    y = y0_ref[0] + y1_ref[0]
    yc = y - _head_sum(y, bd) * (1.0 / HEAD_DIM)
    var = _head_sum(yc * yc, bd) * (1.0 / HEAD_DIM)
    yn = yc * lax.rsqrt(var + GN_EPS) * lnw_ref[...] + lnb_ref[...]
    rw = (yn + bonus_ref[0]) * g_ref[0]
    merged = (gates[:, :d] * _dot(attn_ref[0], wua_ref[...])
              + gates[:, d:] * _dot(rw.astype(BF16), wur_ref[...]))
    o_ref[0] = x + _dot(merged.astype(BF16), wout_ref[...])


def _merge(x, gain, wg, attn, y0, y1, bonus, g, lnw, lnb, bd, wua, wur, wout, tm):
    b, s, d = x.shape
    c = RWKV_WIDTH
    const = lambda shape: pl.BlockSpec(shape, lambda bi, i: (0,) * len(shape))
    tok = lambda width: pl.BlockSpec((1, tm, width), lambda bi, i: (bi, i, 0))
    return pl.pallas_call(
        _merge_kernel,
        grid=(b, s // tm),
        in_specs=[tok(d), const((1, d)), const((d, 2 * d)), tok(ATTN_WIDTH), tok(c), tok(c), tok(c), tok(c),
                  const((1, c)), const((1, c)), const((LANES, LANES)), const((ATTN_WIDTH, d)), const((c, d)),
                  const((d, d))],
        out_specs=tok(d),
        out_shape=jax.ShapeDtypeStruct((b, s, d), F32),
        compiler_params=_cparams(("parallel", "parallel")),
        name="merge",
    )(x, gain, wg, attn, y0, y1, bonus, g, lnw, lnb, bd, wua, wur, wout)


FF_SLAB = 1024


def _ffn_kernel(x_ref, p_ref, gf_ref, w1_ref, w2_ref, gp_ref, wpg_ref, wple_ref, o_ref):
    x = x_ref[0]
    h = _rms(x, gf_ref[...]).astype(BF16)
    acc = x
    for j in range(0, w1_ref.shape[1], FF_SLAB):
        hid = jnp.maximum(_dot(h, w1_ref[:, j:j + FF_SLAB]), 0.0)
        acc = acc + _dot((hid * hid).astype(BF16), w2_ref[j:j + FF_SLAB, :])
    hp = _rms(acc, gp_ref[...]).astype(BF16)
    o_ref[0] = acc + _dot(p_ref[0].astype(BF16), wple_ref[...]) * _sigmoid(_dot(hp, wpg_ref[...]))


def _ffn(x, p, gf, w1, w2, gp, wpg, wple, tm):
    b, s, d = x.shape
    dff = w1.shape[1]
    pd = p.shape[-1]
    const = lambda shape: pl.BlockSpec(shape, lambda bi, i: (0,) * len(shape))
    tok = lambda width: pl.BlockSpec((1, tm, width), lambda bi, i: (bi, i, 0))
    return pl.pallas_call(
        _ffn_kernel,
        grid=(b, s // tm),
        in_specs=[tok(d), tok(pd), const((1, d)), const((d, dff)), const((dff, d)), const((1, d)),
                  const((d, d)), const((pd, d))],
        out_specs=tok(d),
        out_shape=jax.ShapeDtypeStruct((b, s, d), F32),
        compiler_params=_cparams(("parallel", "parallel")),
        name="ffn",
    )(x, p, gf, w1, w2, gp, wpg, wple)


def _rotary_tables(s):
    half = ROT_DIM // 2
    inv_freq = jnp.power(jnp.float32(ROPE_THETA), -jnp.arange(half, dtype=F32) * 2.0 / ROT_DIM)
    ang = jnp.arange(s).astype(F32)[:, None] * inv_freq[None, :]
    cos, sin = jnp.cos(ang), jnp.sin(ang)
    pad = jnp.zeros((s, HEAD_DIM - ROT_DIM), F32)
    zero = jnp.zeros((s, half), F32)
    c = jnp.concatenate([cos, cos, pad + 1.0], axis=1)
    s1 = jnp.concatenate([-sin, zero, pad], axis=1)
    s2 = jnp.concatenate([zero, sin, pad], axis=1)
    rep = LANES // HEAD_DIM
    return jnp.tile(c, (1, rep)), jnp.tile(s1, (1, rep)), jnp.tile(s2, (1, rep))


def _swap_halves(w):
    return jnp.concatenate([w[:, HEAD_DIM:], w[:, :HEAD_DIM]], axis=1)


def kernel(x, p, norm_mix, w_in, shift_mu, q_norm, k_norm, sink, w0, w2, a0, a2, g2, k_k, k_a, r_k, lnx_w, lnx_b,
           w_up_attn, w_up_rwkv, w_out, norm_ffn, w_ff1, w_ff2, norm_ple, w_ple_gate, w_ple):
    bsz, s, d = x.shape
    depth = w_in.shape[0]
    c = RWKV_WIDTH
    tm = min(512, s)
    tt = min(256, s)
    cos, s1, s2 = _rotary_tables(s)
    lane = jnp.arange(LANES)
    bd = ((lane[:, None] // HEAD_DIM) == (lane[None, :] // HEAD_DIM)).astype(BF16)
    o_k, o_v, o_r, o_g = ATTN_WIDTH, ATTN_WIDTH + KV_WIDTH, ATTN_WIDTH + 2 * KV_WIDTH, ATTN_WIDTH + 2 * KV_WIDTH + RWKV_COLS
    for i in range(depth):
        wi = w_in[i]
        wk, wv = wi[:, o_k:o_v], wi[:, o_v:o_r]
        w_proj = jnp.concatenate([wi[:, :o_k], wk, _swap_halves(wk), wv, _swap_halves(wv), wi[:, o_r:o_g]],
                                 axis=1).astype(BF16)
        qkg = jnp.concatenate([jnp.tile(q_norm[i], Q_HEADS), jnp.tile(k_norm[i], 2 * KV_HEADS)])[None, :]
        q, k2, v2, zr = _proj(x, norm_mix[i][None, :], w_proj, cos, s1, s2, qkg, bd, tm)
        attn = _attn(sink[i], q, k2, v2)

        zpad = jnp.zeros((DECAY_RANK, c), F32)
        wlo = jnp.concatenate([jnp.concatenate([w2[i, 0], zpad], axis=0), jnp.concatenate([w2[i, 1], zpad], axis=0),
                               jnp.concatenate([zpad, a2[i, 0]], axis=0), jnp.concatenate([zpad, a2[i, 1]], axis=0)],
                              axis=1).astype(BF16)
        (r, v, a, lw0, lw1, k0, k1, b0, b1, g, bonus) = _prep(
            zr, shift_mu[i][None, :], wlo, g2[i].astype(BF16), w0[i], a0[i], k_k[i][None, :], k_a[i][None, :],
            r_k[i].reshape(2, c), bd, tt)
        y0, y1 = _scan(r, v, a, lw0, lw1, k0, k1, b0, b1)

        x = _merge(x, norm_mix[i][None, :], wi[:, o_g:].astype(BF16), attn, y0, y1, bonus, g,
                   lnx_w[i][None, :], lnx_b[i][None, :], bd, w_up_attn[i].astype(BF16), w_up_rwkv[i].astype(BF16),
                   w_out[i].astype(BF16), tm)
        x = _ffn(x, p[i], norm_ffn[i][None, :], w_ff1[i].astype(BF16), w_ff2[i].astype(BF16), norm_ple[i][None, :],
                 w_ple_gate[i].astype(BF16), w_ple[i].astype(BF16), tm)
    return x
```

```python
import functools
import math

import jax
import jax.numpy as jnp
from jax import lax
from jax.experimental import pallas as pl
from jax.experimental.pallas import tpu as pltpu

F32 = jnp.float32
BF16 = jnp.bfloat16

LANES = 128
HEAD_DIM = 64
Q_HEADS = 8
KV_HEADS = 2
ATTN_WIDTH = Q_HEADS * HEAD_DIM
KV_WIDTH = KV_HEADS * HEAD_DIM
WINDOW = 128
ATTN_BLOCK = 128
ROPE_THETA = 500000.0
ROT_DIM = HEAD_DIM // 4
RWKV_WIDTH = 512
DECAY_RANK = 64
ICLR_RANK = 64
GATE_RANK = 128
RWKV_COLS = 3 * RWKV_WIDTH + DECAY_RANK + ICLR_RANK + GATE_RANK
NORM_EPS = 1e-6
GN_EPS = 64e-5
CHUNK = 64
GROUP_W = 128
VMEM_LIMIT = 56 * 1024 * 1024


def _cparams(sem):
    return pltpu.CompilerParams(dimension_semantics=sem, vmem_limit_bytes=VMEM_LIMIT)


def _dot(a, b):
    return jnp.dot(a, b, preferred_element_type=F32)


def _dot_nt(a, b):
    return lax.dot_general(a, b, (((1,), (1,)), ((), ())), preferred_element_type=F32)


def _dot_tn(a, b):
    return lax.dot_general(a, b, (((0,), (0,)), ((), ())), preferred_element_type=F32)


def _split_dot(x, w_bf16):
    hi = x.astype(BF16)
    lo = (x - hi.astype(F32)).astype(BF16)
    return _dot(hi, w_bf16) + _dot(lo, w_bf16)


def _head_sum(x, bd):
    cols = [_split_dot(x[:, i:i + LANES], bd) for i in range(0, x.shape[1], LANES)]
    return cols[0] if len(cols) == 1 else jnp.concatenate(cols, axis=1)


def _rms(x, gain):
    return x * lax.rsqrt(jnp.mean(x * x, axis=-1, keepdims=True) + NORM_EPS) * gain


def _sigmoid(x):
    return 1.0 / (1.0 + jnp.exp(-x))


QK_W = ATTN_WIDTH + 2 * KV_WIDTH
PROJ_W = QK_W + 2 * KV_WIDTH + RWKV_COLS


def _proj_kernel(x_ref, g_ref, w_ref, cos_ref, s1_ref, s2_ref, qkg_ref, bd_ref,
                 q_ref, k_ref, v_ref, zr_ref):
    h = _rms(x_ref[0], g_ref[...]).astype(BF16)
    proj = _dot(h, w_ref[...])
    qk = proj[:, :QK_W]
    ms = _head_sum(qk * qk, bd_ref[...]) * (1.0 / HEAD_DIM)
    qk = qk * lax.rsqrt(ms + NORM_EPS) * qkg_ref[...]
    cos, s1, s2 = cos_ref[...], s1_ref[...], s2_ref[...]
    tiles = []
    for i in range(0, QK_W, LANES):
        t = qk[:, i:i + LANES]
        t = t * cos + pltpu.roll(t, LANES - ROT_DIM // 2, 1) * s1 + pltpu.roll(t, ROT_DIM // 2, 1) * s2
        tiles.append(t)
    q_ref[0] = (jnp.concatenate(tiles[:4], axis=1) * (HEAD_DIM ** -0.5)).astype(BF16)
    k_ref[0] = jnp.concatenate(tiles[4:], axis=1).astype(BF16)
    v_ref[0] = proj[:, QK_W:QK_W + 2 * KV_WIDTH].astype(BF16)
    zr_ref[0] = proj[:, QK_W + 2 * KV_WIDTH:]


def _proj(x, gain, w, cos, s1, s2, qkg, bd, tm):
    b, s, d = x.shape
    const = lambda shape: pl.BlockSpec(shape, lambda bi, i: (0,) * len(shape))
    tok = lambda width: pl.BlockSpec((1, tm, width), lambda bi, i: (bi, i, 0))
    tab = pl.BlockSpec((tm, LANES), lambda bi, i: (i, 0))
    return pl.pallas_call(
        _proj_kernel,
        grid=(b, s // tm),
        in_specs=[tok(d), const((1, d)), const((d, PROJ_W)), tab, tab, tab, const((1, QK_W)),
                  const((LANES, LANES))],
        out_specs=[tok(ATTN_WIDTH), tok(2 * KV_WIDTH), tok(2 * KV_WIDTH), tok(RWKV_COLS)],
        out_shape=[jax.ShapeDtypeStruct((b, s, ATTN_WIDTH), BF16),
                   jax.ShapeDtypeStruct((b, s, 2 * KV_WIDTH), BF16),
                   jax.ShapeDtypeStruct((b, s, 2 * KV_WIDTH), BF16),
                   jax.ShapeDtypeStruct((b, s, RWKV_COLS), F32)],
        compiler_params=_cparams(("parallel", "parallel")),
        name="proj",
    )(x, gain, w, cos, s1, s2, qkg, bd)


def _attn_kernel(sink_ref, q_ref, kp_ref, kc_ref, kn_ref, vp_ref, vc_ref, vn_ref, o_ref):
    n = pl.program_id(1)
    nb = pl.num_programs(1)
    blk = ATTN_BLOCK
    k = jnp.concatenate([kp_ref[0], kc_ref[0], kn_ref[0]], axis=0)
    v = jnp.concatenate([vp_ref[0], vc_ref[0], vn_ref[0]], axis=0)
    qi = lax.broadcasted_iota(jnp.int32, (blk, 3 * blk), 0)
    ki = lax.broadcasted_iota(jnp.int32, (blk, 3 * blk), 1)
    valid = (ki >= qi) & (ki <= qi + 2 * WINDOW)
    valid = valid & ((ki >= blk) | (n > 0)) & ((ki < 2 * blk) | (n < nb - 1))
    left = lax.broadcasted_iota(jnp.int32, (1, LANES), 1) < HEAD_DIM
    zero = jnp.zeros((), BF16)
    outs = []
    for p in range(Q_HEADS // 2):
        g = p // 2
        qp = q_ref[0, :, p * LANES:(p + 1) * LANES]
        lo, hi = (0, LANES) if g == 0 else (LANES, 0)
        acc = None
        for side in range(2):
            off = lo if side == 0 else hi
            keep = left if side == 0 else jnp.logical_not(left)
            qh = jnp.where(keep, qp, zero)
            sc = _dot_nt(qh, k[:, off:off + LANES])
            sc = jnp.where(valid, sc, -1e30)
            sk = sink_ref[2 * p + side]
            m = jnp.maximum(jnp.max(sc, axis=-1, keepdims=True), sk)
            e = jnp.exp(sc - m)
            den = jnp.sum(e, axis=-1, keepdims=True) + jnp.exp(sk - m)
            vh = jnp.where(keep, v[:, off:off + LANES], zero)
            part = _dot(e.astype(BF16), vh) / den
            acc = part if acc is None else acc + part
        outs.append(acc)
    o_ref[0] = jnp.concatenate(outs, axis=1).astype(BF16)


def _attn(sink, q, k2, v2):
    b, s, _ = q.shape
    nb = s // ATTN_BLOCK
    kv = lambda fn: pl.BlockSpec((1, ATTN_BLOCK, 2 * KV_WIDTH), fn)
    prev = lambda bi, n: (bi, jnp.maximum(n - 1, 0), 0)
    cur = lambda bi, n: (bi, n, 0)
    nxt = lambda bi, n: (bi, jnp.minimum(n + 1, nb - 1), 0)
    return pl.pallas_call(
        _attn_kernel,
        grid=(b, nb),
        in_specs=[pl.BlockSpec(memory_space=pltpu.SMEM),
                  pl.BlockSpec((1, ATTN_BLOCK, ATTN_WIDTH), cur),
                  kv(prev), kv(cur), kv(nxt), kv(prev), kv(cur), kv(nxt)],
        out_specs=pl.BlockSpec((1, ATTN_BLOCK, ATTN_WIDTH), cur),
        out_shape=jax.ShapeDtypeStruct((b, s, ATTN_WIDTH), BF16),
        compiler_params=_cparams(("parallel", "parallel")),
        name="attn",
    )(sink, q, k2, k2, k2, v2, v2, v2)


HALO = 8


def _prep_kernel(z_ref, zp_ref, zn_ref, mu_ref, wlo_ref, g2_ref, w0_ref, a0_ref, kk_ref, ka_ref, rk_ref,
                 bd_ref, r_ref, v_ref, a_ref, lw0_ref, lw1_ref, k0_ref, k1_ref, b0_ref, b1_ref,
                 g_ref, bonus_ref):
    i = pl.program_id(1)
    nt = pl.num_programs(1)
    z = z_ref[0]
    tt = z.shape[0]
    row = lax.broadcasted_iota(jnp.int32, (tt, 1), 0)
    prev_row = jnp.where(i > 0, zp_ref[0, HALO - 1:HALO, :], 0.0)
    next_row = jnp.where(i < nt - 1, zn_ref[0, 0:1, :], 0.0)
    prev = jnp.where(row == 0, prev_row, pltpu.roll(z, 1, 0))
    nxt = jnp.where(row == tt - 1, next_row, pltpu.roll(z, tt - 1, 0))
    z = z + mu_ref[...] * (0.5 * (prev + nxt) - z)

    c = RWKV_WIDTH
    r, k, v = z[:, :c], z[:, c:2 * c], z[:, 2 * c:3 * c]
    lowrank = z[:, 3 * c:3 * c + LANES]
    left = lax.broadcasted_iota(jnp.int32, (1, LANES), 1) < DECAY_RANK
    lowrank = jnp.where(left, jnp.tanh(lowrank), lowrank).astype(BF16)
    wa = _dot(lowrank, wlo_ref[...])
    g_ref[0] = _dot(_sigmoid(z[:, 3 * c + LANES:]).astype(BF16), g2_ref[...])

    bd = bd_ref[...]
    kk = k * kk_ref[...]
    nrm = jnp.sqrt(_head_sum(kk * kk, bd))
    kk = kk / jnp.maximum(nrm, 1e-12)
    r_ref[0] = r
    v_ref[0] = v
    a_ref[0] = -kk
    dot_rk = None
    for d, (lw_ref, k_ref, b_ref) in enumerate(((lw0_ref, k0_ref, b0_ref), (lw1_ref, k1_ref, b1_ref))):
        w_raw = w0_ref[d:d + 1, :] + wa[:, d * c:(d + 1) * c]
        lw_ref[0] = -math.exp(-0.5) * _sigmoid(w_raw)
        rate = _sigmoid(a0_ref[d:d + 1, :] + wa[:, (2 + d) * c:(3 + d) * c])
        k_dir = k * (1.0 + (rate - 1.0) * ka_ref[...])
        k_ref[0] = k_dir
        b_ref[0] = kk * rate
        term = k_dir * rk_ref[d:d + 1, :]
        dot_rk = term if dot_rk is None else dot_rk + term
    bonus_ref[0] = _head_sum(r * dot_rk, bd) * v


def _prep(zr, mu, wlo, g2, w0, a0, k_k, k_a, r_k, bd, tt):
    b, s, _ = zr.shape
    nh = tt // HALO
    const = lambda shape: pl.BlockSpec(shape, lambda bi, i: (0,) * len(shape))
    tok = lambda width: pl.BlockSpec((1, tt, width), lambda bi, i: (bi, i, 0))
    halo_p = pl.BlockSpec((1, HALO, RWKV_COLS), lambda bi, i: (bi, jnp.maximum(i * nh - 1, 0), 0))
    halo_n = pl.BlockSpec((1, HALO, RWKV_COLS), lambda bi, i: (bi, jnp.minimum((i + 1) * nh, s // HALO - 1), 0))
    c = RWKV_WIDTH
    return pl.pallas_call(
        _prep_kernel,
        grid=(b, s // tt),
        in_specs=[tok(RWKV_COLS), halo_p, halo_n, const((1, RWKV_COLS)), const((LANES, 4 * c)),
                  const((GATE_RANK, c)), const((2, c)), const((2, c)), const((1, c)), const((1, c)),
                  const((2, c)), const((LANES, LANES))],
        out_specs=[tok(c)] * 11,
        out_shape=[jax.ShapeDtypeStruct((b, s, c), F32)] * 11,
        compiler_params=_cparams(("parallel", "parallel")),
        name="prep",
    )(zr, zr, zr, mu, wlo, g2, w0, a0, k_k, k_a, r_k, bd)


def _scan_kernel(r0_ref, v0_ref, a0_ref, lw0_ref, k0_ref, b0_ref,
                 r1_ref, v1_ref, a1_ref, lw1_ref, k1_ref, b1_ref,
                 y0_ref, y1_ref, ht_ref):
    L, gw = CHUNK, GROUP_W
    rep = gw // L

    @pl.when(pl.program_id(1) == 0)
    def _():
        ht_ref[...] = jnp.zeros_like(ht_ref)

    tau = lax.broadcasted_iota(jnp.int32, (L, gw), 0)
    sig = lax.broadcasted_iota(jnp.int32, (L, gw), 1) % L
    bd = (lax.broadcasted_iota(jnp.int32, (gw, gw), 0) // L) == (lax.broadcasted_iota(jnp.int32, (gw, gw), 1) // L)
    ti = lax.broadcasted_iota(jnp.int32, (L, L), 0)
    si = lax.broadcasted_iota(jnp.int32, (L, L), 1)

    def bdtile(x):
        xb = x.astype(BF16)
        return jnp.where(bd, jnp.concatenate([xb] * rep, axis=0), jnp.zeros((), BF16))

    dirs = ((r0_ref, v0_ref, a0_ref, lw0_ref, k0_ref, b0_ref),
            (r1_ref, v1_ref, a1_ref, lw1_ref, k1_ref, b1_ref))
    recs = []
    for d, (r_ref, v_ref, a_ref, lw_ref, k_ref, b_ref) in enumerate(dirs):
        if d == 0:
            tri = (si <= ti).astype(BF16)
            strict, incl = sig < tau, sig <= tau
        else:
            tri = (si >= ti).astype(BF16)
            strict, incl = sig > tau, sig >= tau
        lw = lw_ref[0]
        cl = _split_dot_left(tri, lw)
        e_in = jnp.exp(cl)
        e_inv = jnp.exp(-cl)
        e_ex = jnp.exp(cl - lw)
        d_end = jnp.exp(cl[L - 1:L, :] if d == 0 else cl[0:1, :])
        a_t = a_ref[0] * e_ex
        r_t = r_ref[0] * e_in
        b_t = b_ref[0] * e_inv
        k_t = k_ref[0] * e_inv
        b_h = b_t * d_end
        k_h = k_t * d_end
        v_all = v_ref[0]
        for g in range(RWKV_WIDTH // gw):
            sl = slice(g * gw, (g + 1) * gw)
            recs.append(dict(d=d, g=g, strict=strict, incl=incl, at=a_t[:, sl], rt=r_t[:, sl], vv=v_all[:, sl],
                             bt=b_t[:, sl], kt=k_t[:, sl], bh=b_h[:, sl], kh=k_h[:, sl], dend=d_end[:, sl]))

    for c in recs:
        c["lhs"] = jnp.concatenate([c["at"], c["rt"]], axis=0).astype(BF16)
    for c in recs:
        c["gb"] = _dot_nt(c["lhs"], bdtile(c["bt"]))
    for c in recs:
        c["gk"] = _dot_nt(c["lhs"], bdtile(c["kt"]))
    for c in recs:
        c["m_rb"] = jnp.where(c["incl"], c["gb"][L:], 0.0).astype(BF16)
        c["m_rk"] = jnp.where(c["incl"], c["gk"][L:], 0.0).astype(BF16)
        c["p"] = jnp.where(c["strict"], c["gb"][:L], 0.0)
        c["v_bd"] = bdtile(c["vv"])
        c["x1"] = c["at"]
    for c in recs:
        c["x2"] = _dot(jnp.where(c["strict"], c["gk"][:L], 0.0).astype(BF16), c["v_bd"])
    levels = int(math.log2(L))
    for lvl in range(levels):
        last = lvl == levels - 1
        for c in recs:
            ops = [bdtile(c["x1"]), bdtile(c["x2"])] + ([] if last else [bdtile(c["p"])])
            c["res"] = _dot(c["p"].astype(BF16), jnp.concatenate(ops, axis=1))
        for c in recs:
            res = c["res"]
            c["x1"] = c["x1"] + res[:, :gw]
            c["x2"] = c["x2"] + res[:, gw:2 * gw]
            if not last:
                c["p"] = res[:, 2 * gw:]
    for c in recs:
        c["ht"] = ht_ref[c["d"], c["g"]]
        c["htb"] = c["ht"].astype(BF16)
    for c in recs:
        c["u"] = c["x2"] + _dot_nt(c["x1"].astype(BF16), c["htb"])
    for c in recs:
        c["y"] = (_dot_nt(c["rt"].astype(BF16), c["htb"]) + _dot(c["m_rb"], bdtile(c["u"]))
                  + _dot(c["m_rk"], c["v_bd"]))
    for c in recs:
        upd = _dot_tn(jnp.concatenate([c["u"], c["vv"]], axis=0).astype(BF16),
                      jnp.concatenate([c["bh"], c["kh"]], axis=0).astype(BF16))
        ht_ref[c["d"], c["g"]] = c["ht"] * c["dend"] + jnp.where(bd, upd, 0.0)
    ng = RWKV_WIDTH // gw
    y0_ref[0] = jnp.concatenate([c["y"] for c in recs[:ng]], axis=1)
    y1_ref[0] = jnp.concatenate([c["y"] for c in recs[ng:]], axis=1)


def _split_dot_left(w_bf16, x):
    hi = x.astype(BF16)
    lo = (x - hi.astype(F32)).astype(BF16)
    return _dot(w_bf16, hi) + _dot(w_bf16, lo)


def _scan(r, v, a, lw0, lw1, k0, k1, b0, b1):
    b, s, c = r.shape
    nc = s // CHUNK
    fwd = pl.BlockSpec((1, CHUNK, c), lambda bi, i: (bi, i, 0))
    bwd = pl.BlockSpec((1, CHUNK, c), lambda bi, i: (bi, nc - 1 - i, 0))
    ng = c // GROUP_W
    return pl.pallas_call(
        _scan_kernel,
        grid=(b, nc),
        in_specs=[fwd] * 6 + [bwd] * 6,
        out_specs=[fwd, bwd],
        out_shape=[jax.ShapeDtypeStruct((b, s, c), F32)] * 2,
        scratch_shapes=[pltpu.VMEM((2, ng, GROUP_W, GROUP_W), F32)],
        compiler_params=_cparams(("parallel", "arbitrary")),
        name="scan",
    )(r, v, a, lw0, k0, b0, r, v, a, lw1, k1, b1)


def _merge_kernel(x_ref, gain_ref, wg_ref, attn_ref, y0_ref, y1_ref, bonus_ref, g_ref, lnw_ref, lnb_ref,
                  bd_ref, wua_ref, wur_ref, wout_ref, o_ref):
    x = x_ref[0]
    d = x.shape[1]
    h = _rms(x, gain_ref[...]).astype(BF16)
    gates = _sigmoid(_dot(h, wg_ref[...]))
    bd = bd_ref[...]
    y = y0_ref[0] + y1_ref[0]
    yc = y - _head_sum(y, bd) * (1.0 / HEAD_DIM)
    var = _head_sum(yc * yc, bd) * (1.0 / HEAD_DIM)
    yn = yc * lax.rsqrt(var + GN_EPS) * lnw_ref[...] + lnb_ref[...]
    rw = (yn + bonus_ref[0]) * g_ref[0]
    merged = (gates[:, :d] * _dot(attn_ref[0], wua_ref[...])
              + gates[:, d:] * _dot(rw.astype(BF16), wur_ref[...]))
    o_ref[0] = x + _dot(merged.astype(BF16), wout_ref[...])


def _merge(x, gain, wg, attn, y0, y1, bonus, g, lnw, lnb, bd, wua, wur, wout, tm):
    b, s, d = x.shape
    c = RWKV_WIDTH
    const = lambda shape: pl.BlockSpec(shape, lambda bi, i: (0,) * len(shape))
    tok = lambda width: pl.BlockSpec((1, tm, width), lambda bi, i: (bi, i, 0))
    return pl.pallas_call(
        _merge_kernel,
        grid=(b, s // tm),
        in_specs=[tok(d), const((1, d)), const((d, 2 * d)), tok(ATTN_WIDTH), tok(c), tok(c), tok(c), tok(c),
                  const((1, c)), const((1, c)), const((LANES, LANES)), const((ATTN_WIDTH, d)), const((c, d)),
                  const((d, d))],
        out_specs=tok(d),
        out_shape=jax.ShapeDtypeStruct((b, s, d), F32),
        compiler_params=_cparams(("parallel", "parallel")),
        name="merge",
    )(x, gain, wg, attn, y0, y1, bonus, g, lnw, lnb, bd, wua, wur, wout)


FF_SLAB = 1024


def _ffn_kernel(x_ref, p_ref, gf_ref, w1_ref, w2_ref, gp_ref, wpg_ref, wple_ref, o_ref):
    x = x_ref[0]
    h = _rms(x, gf_ref[...]).astype(BF16)
    acc = x
    for j in range(0, w1_ref.shape[1], FF_SLAB):
        hid = jnp.maximum(_dot(h, w1_ref[:, j:j + FF_SLAB]), 0.0)
        acc = acc + _dot((hid * hid).astype(BF16), w2_ref[j:j + FF_SLAB, :])
    hp = _rms(acc, gp_ref[...]).astype(BF16)
    o_ref[0] = acc + _dot(p_ref[0].astype(BF16), wple_ref[...]) * _sigmoid(_dot(hp, wpg_ref[...]))


def _ffn(x, p, gf, w1, w2, gp, wpg, wple, tm):
    b, s, d = x.shape
    dff = w1.shape[1]
    pd = p.shape[-1]
    const = lambda shape: pl.BlockSpec(shape, lambda bi, i: (0,) * len(shape))
    tok = lambda width: pl.BlockSpec((1, tm, width), lambda bi, i: (bi, i, 0))
    return pl.pallas_call(
        _ffn_kernel,
        grid=(b, s // tm),
        in_specs=[tok(d), tok(pd), const((1, d)), const((d, dff)), const((dff, d)), const((1, d)),
                  const((d, d)), const((pd, d))],
        out_specs=tok(d),
        out_shape=jax.ShapeDtypeStruct((b, s, d), F32),
        compiler_params=_cparams(("parallel", "parallel")),
        name="ffn",
    )(x, p, gf, w1, w2, gp, wpg, wple)


def _rotary_tables(s):
    half = ROT_DIM // 2
    inv_freq = jnp.power(jnp.float32(ROPE_THETA), -jnp.arange(half, dtype=F32) * 2.0 / ROT_DIM)
    ang = jnp.arange(s).astype(F32)[:, None] * inv_freq[None, :]
    cos, sin = jnp.cos(ang), jnp.sin(ang)
    pad = jnp.zeros((s, HEAD_DIM - ROT_DIM), F32)
    zero = jnp.zeros((s, half), F32)
    c = jnp.concatenate([cos, cos, pad + 1.0], axis=1)
    s1 = jnp.concatenate([-sin, zero, pad], axis=1)
    s2 = jnp.concatenate([zero, sin, pad], axis=1)
    rep = LANES // HEAD_DIM
    return jnp.tile(c, (1, rep)), jnp.tile(s1, (1, rep)), jnp.tile(s2, (1, rep))


def _swap_halves(w):
    return jnp.concatenate([w[:, HEAD_DIM:], w[:, :HEAD_DIM]], axis=1)


def kernel(x, p, norm_mix, w_in, shift_mu, q_norm, k_norm, sink, w0, w2, a0, a2, g2, k_k, k_a, r_k, lnx_w, lnx_b,
           w_up_attn, w_up_rwkv, w_out, norm_ffn, w_ff1, w_ff2, norm_ple, w_ple_gate, w_ple):
    bsz, s, d = x.shape
    depth = w_in.shape[0]
    c = RWKV_WIDTH
    tm = min(512, s)
    tt = min(256, s)
    cos, s1, s2 = _rotary_tables(s)
    lane = jnp.arange(LANES)
    bd = ((lane[:, None] // HEAD_DIM) == (lane[None, :] // HEAD_DIM)).astype(BF16)
    o_k, o_v, o_r, o_g = ATTN_WIDTH, ATTN_WIDTH + KV_WIDTH, ATTN_WIDTH + 2 * KV_WIDTH, ATTN_WIDTH + 2 * KV_WIDTH + RWKV_COLS
    for i in range(depth):
        wi = w_in[i]
        wk, wv = wi[:, o_k:o_v], wi[:, o_v:o_r]
        w_proj = jnp.concatenate([wi[:, :o_k], wk, _swap_halves(wk), wv, _swap_halves(wv), wi[:, o_r:o_g]],
                                 axis=1).astype(BF16)
        qkg = jnp.concatenate([jnp.tile(q_norm[i], Q_HEADS), jnp.tile(k_norm[i], 2 * KV_HEADS)])[None, :]
        q, k2, v2, zr = _proj(x, norm_mix[i][None, :], w_proj, cos, s1, s2, qkg, bd, tm)
        attn = _attn(sink[i], q, k2, v2)

        zpad = jnp.zeros((DECAY_RANK, c), F32)
        wlo = jnp.concatenate([jnp.concatenate([w2[i, 0], zpad], axis=0), jnp.concatenate([w2[i, 1], zpad], axis=0),
                               jnp.concatenate([zpad, a2[i, 0]], axis=0), jnp.concatenate([zpad, a2[i, 1]], axis=0)],
                              axis=1).astype(BF16)
        (r, v, a, lw0, lw1, k0, k1, b0, b1, g, bonus) = _prep(
            zr, shift_mu[i][None, :], wlo, g2[i].astype(BF16), w0[i], a0[i], k_k[i][None, :], k_a[i][None, :],
            r_k[i].reshape(2, c), bd, tt)
        y0, y1 = _scan(r, v, a, lw0, lw1, k0, k1, b0, b1)

        x = _merge(x, norm_mix[i][None, :], wi[:, o_g:].astype(BF16), attn, y0, y1, bonus, g,
                   lnx_w[i][None, :], lnx_b[i][None, :], bd, w_up_attn[i].astype(BF16), w_up_rwkv[i].astype(BF16),
                   w_out[i].astype(BF16), tm)
        x = _ffn(x, p[i], norm_ffn[i][None, :], w_ff1[i].astype(BF16), w_ff2[i].astype(BF16), norm_ple[i][None, :],
                 w_ple_gate[i].astype(BF16), w_ple[i].astype(BF16), tm)
    return x
```

```python
import functools
import math

import jax
import jax.numpy as jnp
from jax import lax
from jax.experimental import pallas as pl
from jax.experimental.pallas import tpu as pltpu

F32 = jnp.float32
BF16 = jnp.bfloat16

LANES = 128
HEAD_DIM = 64
Q_HEADS = 8
KV_HEADS = 2
ATTN_WIDTH = Q_HEADS * HEAD_DIM
KV_WIDTH = KV_HEADS * HEAD_DIM
WINDOW = 128
ATTN_BLOCK = 128
ROPE_THETA = 500000.0
ROT_DIM = HEAD_DIM // 4
RWKV_WIDTH = 512
DECAY_RANK = 64
ICLR_RANK = 64
GATE_RANK = 128
RWKV_COLS = 3 * RWKV_WIDTH + DECAY_RANK + ICLR_RANK + GATE_RANK
NORM_EPS = 1e-6
GN_EPS = 64e-5
CHUNK = 64
GROUP_W = 128
SCAN_CHUNKS = 4
VMEM_LIMIT = 56 * 1024 * 1024


def _cparams(sem):
    return pltpu.CompilerParams(dimension_semantics=sem, vmem_limit_bytes=VMEM_LIMIT)


def _dot(a, b):
    return jnp.dot(a, b, preferred_element_type=F32)


def _dot_nt(a, b):
    return lax.dot_general(a, b, (((1,), (1,)), ((), ())), preferred_element_type=F32)


def _dot_tn(a, b):
    return lax.dot_general(a, b, (((0,), (0,)), ((), ())), preferred_element_type=F32)


def _split_dot(x, w_bf16):
    hi = x.astype(BF16)
    lo = (x - hi.astype(F32)).astype(BF16)
    return _dot(hi, w_bf16) + _dot(lo, w_bf16)


def _head_sum(x, bd):
    cols = [_split_dot(x[:, i:i + LANES], bd) for i in range(0, x.shape[1], LANES)]
    return cols[0] if len(cols) == 1 else jnp.concatenate(cols, axis=1)


def _rms(x, gain):
    return x * lax.rsqrt(jnp.mean(x * x, axis=-1, keepdims=True) + NORM_EPS) * gain


def _sigmoid(x):
    return 1.0 / (1.0 + jnp.exp(-x))


QK_W = ATTN_WIDTH + 2 * KV_WIDTH
PROJ_W = QK_W + 2 * KV_WIDTH + RWKV_COLS


def _proj_kernel(x_ref, g_ref, w_ref, cos_ref, s1_ref, s2_ref, qkg_ref, bd_ref,
                 q_ref, k_ref, v_ref, zr_ref):
    h = _rms(x_ref[0], g_ref[...]).astype(BF16)
    proj = _dot(h, w_ref[...])
    qk = proj[:, :QK_W]
    ms = _head_sum(qk * qk, bd_ref[...]) * (1.0 / HEAD_DIM)
    qk = qk * lax.rsqrt(ms + NORM_EPS) * qkg_ref[...]
    cos, s1, s2 = cos_ref[...], s1_ref[...], s2_ref[...]
    tiles = []
    for i in range(0, QK_W, LANES):
        t = qk[:, i:i + LANES]
        t = t * cos + pltpu.roll(t, LANES - ROT_DIM // 2, 1) * s1 + pltpu.roll(t, ROT_DIM // 2, 1) * s2
        tiles.append(t)
    q_ref[0] = (jnp.concatenate(tiles[:4], axis=1) * (HEAD_DIM ** -0.5)).astype(BF16)
    k_ref[0] = jnp.concatenate(tiles[4:], axis=1).astype(BF16)
    v_ref[0] = proj[:, QK_W:QK_W + 2 * KV_WIDTH].astype(BF16)
    zr_ref[0] = proj[:, QK_W + 2 * KV_WIDTH:]


def _proj(x, gain, w, cos, s1, s2, qkg, bd, tm):
    b, s, d = x.shape
    const = lambda shape: pl.BlockSpec(shape, lambda bi, i: (0,) * len(shape))
    tok = lambda width: pl.BlockSpec((1, tm, width), lambda bi, i: (bi, i, 0))
    tab = pl.BlockSpec((tm, LANES), lambda bi, i: (i, 0))
    return pl.pallas_call(
        _proj_kernel,
        grid=(b, s // tm),
        in_specs=[tok(d), const((1, d)), const((d, PROJ_W)), tab, tab, tab, const((1, QK_W)),
                  const((LANES, LANES))],
        out_specs=[tok(ATTN_WIDTH), tok(2 * KV_WIDTH), tok(2 * KV_WIDTH), tok(RWKV_COLS)],
        out_shape=[jax.ShapeDtypeStruct((b, s, ATTN_WIDTH), BF16),
                   jax.ShapeDtypeStruct((b, s, 2 * KV_WIDTH), BF16),
                   jax.ShapeDtypeStruct((b, s, 2 * KV_WIDTH), BF16),
                   jax.ShapeDtypeStruct((b, s, RWKV_COLS), F32)],
        compiler_params=_cparams(("parallel", "parallel")),
        name="proj",
    )(x, gain, w, cos, s1, s2, qkg, bd)


def _attn_kernel(sink_ref, q_ref, kp_ref, kc_ref, kn_ref, vp_ref, vc_ref, vn_ref, o_ref):
    n = pl.program_id(1)
    nb = pl.num_programs(1)
    blk = ATTN_BLOCK
    k = jnp.concatenate([kp_ref[0], kc_ref[0], kn_ref[0]], axis=0)
    v = jnp.concatenate([vp_ref[0], vc_ref[0], vn_ref[0]], axis=0)
    qi = lax.broadcasted_iota(jnp.int32, (2 * blk, 3 * blk), 0) % blk
    ki = lax.broadcasted_iota(jnp.int32, (2 * blk, 3 * blk), 1)
    valid = (ki >= qi) & (ki <= qi + 2 * WINDOW)
    valid = valid & ((ki >= blk) | (n > 0)) & ((ki < 2 * blk) | (n < nb - 1))
    first = lax.broadcasted_iota(jnp.int32, (2 * blk, 1), 0) < blk
    left = lax.broadcasted_iota(jnp.int32, (1, LANES), 1) < HEAD_DIM
    zero = jnp.zeros((), BF16)
    recs = []
    for g in range(KV_HEADS):
        pairs = jnp.concatenate([q_ref[0, :, p * LANES:(p + 1) * LANES] for p in (2 * g, 2 * g + 1)], axis=0)
        for side in range(2):
            off = LANES * ((g + side) % 2)
            keep = left if side == 0 else jnp.logical_not(left)
            sk = jnp.where(first, sink_ref[4 * g + side], sink_ref[4 * g + 2 + side])
            recs.append(dict(g=g, q=jnp.where(keep, pairs, zero), k=k[:, off:off + LANES],
                             v=jnp.where(keep, v[:, off:off + LANES], zero), sk=sk))
    for c in recs:
        c["sc"] = jnp.where(valid, _dot_nt(c["q"], c["k"]), -1e30)
    for c in recs:
        m = jnp.maximum(jnp.max(c["sc"], axis=-1, keepdims=True), c["sk"])
        e = jnp.exp(c["sc"] - m)
        c["den"] = jnp.sum(e, axis=-1, keepdims=True) + jnp.exp(c["sk"] - m)
        c["e"] = e.astype(BF16)
    for c in recs:
        c["o"] = _dot(c["e"], c["v"]) / c["den"]
    outs = []
    for g in range(KV_HEADS):
        both = recs[2 * g]["o"] + recs[2 * g + 1]["o"]
        outs += [both[:blk], both[blk:]]
    o_ref[0] = jnp.concatenate(outs, axis=1).astype(BF16)


def _attn(sink, q, k2, v2):
    b, s, _ = q.shape
    nb = s // ATTN_BLOCK
    kv = lambda fn: pl.BlockSpec((1, ATTN_BLOCK, 2 * KV_WIDTH), fn)
    prev = lambda bi, n: (bi, jnp.maximum(n - 1, 0), 0)
    cur = lambda bi, n: (bi, n, 0)
    nxt = lambda bi, n: (bi, jnp.minimum(n + 1, nb - 1), 0)
    return pl.pallas_call(
        _attn_kernel,
        grid=(b, nb),
        in_specs=[pl.BlockSpec(memory_space=pltpu.SMEM),
                  pl.BlockSpec((1, ATTN_BLOCK, ATTN_WIDTH), cur),
                  kv(prev), kv(cur), kv(nxt), kv(prev), kv(cur), kv(nxt)],
        out_specs=pl.BlockSpec((1, ATTN_BLOCK, ATTN_WIDTH), cur),
        out_shape=jax.ShapeDtypeStruct((b, s, ATTN_WIDTH), BF16),
        compiler_params=_cparams(("parallel", "parallel")),
        name="attn",
    )(sink, q, k2, k2, k2, v2, v2, v2)


HALO = 8


def _prep_kernel(z_ref, zp_ref, zn_ref, mu_ref, wlo_ref, g2_ref, w0_ref, a0_ref, kk_ref, ka_ref, rk_ref,
                 bd_ref, r_ref, v_ref, a_ref, lw0_ref, lw1_ref, k0_ref, k1_ref, b0_ref, b1_ref,
                 g_ref, bonus_ref):
    i = pl.program_id(1)
    nt = pl.num_programs(1)
    z = z_ref[0]
    tt = z.shape[0]
    row = lax.broadcasted_iota(jnp.int32, (tt, 1), 0)
    prev_row = jnp.where(i > 0, zp_ref[0, HALO - 1:HALO, :], 0.0)
    next_row = jnp.where(i < nt - 1, zn_ref[0, 0:1, :], 0.0)
    prev = jnp.where(row == 0, prev_row, pltpu.roll(z, 1, 0))
    nxt = jnp.where(row == tt - 1, next_row, pltpu.roll(z, tt - 1, 0))
    z = z + mu_ref[...] * (0.5 * (prev + nxt) - z)

    c = RWKV_WIDTH
    r, k, v = z[:, :c], z[:, c:2 * c], z[:, 2 * c:3 * c]
    lowrank = z[:, 3 * c:3 * c + LANES]
    left = lax.broadcasted_iota(jnp.int32, (1, LANES), 1) < DECAY_RANK
    lowrank = jnp.where(left, jnp.tanh(lowrank), lowrank).astype(BF16)
    wa = _dot(lowrank, wlo_ref[...])
    g_ref[0] = _dot(_sigmoid(z[:, 3 * c + LANES:]).astype(BF16), g2_ref[...])

    bd = bd_ref[...]
    kk = k * kk_ref[...]
    nrm = jnp.sqrt(_head_sum(kk * kk, bd))
    kk = kk / jnp.maximum(nrm, 1e-12)
    r_ref[0] = r
    v_ref[0] = v
    a_ref[0] = -kk
    dot_rk = None
    for d, (lw_ref, k_ref, b_ref) in enumerate(((lw0_ref, k0_ref, b0_ref), (lw1_ref, k1_ref, b1_ref))):
        w_raw = w0_ref[d:d + 1, :] + wa[:, d * c:(d + 1) * c]
        lw_ref[0] = -math.exp(-0.5) * _sigmoid(w_raw)
        rate = _sigmoid(a0_ref[d:d + 1, :] + wa[:, (2 + d) * c:(3 + d) * c])
        k_dir = k * (1.0 + (rate - 1.0) * ka_ref[...])
        k_ref[0] = k_dir
        b_ref[0] = kk * rate
        term = k_dir * rk_ref[d:d + 1, :]
        dot_rk = term if dot_rk is None else dot_rk + term
    bonus_ref[0] = _head_sum(r * dot_rk, bd) * v


def _prep(zr, mu, wlo, g2, w0, a0, k_k, k_a, r_k, bd, tt):
    b, s, _ = zr.shape
    nh = tt // HALO
    const = lambda shape: pl.BlockSpec(shape, lambda bi, i: (0,) * len(shape))
    tok = lambda width: pl.BlockSpec((1, tt, width), lambda bi, i: (bi, i, 0))
    halo_p = pl.BlockSpec((1, HALO, RWKV_COLS), lambda bi, i: (bi, jnp.maximum(i * nh - 1, 0), 0))
    halo_n = pl.BlockSpec((1, HALO, RWKV_COLS), lambda bi, i: (bi, jnp.minimum((i + 1) * nh, s // HALO - 1), 0))
    c = RWKV_WIDTH
    return pl.pallas_call(
        _prep_kernel,
        grid=(b, s // tt),
        in_specs=[tok(RWKV_COLS), halo_p, halo_n, const((1, RWKV_COLS)), const((LANES, 4 * c)),
                  const((GATE_RANK, c)), const((2, c)), const((2, c)), const((1, c)), const((1, c)),
                  const((2, c)), const((LANES, LANES))],
        out_specs=[tok(c)] * 11,
        out_shape=[jax.ShapeDtypeStruct((b, s, c), F32)] * 11,
        compiler_params=_cparams(("parallel", "parallel")),
        name="prep",
    )(zr, zr, zr, mu, wlo, g2, w0, a0, k_k, k_a, r_k, bd)


def _scan_kernel(r0_ref, v0_ref, a0_ref, lw0_ref, k0_ref, b0_ref,
                 r1_ref, v1_ref, a1_ref, lw1_ref, k1_ref, b1_ref,
                 y0_ref, y1_ref, ht_ref):
    L, gw, nch = CHUNK, GROUP_W, SCAN_CHUNKS
    rep = gw // L
    ng = RWKV_WIDTH // gw
    rows = nch * L

    @pl.when(pl.program_id(1) == 0)
    def _():
        ht_ref[...] = jnp.zeros_like(ht_ref)

    tau = lax.broadcasted_iota(jnp.int32, (L, gw), 0)
    sig = lax.broadcasted_iota(jnp.int32, (L, gw), 1) % L
    bd = (lax.broadcasted_iota(jnp.int32, (gw, gw), 0) // L) == (lax.broadcasted_iota(jnp.int32, (gw, gw), 1) // L)
    ti = lax.broadcasted_iota(jnp.int32, (rows, rows), 0)
    si = lax.broadcasted_iota(jnp.int32, (rows, rows), 1)
    same_chunk = (ti // L) == (si // L)

    def bdtile(x):
        xb = x.astype(BF16)
        return jnp.where(bd, jnp.concatenate([xb] * rep, axis=0), jnp.zeros((), BF16))

    dirs = ((r0_ref, v0_ref, a0_ref, lw0_ref, k0_ref, b0_ref),
            (r1_ref, v1_ref, a1_ref, lw1_ref, k1_ref, b1_ref))
    recs = {}
    for d, (r_ref, v_ref, a_ref, lw_ref, k_ref, b_ref) in enumerate(dirs):
        if d == 0:
            tri = (same_chunk & (si <= ti)).astype(BF16)
            strict, incl = sig < tau, sig <= tau
        else:
            tri = (same_chunk & (si >= ti)).astype(BF16)
            strict, incl = sig > tau, sig >= tau
        lw = lw_ref[0]
        cl = _split_dot_left(tri, lw)
        e_in = jnp.exp(cl)
        e_inv = jnp.exp(-cl)
        e_ex = jnp.exp(cl - lw)
        a_t = a_ref[0] * e_ex
        r_t = r_ref[0] * e_in
        b_t = b_ref[0] * e_inv
        k_t = k_ref[0] * e_inv
        v_all = v_ref[0]
        for j in range(nch):
            rs = slice(j * L, (j + 1) * L)
            last = (j + 1) * L - 1 if d == 0 else j * L
            d_end = jnp.exp(cl[last:last + 1, :])
            b_h = b_t[rs] * d_end
            k_h = k_t[rs] * d_end
            for g in range(ng):
                sl = slice(g * gw, (g + 1) * gw)
                recs[(j, d, g)] = dict(strict=strict, incl=incl, at=a_t[rs, sl], rt=r_t[rs, sl], vv=v_all[rs, sl],
                                       bt=b_t[rs, sl], kt=k_t[rs, sl], bh=b_h[:, sl], kh=k_h[:, sl],
                                       dend=d_end[:, sl])
    par = list(recs.values())
    for c in par:
        c["lhs"] = jnp.concatenate([c["at"], c["rt"]], axis=0).astype(BF16)
    for c in par:
        c["gb"] = _dot_nt(c["lhs"], bdtile(c["bt"]))
    for c in par:
        c["gk"] = _dot_nt(c["lhs"], bdtile(c["kt"]))
    for c in par:
        c["m_rb"] = jnp.where(c["incl"], c["gb"][L:], 0.0).astype(BF16)
        c["m_rk"] = jnp.where(c["incl"], c["gk"][L:], 0.0).astype(BF16)
        c["p"] = jnp.where(c["strict"], c["gb"][:L], 0.0)
        c["v_bd"] = bdtile(c["vv"])
    for c in par:
        c["makv"] = _dot(jnp.where(c["strict"], c["gk"][:L], 0.0).astype(BF16), c["v_bd"])
    levels = int(math.log2(L))
    for lvl in range(levels):
        for c in par:
            ops = ([] if lvl == 0 else [bdtile(c["t"])]) + ([] if lvl == levels - 1 else [bdtile(c["p"])])
            c["res"] = _dot(c["p"].astype(BF16), ops[0] if len(ops) == 1 else jnp.concatenate(ops, axis=1))
        for c in par:
            if lvl == 0:
                c["t"] = jnp.where(sig == tau, 1.0, c["p"])
                c["p"] = c["res"]
            else:
                c["t"] = c["t"] + c["res"][:, :gw]
                if lvl < levels - 1:
                    c["p"] = c["res"][:, gw:]
    state = {(d, g): ht_ref[d, g] for d in range(2) for g in range(ng)}
    for step in range(nch):
        cur = [(d, g, recs[(step if d == 0 else nch - 1 - step, d, g)]) for d in range(2) for g in range(ng)]
        for d, g, c in cur:
            c["htb"] = state[(d, g)].astype(BF16)
        for d, g, c in cur:
            c["rhs"] = c["makv"] + _dot_nt(c["at"].astype(BF16), c["htb"])
        for d, g, c in cur:
            c["u"] = _dot(c["t"].astype(BF16), bdtile(c["rhs"]))
        for d, g, c in cur:
            upd = _dot_tn(jnp.concatenate([c["u"], c["vv"]], axis=0).astype(BF16),
                          jnp.concatenate([c["bh"], c["kh"]], axis=0).astype(BF16))
            state[(d, g)] = state[(d, g)] * c["dend"] + jnp.where(bd, upd, 0.0)
        for d, g, c in cur:
            c["y"] = (_dot_nt(c["rt"].astype(BF16), c["htb"]) + _dot(c["m_rb"], bdtile(c["u"]))
                      + _dot(c["m_rk"], c["v_bd"]))
    for (d, g), h in state.items():
        ht_ref[d, g] = h
    for d, y_ref in enumerate((y0_ref, y1_ref)):
        y_ref[0] = jnp.concatenate(
            [jnp.concatenate([recs[(j, d, g)]["y"] for g in range(ng)], axis=1) for j in range(nch)], axis=0)


def _split_dot_left(w_bf16, x):
    hi = x.astype(BF16)
    lo = (x - hi.astype(F32)).astype(BF16)
    return _dot(w_bf16, hi) + _dot(w_bf16, lo)


def _scan(r, v, a, lw0, lw1, k0, k1, b0, b1):
    b, s, c = r.shape
    rows = CHUNK * SCAN_CHUNKS
    nblk = s // rows
    fwd = pl.BlockSpec((1, rows, c), lambda bi, i: (bi, i, 0))
    bwd = pl.BlockSpec((1, rows, c), lambda bi, i: (bi, nblk - 1 - i, 0))
    ng = c // GROUP_W
    return pl.pallas_call(
        _scan_kernel,
        grid=(b, nblk),
        in_specs=[fwd] * 6 + [bwd] * 6,
        out_specs=[fwd, bwd],
        out_shape=[jax.ShapeDtypeStruct((b, s, c), F32)] * 2,
        scratch_shapes=[pltpu.VMEM((2, ng, GROUP_W, GROUP_W), F32)],
        compiler_params=_cparams(("parallel", "arbitrary")),
        name="scan",
    )(r, v, a, lw0, k0, b0, r, v, a, lw1, k1, b1)


def _merge_kernel(x_ref, gain_ref, wg_ref, attn_ref, y0_ref, y1_ref, bonus_ref, g_ref, lnw_ref, lnb_ref,
                  bd_ref, wua_ref, wur_ref, wout_ref, o_ref):
    x = x_ref[0]
    d = x.shape[1]
    h = _rms(x, gain_ref[...]).astype(BF16)
    gates = _sigmoid(_dot(h, wg_ref[...]))
    bd = bd_ref[...]
    y = y0_ref[0] + y1_ref[0]
    yc = y - _head_sum(y, bd) * (1.0 / HEAD_DIM)
    var = _head_sum(yc * yc, bd) * (1.0 / HEAD_DIM)
    yn = yc * lax.rsqrt(var + GN_EPS) * lnw_ref[...] + lnb_ref[...]
    rw = (yn + bonus_ref[0]) * g_ref[0]
    merged = (gates[:, :d] * _dot(attn_ref[0], wua_ref[...])
              + gates[:, d:] * _dot(rw.astype(BF16), wur_ref[...]))
    o_ref[0] = x + _dot(merged.astype(BF16), wout_ref[...])


def _merge(x, gain, wg, attn, y0, y1, bonus, g, lnw, lnb, bd, wua, wur, wout, tm):
    b, s, d = x.shape
    c = RWKV_WIDTH
    const = lambda shape: pl.BlockSpec(shape, lambda bi, i: (0,) * len(shape))
    tok = lambda width: pl.BlockSpec((1, tm, width), lambda bi, i: (bi, i, 0))
    return pl.pallas_call(
        _merge_kernel,
        grid=(b, s // tm),
        in_specs=[tok(d), const((1, d)), const((d, 2 * d)), tok(ATTN_WIDTH), tok(c), tok(c), tok(c), tok(c),
                  const((1, c)), const((1, c)), const((LANES, LANES)), const((ATTN_WIDTH, d)), const((c, d)),
                  const((d, d))],
        out_specs=tok(d),
        out_shape=jax.ShapeDtypeStruct((b, s, d), F32),
        compiler_params=_cparams(("parallel", "parallel")),
        name="merge",
    )(x, gain, wg, attn, y0, y1, bonus, g, lnw, lnb, bd, wua, wur, wout)


FF_SLAB = 1024


def _ffn_kernel(x_ref, p_ref, gf_ref, w1_ref, w2_ref, gp_ref, wpg_ref, wple_ref, o_ref):
    x = x_ref[0]
    h = _rms(x, gf_ref[...]).astype(BF16)
    acc = x
    for j in range(0, w1_ref.shape[1], FF_SLAB):
        hid = jnp.maximum(_dot(h, w1_ref[:, j:j + FF_SLAB]), 0.0)
        acc = acc + _dot((hid * hid).astype(BF16), w2_ref[j:j + FF_SLAB, :])
    hp = _rms(acc, gp_ref[...]).astype(BF16)
    o_ref[0] = acc + _dot(p_ref[0].astype(BF16), wple_ref[...]) * _sigmoid(_dot(hp, wpg_ref[...]))


def _ffn(x, p, gf, w1, w2, gp, wpg, wple, tm):
    b, s, d = x.shape
    dff = w1.shape[1]
    pd = p.shape[-1]
    const = lambda shape: pl.BlockSpec(shape, lambda bi, i: (0,) * len(shape))
    tok = lambda width: pl.BlockSpec((1, tm, width), lambda bi, i: (bi, i, 0))
    return pl.pallas_call(
        _ffn_kernel,
        grid=(b, s // tm),
        in_specs=[tok(d), tok(pd), const((1, d)), const((d, dff)), const((dff, d)), const((1, d)),
                  const((d, d)), const((pd, d))],
        out_specs=tok(d),
        out_shape=jax.ShapeDtypeStruct((b, s, d), F32),
        compiler_params=_cparams(("parallel", "parallel")),
        name="ffn",
    )(x, p, gf, w1, w2, gp, wpg, wple)


def _rotary_tables(s):
    half = ROT_DIM // 2
    inv_freq = jnp.power(jnp.float32(ROPE_THETA), -jnp.arange(half, dtype=F32) * 2.0 / ROT_DIM)
    ang = jnp.arange(s).astype(F32)[:, None] * inv_freq[None, :]
    cos, sin = jnp.cos(ang), jnp.sin(ang)
    pad = jnp.zeros((s, HEAD_DIM - ROT_DIM), F32)
    zero = jnp.zeros((s, half), F32)
    c = jnp.concatenate([cos, cos, pad + 1.0], axis=1)
    s1 = jnp.concatenate([-sin, zero, pad], axis=1)
    s2 = jnp.concatenate([zero, sin, pad], axis=1)
    rep = LANES // HEAD_DIM
    return jnp.tile(c, (1, rep)), jnp.tile(s1, (1, rep)), jnp.tile(s2, (1, rep))


def _swap_halves(w):
    return jnp.concatenate([w[:, HEAD_DIM:], w[:, :HEAD_DIM]], axis=1)


def kernel(x, p, norm_mix, w_in, shift_mu, q_norm, k_norm, sink, w0, w2, a0, a2, g2, k_k, k_a, r_k, lnx_w, lnx_b,
           w_up_attn, w_up_rwkv, w_out, norm_ffn, w_ff1, w_ff2, norm_ple, w_ple_gate, w_ple):
    bsz, s, d = x.shape
    depth = w_in.shape[0]
    c = RWKV_WIDTH
    tm = min(512, s)
    tt = min(256, s)
    cos, s1, s2 = _rotary_tables(s)
    lane = jnp.arange(LANES)
    bd = ((lane[:, None] // HEAD_DIM) == (lane[None, :] // HEAD_DIM)).astype(BF16)
    o_k, o_v, o_r, o_g = ATTN_WIDTH, ATTN_WIDTH + KV_WIDTH, ATTN_WIDTH + 2 * KV_WIDTH, ATTN_WIDTH + 2 * KV_WIDTH + RWKV_COLS
    for i in range(depth):
        wi = w_in[i]
        wk, wv = wi[:, o_k:o_v], wi[:, o_v:o_r]
        w_proj = jnp.concatenate([wi[:, :o_k], wk, _swap_halves(wk), wv, _swap_halves(wv), wi[:, o_r:o_g]],
                                 axis=1).astype(BF16)
        qkg = jnp.concatenate([jnp.tile(q_norm[i], Q_HEADS), jnp.tile(k_norm[i], 2 * KV_HEADS)])[None, :]
        q, k2, v2, zr = _proj(x, norm_mix[i][None, :], w_proj, cos, s1, s2, qkg, bd, tm)
        attn = _attn(sink[i], q, k2, v2)

        zpad = jnp.zeros((DECAY_RANK, c), F32)
        wlo = jnp.concatenate([jnp.concatenate([w2[i, 0], zpad], axis=0), jnp.concatenate([w2[i, 1], zpad], axis=0),
                               jnp.concatenate([zpad, a2[i, 0]], axis=0), jnp.concatenate([zpad, a2[i, 1]], axis=0)],
                              axis=1).astype(BF16)
        (r, v, a, lw0, lw1, k0, k1, b0, b1, g, bonus) = _prep(
            zr, shift_mu[i][None, :], wlo, g2[i].astype(BF16), w0[i], a0[i], k_k[i][None, :], k_a[i][None, :],
            r_k[i].reshape(2, c), bd, tt)
        y0, y1 = _scan(r, v, a, lw0, lw1, k0, k1, b0, b1)

        x = _merge(x, norm_mix[i][None, :], wi[:, o_g:].astype(BF16), attn, y0, y1, bonus, g,
                   lnx_w[i][None, :], lnx_b[i][None, :], bd, w_up_attn[i].astype(BF16), w_up_rwkv[i].astype(BF16),
                   w_out[i].astype(BF16), tm)
        x = _ffn(x, p[i], norm_ffn[i][None, :], w_ff1[i].astype(BF16), w_ff2[i].astype(BF16), norm_ple[i][None, :],
                 w_ple_gate[i].astype(BF16), w_ple[i].astype(BF16), tm)
    return x
```

```python
import functools
import math

import jax
import jax.numpy as jnp
from jax import lax
from jax.experimental import pallas as pl
from jax.experimental.pallas import tpu as pltpu

F32 = jnp.float32
BF16 = jnp.bfloat16

LANES = 128
HEAD_DIM = 64
Q_HEADS = 8
KV_HEADS = 2
ATTN_WIDTH = Q_HEADS * HEAD_DIM
KV_WIDTH = KV_HEADS * HEAD_DIM
WINDOW = 128
ATTN_BLOCK = 128
ROPE_THETA = 500000.0
ROT_DIM = HEAD_DIM // 4
RWKV_WIDTH = 512
DECAY_RANK = 64
ICLR_RANK = 64
GATE_RANK = 128
RWKV_COLS = 3 * RWKV_WIDTH + DECAY_RANK + ICLR_RANK + GATE_RANK
NORM_EPS = 1e-6
GN_EPS = 64e-5
CHUNK = 64
GROUP_W = 128
SCAN_CHUNKS = 4
VMEM_LIMIT = 56 * 1024 * 1024


def _cparams(sem):
    return pltpu.CompilerParams(dimension_semantics=sem, vmem_limit_bytes=VMEM_LIMIT)


def _dot(a, b):
    return jnp.dot(a, b, preferred_element_type=F32)


def _dot_nt(a, b):
    return lax.dot_general(a, b, (((1,), (1,)), ((), ())), preferred_element_type=F32)


def _dot_tn(a, b):
    return lax.dot_general(a, b, (((0,), (0,)), ((), ())), preferred_element_type=F32)


def _split_dot(x, w_bf16):
    hi = x.astype(BF16)
    lo = (x - hi.astype(F32)).astype(BF16)
    return _dot(hi, w_bf16) + _dot(lo, w_bf16)


def _head_sum(x, bd):
    cols = [_split_dot(x[:, i:i + LANES], bd) for i in range(0, x.shape[1], LANES)]
    return cols[0] if len(cols) == 1 else jnp.concatenate(cols, axis=1)


def _rms(x, gain):
    return x * lax.rsqrt(jnp.mean(x * x, axis=-1, keepdims=True) + NORM_EPS) * gain


def _sigmoid(x):
    return 1.0 / (1.0 + jnp.exp(-x))


QK_W = ATTN_WIDTH + 2 * KV_WIDTH
PROJ_W = QK_W + 2 * KV_WIDTH + RWKV_COLS


def _proj_kernel(x_ref, g_ref, w_ref, cos_ref, s1_ref, s2_ref, qkg_ref, bd_ref,
                 q_ref, k_ref, v_ref, zr_ref):
    h = _rms(x_ref[0], g_ref[...]).astype(BF16)
    proj = _dot(h, w_ref[...])
    qk = proj[:, :QK_W]
    ms = _head_sum(qk * qk, bd_ref[...]) * (1.0 / HEAD_DIM)
    qk = qk * lax.rsqrt(ms + NORM_EPS) * qkg_ref[...]
    cos, s1, s2 = cos_ref[...], s1_ref[...], s2_ref[...]
    tiles = []
    for i in range(0, QK_W, LANES):
        t = qk[:, i:i + LANES]
        t = t * cos + pltpu.roll(t, LANES - ROT_DIM // 2, 1) * s1 + pltpu.roll(t, ROT_DIM // 2, 1) * s2
        tiles.append(t)
    q_ref[0] = (jnp.concatenate(tiles[:4], axis=1) * (HEAD_DIM ** -0.5)).astype(BF16)
    k_ref[0] = jnp.concatenate(tiles[4:], axis=1).astype(BF16)
    v_ref[0] = proj[:, QK_W:QK_W + 2 * KV_WIDTH].astype(BF16)
    zr_ref[0] = proj[:, QK_W + 2 * KV_WIDTH:]


def _proj(x, gain, w, cos, s1, s2, qkg, bd, tm):
    b, s, d = x.shape
    const = lambda shape: pl.BlockSpec(shape, lambda bi, i: (0,) * len(shape))
    tok = lambda width: pl.BlockSpec((1, tm, width), lambda bi, i: (bi, i, 0))
    tab = pl.BlockSpec((tm, LANES), lambda bi, i: (i, 0))
    return pl.pallas_call(
        _proj_kernel,
        grid=(b, s // tm),
        in_specs=[tok(d), const((1, d)), const((d, PROJ_W)), tab, tab, tab, const((1, QK_W)),
                  const((LANES, LANES))],
        out_specs=[tok(ATTN_WIDTH), tok(2 * KV_WIDTH), tok(2 * KV_WIDTH), tok(RWKV_COLS)],
        out_shape=[jax.ShapeDtypeStruct((b, s, ATTN_WIDTH), BF16),
                   jax.ShapeDtypeStruct((b, s, 2 * KV_WIDTH), BF16),
                   jax.ShapeDtypeStruct((b, s, 2 * KV_WIDTH), BF16),
                   jax.ShapeDtypeStruct((b, s, RWKV_COLS), F32)],
        compiler_params=_cparams(("parallel", "parallel")),
        name="proj",
    )(x, gain, w, cos, s1, s2, qkg, bd)


def _attn_kernel(sink_ref, q_ref, kp_ref, kc_ref, kn_ref, vp_ref, vc_ref, vn_ref, o_ref):
    n = pl.program_id(1)
    nb = pl.num_programs(1)
    blk = ATTN_BLOCK
    k = jnp.concatenate([kp_ref[0], kc_ref[0], kn_ref[0]], axis=0)
    v = jnp.concatenate([vp_ref[0], vc_ref[0], vn_ref[0]], axis=0)
    qi = lax.broadcasted_iota(jnp.int32, (2 * blk, 3 * blk), 0) % blk
    ki = lax.broadcasted_iota(jnp.int32, (2 * blk, 3 * blk), 1)
    valid = (ki >= qi) & (ki <= qi + 2 * WINDOW)
    valid = valid & ((ki >= blk) | (n > 0)) & ((ki < 2 * blk) | (n < nb - 1))
    first = lax.broadcasted_iota(jnp.int32, (2 * blk, 1), 0) < blk
    left = lax.broadcasted_iota(jnp.int32, (1, LANES), 1) < HEAD_DIM
    zero = jnp.zeros((), BF16)
    recs = []
    for g in range(KV_HEADS):
        pairs = jnp.concatenate([q_ref[0, :, p * LANES:(p + 1) * LANES] for p in (2 * g, 2 * g + 1)], axis=0)
        for side in range(2):
            off = LANES * ((g + side) % 2)
            keep = left if side == 0 else jnp.logical_not(left)
            sk = jnp.where(first, sink_ref[4 * g + side], sink_ref[4 * g + 2 + side])
            recs.append(dict(g=g, q=jnp.where(keep, pairs, zero), k=k[:, off:off + LANES],
                             v=jnp.where(keep, v[:, off:off + LANES], zero), sk=sk))
    for c in recs:
        c["sc"] = jnp.where(valid, _dot_nt(c["q"], c["k"]), -1e30)
    for c in recs:
        m = jnp.maximum(jnp.max(c["sc"], axis=-1, keepdims=True), c["sk"])
        e = jnp.exp(c["sc"] - m)
        c["den"] = jnp.sum(e, axis=-1, keepdims=True) + jnp.exp(c["sk"] - m)
        c["e"] = e.astype(BF16)
    for c in recs:
        c["o"] = _dot(c["e"], c["v"]) / c["den"]
    outs = []
    for g in range(KV_HEADS):
        both = recs[2 * g]["o"] + recs[2 * g + 1]["o"]
        outs += [both[:blk], both[blk:]]
    o_ref[0] = jnp.concatenate(outs, axis=1).astype(BF16)


def _attn(sink, q, k2, v2):
    b, s, _ = q.shape
    nb = s // ATTN_BLOCK
    kv = lambda fn: pl.BlockSpec((1, ATTN_BLOCK, 2 * KV_WIDTH), fn)
    prev = lambda bi, n: (bi, jnp.maximum(n - 1, 0), 0)
    cur = lambda bi, n: (bi, n, 0)
    nxt = lambda bi, n: (bi, jnp.minimum(n + 1, nb - 1), 0)
    return pl.pallas_call(
        _attn_kernel,
        grid=(b, nb),
        in_specs=[pl.BlockSpec(memory_space=pltpu.SMEM),
                  pl.BlockSpec((1, ATTN_BLOCK, ATTN_WIDTH), cur),
                  kv(prev), kv(cur), kv(nxt), kv(prev), kv(cur), kv(nxt)],
        out_specs=pl.BlockSpec((1, ATTN_BLOCK, ATTN_WIDTH), cur),
        out_shape=jax.ShapeDtypeStruct((b, s, ATTN_WIDTH), BF16),
        compiler_params=_cparams(("parallel", "parallel")),
        name="attn",
    )(sink, q, k2, k2, k2, v2, v2, v2)


HALO = 8


def _prep_kernel(z_ref, zp_ref, zn_ref, mu_ref, wlo_ref, g2_ref, w0_ref, a0_ref, kk_ref, ka_ref, rk_ref,
                 bd_ref, r_ref, v_ref, a_ref, lw0_ref, lw1_ref, k0_ref, k1_ref, b0_ref, b1_ref,
                 g_ref, bonus_ref):
    i = pl.program_id(1)
    nt = pl.num_programs(1)
    z = z_ref[0]
    tt = z.shape[0]
    row = lax.broadcasted_iota(jnp.int32, (tt, 1), 0)
    prev_row = jnp.where(i > 0, zp_ref[0, HALO - 1:HALO, :], 0.0)
    next_row = jnp.where(i < nt - 1, zn_ref[0, 0:1, :], 0.0)
    prev = jnp.where(row == 0, prev_row, pltpu.roll(z, 1, 0))
    nxt = jnp.where(row == tt - 1, next_row, pltpu.roll(z, tt - 1, 0))
    z = z + mu_ref[...] * (0.5 * (prev + nxt) - z)

    c = RWKV_WIDTH
    r, k, v = z[:, :c], z[:, c:2 * c], z[:, 2 * c:3 * c]
    lowrank = z[:, 3 * c:3 * c + LANES]
    left = lax.broadcasted_iota(jnp.int32, (1, LANES), 1) < DECAY_RANK
    lowrank = jnp.where(left, jnp.tanh(lowrank), lowrank).astype(BF16)
    wa = _dot(lowrank, wlo_ref[...])
    g_ref[0] = _dot(_sigmoid(z[:, 3 * c + LANES:]).astype(BF16), g2_ref[...])

    bd = bd_ref[...]
    kk = k * kk_ref[...]
    nrm = jnp.sqrt(_head_sum(kk * kk, bd))
    kk = kk / jnp.maximum(nrm, 1e-12)
    r_ref[0] = r.astype(BF16)
    v_ref[0] = v.astype(BF16)
    a_ref[0] = (-kk).astype(BF16)
    dot_rk = None
    for d, (lw_ref, k_ref, b_ref) in enumerate(((lw0_ref, k0_ref, b0_ref), (lw1_ref, k1_ref, b1_ref))):
        w_raw = w0_ref[d:d + 1, :] + wa[:, d * c:(d + 1) * c]
        lw_ref[0] = -math.exp(-0.5) * _sigmoid(w_raw)
        rate = _sigmoid(a0_ref[d:d + 1, :] + wa[:, (2 + d) * c:(3 + d) * c])
        k_dir = k * (1.0 + (rate - 1.0) * ka_ref[...])
        k_ref[0] = k_dir.astype(BF16)
        b_ref[0] = (kk * rate).astype(BF16)
        term = k_dir * rk_ref[d:d + 1, :]
        dot_rk = term if dot_rk is None else dot_rk + term
    bonus_ref[0] = _head_sum(r * dot_rk, bd) * v


def _prep(zr, mu, wlo, g2, w0, a0, k_k, k_a, r_k, bd, tt):
    b, s, _ = zr.shape
    nh = tt // HALO
    const = lambda shape: pl.BlockSpec(shape, lambda bi, i: (0,) * len(shape))
    tok = lambda width: pl.BlockSpec((1, tt, width), lambda bi, i: (bi, i, 0))
    halo_p = pl.BlockSpec((1, HALO, RWKV_COLS), lambda bi, i: (bi, jnp.maximum(i * nh - 1, 0), 0))
    halo_n = pl.BlockSpec((1, HALO, RWKV_COLS), lambda bi, i: (bi, jnp.minimum((i + 1) * nh, s // HALO - 1), 0))
    c = RWKV_WIDTH
    return pl.pallas_call(
        _prep_kernel,
        grid=(b, s // tt),
        in_specs=[tok(RWKV_COLS), halo_p, halo_n, const((1, RWKV_COLS)), const((LANES, 4 * c)),
                  const((GATE_RANK, c)), const((2, c)), const((2, c)), const((1, c)), const((1, c)),
                  const((2, c)), const((LANES, LANES))],
        out_specs=[tok(c)] * 11,
        out_shape=[jax.ShapeDtypeStruct((b, s, c), dt) for dt in (BF16,) * 3 + (F32,) * 2 + (BF16,) * 4 + (F32,) * 2],
        compiler_params=_cparams(("parallel", "parallel")),
        name="prep",
    )(zr, zr, zr, mu, wlo, g2, w0, a0, k_k, k_a, r_k, bd)


def _scan_kernel(r0_ref, v0_ref, a0_ref, lw0_ref, k0_ref, b0_ref,
                 r1_ref, v1_ref, a1_ref, lw1_ref, k1_ref, b1_ref,
                 y0_ref, y1_ref, ht_ref):
    L, gw, nch = CHUNK, GROUP_W, SCAN_CHUNKS
    rep = gw // L
    ng = RWKV_WIDTH // gw
    rows = nch * L

    @pl.when(pl.program_id(1) == 0)
    def _():
        ht_ref[...] = jnp.zeros_like(ht_ref)

    tau = lax.broadcasted_iota(jnp.int32, (L, gw), 0)
    sig = lax.broadcasted_iota(jnp.int32, (L, gw), 1) % L
    bd = (lax.broadcasted_iota(jnp.int32, (gw, gw), 0) // L) == (lax.broadcasted_iota(jnp.int32, (gw, gw), 1) // L)
    ti = lax.broadcasted_iota(jnp.int32, (rows, rows), 0)
    si = lax.broadcasted_iota(jnp.int32, (rows, rows), 1)
    same_chunk = (ti // L) == (si // L)

    def bdtile(x):
        xb = x.astype(BF16)
        return jnp.where(bd, jnp.concatenate([xb] * rep, axis=0), jnp.zeros((), BF16))

    dirs = ((r0_ref, v0_ref, a0_ref, lw0_ref, k0_ref, b0_ref),
            (r1_ref, v1_ref, a1_ref, lw1_ref, k1_ref, b1_ref))
    recs = {}
    for d, (r_ref, v_ref, a_ref, lw_ref, k_ref, b_ref) in enumerate(dirs):
        if d == 0:
            tri = (same_chunk & (si <= ti)).astype(BF16)
            strict, incl = sig < tau, sig <= tau
        else:
            tri = (same_chunk & (si >= ti)).astype(BF16)
            strict, incl = sig > tau, sig >= tau
        lw = lw_ref[0]
        cl = _split_dot_left(tri, lw)
        e_in = jnp.exp(cl)
        e_inv = jnp.exp(-cl)
        e_ex = jnp.exp(cl - lw)
        a_t = a_ref[0].astype(F32) * e_ex
        r_t = r_ref[0].astype(F32) * e_in
        b_t = b_ref[0].astype(F32) * e_inv
        k_t = k_ref[0].astype(F32) * e_inv
        v_all = v_ref[0]
        for j in range(nch):
            rs = slice(j * L, (j + 1) * L)
            last = (j + 1) * L - 1 if d == 0 else j * L
            d_end = jnp.exp(cl[last:last + 1, :])
            b_h = b_t[rs] * d_end
            k_h = k_t[rs] * d_end
            for g in range(ng):
                sl = slice(g * gw, (g + 1) * gw)
                recs[(j, d, g)] = dict(strict=strict, incl=incl, at=a_t[rs, sl], rt=r_t[rs, sl], vv=v_all[rs, sl],
                                       bt=b_t[rs, sl], kt=k_t[rs, sl], bh=b_h[:, sl], kh=k_h[:, sl],
                                       dend=d_end[:, sl])
    par = list(recs.values())
    for c in par:
        c["lhs"] = jnp.concatenate([c["at"], c["rt"]], axis=0).astype(BF16)
    for c in par:
        gram = _dot_nt(c["lhs"], jnp.concatenate([bdtile(c["bt"]), bdtile(c["kt"])], axis=0))
        c["gb"], c["gk"] = gram[:, :gw], gram[:, gw:]
    for c in par:
        c["m_rb"] = jnp.where(c["incl"], c["gb"][L:], 0.0).astype(BF16)
        c["m_rk"] = jnp.where(c["incl"], c["gk"][L:], 0.0).astype(BF16)
        c["p"] = jnp.where(c["strict"], c["gb"][:L], 0.0)
        c["v_bd"] = bdtile(c["vv"])
    for c in par:
        c["makv"] = _dot(jnp.where(c["strict"], c["gk"][:L], 0.0).astype(BF16), c["v_bd"])
    levels = int(math.log2(L))
    for lvl in range(levels):
        for c in par:
            ops = ([] if lvl == 0 else [bdtile(c["t"])]) + ([] if lvl == levels - 1 else [bdtile(c["p"])])
            c["res"] = _dot(c["p"].astype(BF16), ops[0] if len(ops) == 1 else jnp.concatenate(ops, axis=1))
        for c in par:
            if lvl == 0:
                c["t"] = jnp.where(sig == tau, 1.0, c["p"])
                c["p"] = c["res"]
            else:
                c["t"] = c["t"] + c["res"][:, :gw]
                if lvl < levels - 1:
                    c["p"] = c["res"][:, gw:]
    for c in par:
        res = _dot(c["t"].astype(BF16), jnp.concatenate([bdtile(c["at"]), bdtile(c["makv"])], axis=1))
        c["ah"], c["uloc"] = res[:, :gw], res[:, gw:]
    for c in par:
        c["bkh"] = jnp.concatenate([c["bh"], c["kh"]], axis=0).astype(BF16)
        c["g"] = jnp.where(bd, _dot_tn(c["ah"].astype(BF16), c["bkh"][:L]), 0.0).astype(BF16)
    for c in par:
        c["cst"] = jnp.where(bd, _dot_tn(jnp.concatenate([c["uloc"].astype(BF16), c["vv"]], axis=0), c["bkh"]), 0.0)
    for c in par:
        res = _dot(c["m_rb"], jnp.concatenate([bdtile(c["ah"]), bdtile(c["uloc"])], axis=1))
        c["qh"] = (c["rt"] + res[:, :gw]).astype(BF16)
        c["yloc"] = res[:, gw:] + _dot(c["m_rk"], c["v_bd"])
    state = {(d, g): ht_ref[d, g] for d in range(2) for g in range(ng)}
    for step in range(nch):
        cur = [(d, g, recs[(step if d == 0 else nch - 1 - step, d, g)]) for d in range(2) for g in range(ng)]
        for d, g, c in cur:
            c["htb"] = state[(d, g)].astype(BF16)
        for d, g, c in cur:
            state[(d, g)] = state[(d, g)] * c["dend"] + c["cst"] + _dot(c["htb"], c["g"])
        for d, g, c in cur:
            c["y"] = c["yloc"] + _dot_nt(c["qh"], c["htb"])
    for (d, g), h in state.items():
        ht_ref[d, g] = h
    for d, y_ref in enumerate((y0_ref, y1_ref)):
        y_ref[0] = jnp.concatenate(
            [jnp.concatenate([recs[(j, d, g)]["y"] for g in range(ng)], axis=1) for j in range(nch)], axis=0)


def _split_dot_left(w_bf16, x):
    hi = x.astype(BF16)
    lo = (x - hi.astype(F32)).astype(BF16)
    return _dot(w_bf16, hi) + _dot(w_bf16, lo)


def _scan(r, v, a, lw0, lw1, k0, k1, b0, b1):
    b, s, c = r.shape
    rows = CHUNK * SCAN_CHUNKS
    nblk = s // rows
    fwd = pl.BlockSpec((1, rows, c), lambda bi, i: (bi, i, 0))
    bwd = pl.BlockSpec((1, rows, c), lambda bi, i: (bi, nblk - 1 - i, 0))
    ng = c // GROUP_W
    return pl.pallas_call(
        _scan_kernel,
        grid=(b, nblk),
        in_specs=[fwd] * 6 + [bwd] * 6,
        out_specs=[fwd, bwd],
        out_shape=[jax.ShapeDtypeStruct((b, s, c), F32)] * 2,
        scratch_shapes=[pltpu.VMEM((2, ng, GROUP_W, GROUP_W), F32)],
        compiler_params=_cparams(("parallel", "arbitrary")),
        name="scan",
    )(r, v, a, lw0, k0, b0, r, v, a, lw1, k1, b1)


def _merge_kernel(x_ref, gain_ref, wg_ref, attn_ref, y0_ref, y1_ref, bonus_ref, g_ref, lnw_ref, lnb_ref,
                  bd_ref, wua_ref, wur_ref, wout_ref, o_ref):
    x = x_ref[0]
    d = x.shape[1]
    h = _rms(x, gain_ref[...]).astype(BF16)
    gates = _sigmoid(_dot(h, wg_ref[...]))
    bd = bd_ref[...]
    y = y0_ref[0] + y1_ref[0]
    yc = y - _head_sum(y, bd) * (1.0 / HEAD_DIM)
    var = _head_sum(yc * yc, bd) * (1.0 / HEAD_DIM)
    yn = yc * lax.rsqrt(var + GN_EPS) * lnw_ref[...] + lnb_ref[...]
    rw = (yn + bonus_ref[0]) * g_ref[0]
    merged = (gates[:, :d] * _dot(attn_ref[0], wua_ref[...])
              + gates[:, d:] * _dot(rw.astype(BF16), wur_ref[...]))
    o_ref[0] = x + _dot(merged.astype(BF16), wout_ref[...])


def _merge(x, gain, wg, attn, y0, y1, bonus, g, lnw, lnb, bd, wua, wur, wout, tm):
    b, s, d = x.shape
    c = RWKV_WIDTH
    const = lambda shape: pl.BlockSpec(shape, lambda bi, i: (0,) * len(shape))
    tok = lambda width: pl.BlockSpec((1, tm, width), lambda bi, i: (bi, i, 0))
    return pl.pallas_call(
        _merge_kernel,
        grid=(b, s // tm),
        in_specs=[tok(d), const((1, d)), const((d, 2 * d)), tok(ATTN_WIDTH), tok(c), tok(c), tok(c), tok(c),
                  const((1, c)), const((1, c)), const((LANES, LANES)), const((ATTN_WIDTH, d)), const((c, d)),
                  const((d, d))],
        out_specs=tok(d),
        out_shape=jax.ShapeDtypeStruct((b, s, d), F32),
        compiler_params=_cparams(("parallel", "parallel")),
        name="merge",
    )(x, gain, wg, attn, y0, y1, bonus, g, lnw, lnb, bd, wua, wur, wout)


FF_SLAB = 1024


def _ffn_kernel(x_ref, p_ref, gf_ref, w1_ref, w2_ref, gp_ref, wpg_ref, wple_ref, o_ref):
    x = x_ref[0]
    h = _rms(x, gf_ref[...]).astype(BF16)
    acc = x
    for j in range(0, w1_ref.shape[1], FF_SLAB):
        hid = jnp.maximum(_dot(h, w1_ref[:, j:j + FF_SLAB]), 0.0)
        acc = acc + _dot((hid * hid).astype(BF16), w2_ref[j:j + FF_SLAB, :])
    hp = _rms(acc, gp_ref[...]).astype(BF16)
    o_ref[0] = acc + _dot(p_ref[0].astype(BF16), wple_ref[...]) * _sigmoid(_dot(hp, wpg_ref[...]))


def _ffn(x, p, gf, w1, w2, gp, wpg, wple, tm):
    b, s, d = x.shape
    dff = w1.shape[1]
    pd = p.shape[-1]
    const = lambda shape: pl.BlockSpec(shape, lambda bi, i: (0,) * len(shape))
    tok = lambda width: pl.BlockSpec((1, tm, width), lambda bi, i: (bi, i, 0))
    return pl.pallas_call(
        _ffn_kernel,
        grid=(b, s // tm),
        in_specs=[tok(d), tok(pd), const((1, d)), const((d, dff)), const((dff, d)), const((1, d)),
                  const((d, d)), const((pd, d))],
        out_specs=tok(d),
        out_shape=jax.ShapeDtypeStruct((b, s, d), F32),
        compiler_params=_cparams(("parallel", "parallel")),
        name="ffn",
    )(x, p, gf, w1, w2, gp, wpg, wple)


def _rotary_tables(s):
    half = ROT_DIM // 2
    inv_freq = jnp.power(jnp.float32(ROPE_THETA), -jnp.arange(half, dtype=F32) * 2.0 / ROT_DIM)
    ang = jnp.arange(s).astype(F32)[:, None] * inv_freq[None, :]
    cos, sin = jnp.cos(ang), jnp.sin(ang)
    pad = jnp.zeros((s, HEAD_DIM - ROT_DIM), F32)
    zero = jnp.zeros((s, half), F32)
    c = jnp.concatenate([cos, cos, pad + 1.0], axis=1)
    s1 = jnp.concatenate([-sin, zero, pad], axis=1)
    s2 = jnp.concatenate([zero, sin, pad], axis=1)
    rep = LANES // HEAD_DIM
    return jnp.tile(c, (1, rep)), jnp.tile(s1, (1, rep)), jnp.tile(s2, (1, rep))


def _swap_halves(w):
    return jnp.concatenate([w[:, HEAD_DIM:], w[:, :HEAD_DIM]], axis=1)


def kernel(x, p, norm_mix, w_in, shift_mu, q_norm, k_norm, sink, w0, w2, a0, a2, g2, k_k, k_a, r_k, lnx_w, lnx_b,
           w_up_attn, w_up_rwkv, w_out, norm_ffn, w_ff1, w_ff2, norm_ple, w_ple_gate, w_ple):
    bsz, s, d = x.shape
    depth = w_in.shape[0]
    c = RWKV_WIDTH
    tm = min(512, s)
    tt = min(256, s)
    cos, s1, s2 = _rotary_tables(s)
    lane = jnp.arange(LANES)
    bd = ((lane[:, None] // HEAD_DIM) == (lane[None, :] // HEAD_DIM)).astype(BF16)
    o_k, o_v, o_r, o_g = ATTN_WIDTH, ATTN_WIDTH + KV_WIDTH, ATTN_WIDTH + 2 * KV_WIDTH, ATTN_WIDTH + 2 * KV_WIDTH + RWKV_COLS
    for i in range(depth):
        wi = w_in[i]
        wk, wv = wi[:, o_k:o_v], wi[:, o_v:o_r]
        w_proj = jnp.concatenate([wi[:, :o_k], wk, _swap_halves(wk), wv, _swap_halves(wv), wi[:, o_r:o_g]],
                                 axis=1).astype(BF16)
        qkg = jnp.concatenate([jnp.tile(q_norm[i], Q_HEADS), jnp.tile(k_norm[i], 2 * KV_HEADS)])[None, :]
        q, k2, v2, zr = _proj(x, norm_mix[i][None, :], w_proj, cos, s1, s2, qkg, bd, tm)
        attn = _attn(sink[i], q, k2, v2)

        zpad = jnp.zeros((DECAY_RANK, c), F32)
        wlo = jnp.concatenate([jnp.concatenate([w2[i, 0], zpad], axis=0), jnp.concatenate([w2[i, 1], zpad], axis=0),
                               jnp.concatenate([zpad, a2[i, 0]], axis=0), jnp.concatenate([zpad, a2[i, 1]], axis=0)],
                              axis=1).astype(BF16)
        (r, v, a, lw0, lw1, k0, k1, b0, b1, g, bonus) = _prep(
            zr, shift_mu[i][None, :], wlo, g2[i].astype(BF16), w0[i], a0[i], k_k[i][None, :], k_a[i][None, :],
            r_k[i].reshape(2, c), bd, tt)
        y0, y1 = _scan(r, v, a, lw0, lw1, k0, k1, b0, b1)

        x = _merge(x, norm_mix[i][None, :], wi[:, o_g:].astype(BF16), attn, y0, y1, bonus, g,
                   lnx_w[i][None, :], lnx_b[i][None, :], bd, w_up_attn[i].astype(BF16), w_up_rwkv[i].astype(BF16),
                   w_out[i].astype(BF16), tm)
        x = _ffn(x, p[i], norm_ffn[i][None, :], w_ff1[i].astype(BF16), w_ff2[i].astype(BF16), norm_ple[i][None, :],
                 w_ple_gate[i].astype(BF16), w_ple[i].astype(BF16), tm)
    return x
```

```python
import functools
import math

import jax
import jax.numpy as jnp
from jax import lax
from jax.experimental import pallas as pl
from jax.experimental.pallas import tpu as pltpu

F32 = jnp.float32
BF16 = jnp.bfloat16

LANES = 128
HEAD_DIM = 64
Q_HEADS = 8
KV_HEADS = 2
ATTN_WIDTH = Q_HEADS * HEAD_DIM
KV_WIDTH = KV_HEADS * HEAD_DIM
WINDOW = 128
ATTN_BLOCK = 128
ATTN_SUB = 2
ROPE_THETA = 500000.0
ROT_DIM = HEAD_DIM // 4
RWKV_WIDTH = 512
DECAY_RANK = 64
ICLR_RANK = 64
GATE_RANK = 128
RWKV_COLS = 3 * RWKV_WIDTH + DECAY_RANK + ICLR_RANK + GATE_RANK
NORM_EPS = 1e-6
GN_EPS = 64e-5
CHUNK = 64
GROUP_W = 128
SCAN_CHUNKS = 4
VMEM_LIMIT = 56 * 1024 * 1024


def _cparams(sem):
    return pltpu.CompilerParams(dimension_semantics=sem, vmem_limit_bytes=VMEM_LIMIT)


def _dot(a, b):
    return jnp.dot(a, b, preferred_element_type=F32)


def _dot_nt(a, b):
    return lax.dot_general(a, b, (((1,), (1,)), ((), ())), preferred_element_type=F32)


def _dot_tn(a, b):
    return lax.dot_general(a, b, (((0,), (0,)), ((), ())), preferred_element_type=F32)


def _split_dot(x, w_bf16):
    hi = x.astype(BF16)
    lo = (x - hi.astype(F32)).astype(BF16)
    return _dot(hi, w_bf16) + _dot(lo, w_bf16)


def _head_sum(x, bd):
    cols = [_split_dot(x[:, i:i + LANES], bd) for i in range(0, x.shape[1], LANES)]
    return cols[0] if len(cols) == 1 else jnp.concatenate(cols, axis=1)


def _rms(x, gain):
    return x * lax.rsqrt(jnp.mean(x * x, axis=-1, keepdims=True) + NORM_EPS) * gain


def _sigmoid(x):
    return 1.0 / (1.0 + jnp.exp(-x))


QK_W = ATTN_WIDTH + 2 * KV_WIDTH
PROJ_W = QK_W + 2 * KV_WIDTH + RWKV_COLS


def _proj_kernel(x_ref, g_ref, w_ref, cos_ref, s1_ref, s2_ref, qkg_ref, bd_ref,
                 q_ref, k_ref, v_ref, zr_ref):
    h = _rms(x_ref[0], g_ref[...]).astype(BF16)
    proj = _dot(h, w_ref[...])
    qk = proj[:, :QK_W]
    ms = _head_sum(qk * qk, bd_ref[...]) * (1.0 / HEAD_DIM)
    qk = qk * lax.rsqrt(ms + NORM_EPS) * qkg_ref[...]
    cos, s1, s2 = cos_ref[...], s1_ref[...], s2_ref[...]
    tiles = []
    for i in range(0, QK_W, LANES):
        t = qk[:, i:i + LANES]
        t = t * cos + pltpu.roll(t, LANES - ROT_DIM // 2, 1) * s1 + pltpu.roll(t, ROT_DIM // 2, 1) * s2
        tiles.append(t)
    q_ref[0] = (jnp.concatenate(tiles[:4], axis=1) * (HEAD_DIM ** -0.5)).astype(BF16)
    k_ref[0] = jnp.concatenate(tiles[4:], axis=1).astype(BF16)
    v_ref[0] = proj[:, QK_W:QK_W + 2 * KV_WIDTH].astype(BF16)
    zr_ref[0] = proj[:, QK_W + 2 * KV_WIDTH:]


def _proj(x, gain, w, cos, s1, s2, qkg, bd, tm):
    b, s, d = x.shape
    const = lambda shape: pl.BlockSpec(shape, lambda bi, i: (0,) * len(shape))
    tok = lambda width: pl.BlockSpec((1, tm, width), lambda bi, i: (bi, i, 0))
    tab = pl.BlockSpec((tm, LANES), lambda bi, i: (i, 0))
    return pl.pallas_call(
        _proj_kernel,
        grid=(b, s // tm),
        in_specs=[tok(d), const((1, d)), const((d, PROJ_W)), tab, tab, tab, const((1, QK_W)),
                  const((LANES, LANES))],
        out_specs=[tok(ATTN_WIDTH), tok(2 * KV_WIDTH), tok(2 * KV_WIDTH), tok(RWKV_COLS)],
        out_shape=[jax.ShapeDtypeStruct((b, s, ATTN_WIDTH), BF16),
                   jax.ShapeDtypeStruct((b, s, 2 * KV_WIDTH), BF16),
                   jax.ShapeDtypeStruct((b, s, 2 * KV_WIDTH), BF16),
                   jax.ShapeDtypeStruct((b, s, RWKV_COLS), F32)],
        compiler_params=_cparams(("parallel", "parallel")),
        name="proj",
    )(x, gain, w, cos, s1, s2, qkg, bd)


def _attn_kernel(sink_ref, q_ref, kp_ref, kc_ref, kn_ref, vp_ref, vc_ref, vn_ref, o_ref):
    n = pl.program_id(1)
    nb = pl.num_programs(1)
    blk, nsub = ATTN_BLOCK, ATTN_SUB
    k = jnp.concatenate([kp_ref[0], kc_ref[0], kn_ref[0]], axis=0)
    v = jnp.concatenate([vp_ref[0], vc_ref[0], vn_ref[0]], axis=0)
    qi = lax.broadcasted_iota(jnp.int32, (2 * blk, blk), 0) % blk
    ki = lax.broadcasted_iota(jnp.int32, (2 * blk, blk), 1)
    first = lax.broadcasted_iota(jnp.int32, (2 * blk, 1), 0) < blk
    left = lax.broadcasted_iota(jnp.int32, (1, LANES), 1) < HEAD_DIM
    zero = jnp.zeros((), BF16)
    recs = []
    for j in range(nsub):
        rows = slice(j * blk, (j + 1) * blk)
        win = slice(j * blk, (j + 3) * blk)
        mask_prev = (ki >= qi) & ((n > 0) if j == 0 else True)
        mask_next = (ki <= qi) & ((n < nb - 1) if j == nsub - 1 else True)
        for g in range(KV_HEADS):
            pairs = jnp.concatenate([q_ref[0, rows, p * LANES:(p + 1) * LANES] for p in (2 * g, 2 * g + 1)], axis=0)
            for side in range(2):
                off = LANES * ((g + side) % 2)
                keep = left if side == 0 else jnp.logical_not(left)
                sk = jnp.where(first, sink_ref[4 * g + side], sink_ref[4 * g + 2 + side])
                recs.append(dict(q=jnp.where(keep, pairs, zero), k=k[win, off:off + LANES],
                                 v=jnp.where(keep, v[win, off:off + LANES], zero), sk=sk,
                                 mask_prev=mask_prev, mask_next=mask_next))
    for c in recs:
        sc = _dot_nt(c["q"], c["k"])
        c["sc"] = jnp.concatenate([jnp.where(c["mask_prev"], sc[:, :blk], -1e30), sc[:, blk:2 * blk],
                                   jnp.where(c["mask_next"], sc[:, 2 * blk:], -1e30)], axis=1)
    for c in recs:
        m = jnp.maximum(jnp.max(c["sc"], axis=-1, keepdims=True), c["sk"])
        e = jnp.exp(c["sc"] - m)
        c["den"] = jnp.sum(e, axis=-1, keepdims=True) + jnp.exp(c["sk"] - m)
        c["e"] = e.astype(BF16)
    for c in recs:
        c["o"] = _dot(c["e"], c["v"]) / c["den"]
    for j in range(nsub):
        outs = []
        for g in range(KV_HEADS):
            both = recs[4 * j + 2 * g]["o"] + recs[4 * j + 2 * g + 1]["o"]
            outs += [both[:blk], both[blk:]]
        o_ref[0, j * blk:(j + 1) * blk, :] = jnp.concatenate(outs, axis=1).astype(BF16)


def _attn(sink, q, k2, v2):
    b, s, _ = q.shape
    nsub = ATTN_SUB
    nb = s // (ATTN_BLOCK * nsub)
    last = s // ATTN_BLOCK - 1
    edge = lambda fn: pl.BlockSpec((1, ATTN_BLOCK, 2 * KV_WIDTH), fn)
    prev = lambda bi, n: (bi, jnp.maximum(n * nsub - 1, 0), 0)
    cur = lambda bi, n: (bi, n, 0)
    nxt = lambda bi, n: (bi, jnp.minimum((n + 1) * nsub, last), 0)
    mid = pl.BlockSpec((1, ATTN_BLOCK * nsub, 2 * KV_WIDTH), cur)
    return pl.pallas_call(
        _attn_kernel,
        grid=(b, nb),
        in_specs=[pl.BlockSpec(memory_space=pltpu.SMEM),
                  pl.BlockSpec((1, ATTN_BLOCK * nsub, ATTN_WIDTH), cur),
                  edge(prev), mid, edge(nxt), edge(prev), mid, edge(nxt)],
        out_specs=pl.BlockSpec((1, ATTN_BLOCK * nsub, ATTN_WIDTH), cur),
        out_shape=jax.ShapeDtypeStruct((b, s, ATTN_WIDTH), BF16),
        compiler_params=_cparams(("parallel", "parallel")),
        name="attn",
    )(sink, q, k2, k2, k2, v2, v2, v2)


HALO = 8


def _prep_kernel(z_ref, zp_ref, zn_ref, mu_ref, wlo_ref, g2_ref, w0_ref, a0_ref, kk_ref, ka_ref, rk_ref,
                 bd_ref, r_ref, v_ref, a_ref, lw0_ref, lw1_ref, k0_ref, k1_ref, b0_ref, b1_ref,
                 g_ref, bonus_ref):
    i = pl.program_id(1)
    nt = pl.num_programs(1)
    z = z_ref[0]
    tt = z.shape[0]
    row = lax.broadcasted_iota(jnp.int32, (tt, 1), 0)
    prev_row = jnp.where(i > 0, zp_ref[0, HALO - 1:HALO, :], 0.0)
    next_row = jnp.where(i < nt - 1, zn_ref[0, 0:1, :], 0.0)
    prev = jnp.where(row == 0, prev_row, pltpu.roll(z, 1, 0))
    nxt = jnp.where(row == tt - 1, next_row, pltpu.roll(z, tt - 1, 0))
    z = z + mu_ref[...] * (0.5 * (prev + nxt) - z)

    c = RWKV_WIDTH
    r, k, v = z[:, :c], z[:, c:2 * c], z[:, 2 * c:3 * c]
    lowrank = z[:, 3 * c:3 * c + LANES]
    left = lax.broadcasted_iota(jnp.int32, (1, LANES), 1) < DECAY_RANK
    lowrank = jnp.where(left, jnp.tanh(lowrank), lowrank).astype(BF16)
    wa = _dot(lowrank, wlo_ref[...])
    g_ref[0] = _dot(_sigmoid(z[:, 3 * c + LANES:]).astype(BF16), g2_ref[...])

    bd = bd_ref[...]
    kk = k * kk_ref[...]
    nrm = jnp.sqrt(_head_sum(kk * kk, bd))
    kk = kk / jnp.maximum(nrm, 1e-12)
    r_ref[0] = r.astype(BF16)
    v_ref[0] = v.astype(BF16)
    a_ref[0] = (-kk).astype(BF16)
    dot_rk = None
    for d, (lw_ref, k_ref, b_ref) in enumerate(((lw0_ref, k0_ref, b0_ref), (lw1_ref, k1_ref, b1_ref))):
        w_raw = w0_ref[d:d + 1, :] + wa[:, d * c:(d + 1) * c]
        lw_ref[0] = -math.exp(-0.5) * _sigmoid(w_raw)
        rate = _sigmoid(a0_ref[d:d + 1, :] + wa[:, (2 + d) * c:(3 + d) * c])
        k_dir = k * (1.0 + (rate - 1.0) * ka_ref[...])
        k_ref[0] = k_dir.astype(BF16)
        b_ref[0] = (kk * rate).astype(BF16)
        term = k_dir * rk_ref[d:d + 1, :]
        dot_rk = term if dot_rk is None else dot_rk + term
    bonus_ref[0] = _head_sum(r * dot_rk, bd) * v


def _prep(zr, mu, wlo, g2, w0, a0, k_k, k_a, r_k, bd, tt):
    b, s, _ = zr.shape
    nh = tt // HALO
    const = lambda shape: pl.BlockSpec(shape, lambda bi, i: (0,) * len(shape))
    tok = lambda width: pl.BlockSpec((1, tt, width), lambda bi, i: (bi, i, 0))
    halo_p = pl.BlockSpec((1, HALO, RWKV_COLS), lambda bi, i: (bi, jnp.maximum(i * nh - 1, 0), 0))
    halo_n = pl.BlockSpec((1, HALO, RWKV_COLS), lambda bi, i: (bi, jnp.minimum((i + 1) * nh, s // HALO - 1), 0))
    c = RWKV_WIDTH
    return pl.pallas_call(
        _prep_kernel,
        grid=(b, s // tt),
        in_specs=[tok(RWKV_COLS), halo_p, halo_n, const((1, RWKV_COLS)), const((LANES, 4 * c)),
                  const((GATE_RANK, c)), const((2, c)), const((2, c)), const((1, c)), const((1, c)),
                  const((2, c)), const((LANES, LANES))],
        out_specs=[tok(c)] * 11,
        out_shape=[jax.ShapeDtypeStruct((b, s, c), dt) for dt in (BF16,) * 3 + (F32,) * 2 + (BF16,) * 4 + (F32,) * 2],
        compiler_params=_cparams(("parallel", "parallel")),
        name="prep",
    )(zr, zr, zr, mu, wlo, g2, w0, a0, k_k, k_a, r_k, bd)


def _scan_kernel(r0_ref, v0_ref, a0_ref, lw0_ref, k0_ref, b0_ref,
                 r1_ref, v1_ref, a1_ref, lw1_ref, k1_ref, b1_ref,
                 y0_ref, y1_ref, ht_ref):
    L, gw, nch = CHUNK, GROUP_W, SCAN_CHUNKS
    rep = gw // L
    ng = RWKV_WIDTH // gw
    rows = nch * L

    @pl.when(pl.program_id(1) == 0)
    def _():
        ht_ref[...] = jnp.zeros_like(ht_ref)

    tau = lax.broadcasted_iota(jnp.int32, (L, gw), 0)
    sig = lax.broadcasted_iota(jnp.int32, (L, gw), 1) % L
    bd = (lax.broadcasted_iota(jnp.int32, (gw, gw), 0) // L) == (lax.broadcasted_iota(jnp.int32, (gw, gw), 1) // L)
    half = nch // 2
    hrows = half * L
    ti = lax.broadcasted_iota(jnp.int32, (hrows, hrows), 0)
    si = lax.broadcasted_iota(jnp.int32, (hrows, hrows), 1)
    same_chunk = (ti // L) == (si // L)
    tris = ((same_chunk & (si <= ti)).astype(BF16), (same_chunk & (si >= ti)).astype(BF16))
    masks = ((sig < tau, sig <= tau), (sig > tau, sig >= tau))

    def bdtile(x):
        xb = x.astype(BF16)
        return jnp.where(bd, jnp.concatenate([xb] * rep, axis=0), jnp.zeros((), BF16))

    dirs = ((r0_ref, v0_ref, a0_ref, lw0_ref, k0_ref, b0_ref),
            (r1_ref, v1_ref, a1_ref, lw1_ref, k1_ref, b1_ref))

    def build(grp):
        out = {}
        for d, (r_ref, v_ref, a_ref, lw_ref, k_ref, b_ref) in enumerate(dirs):
            base = 0 if (grp == 0) == (d == 0) else hrows
            rsl = slice(base, base + hrows)
            strict, incl = masks[d]
            lw = lw_ref[0, rsl, :]
            cl = _split_dot_left(tris[d], lw)
            e_in = jnp.exp(cl)
            e_inv = jnp.exp(-cl)
            e_ex = jnp.exp(cl - lw)
            a_t = a_ref[0, rsl, :].astype(F32) * e_ex
            r_t = r_ref[0, rsl, :].astype(F32) * e_in
            b_t = b_ref[0, rsl, :].astype(F32) * e_inv
            k_t = k_ref[0, rsl, :].astype(F32) * e_inv
            v_all = v_ref[0, rsl, :]
            for jj in range(half):
                rs = slice(jj * L, (jj + 1) * L)
                last = (jj + 1) * L - 1 if d == 0 else jj * L
                d_end = jnp.exp(cl[last:last + 1, :])
                b_h = b_t[rs] * d_end
                k_h = k_t[rs] * d_end
                for g in range(ng):
                    sl = slice(g * gw, (g + 1) * gw)
                    out[(base // L + jj, d, g)] = dict(
                        strict=strict, incl=incl, at=a_t[rs, sl], rt=r_t[rs, sl], vv=v_all[rs, sl], bt=b_t[rs, sl],
                        kt=k_t[rs, sl], bh=b_h[:, sl], kh=k_h[:, sl], dend=d_end[:, sl])
        return out

    def s_gram(par):
        for c in par:
            lhs = jnp.concatenate([c["at"], c["rt"]], axis=0).astype(BF16)
            gram = _dot_nt(lhs, jnp.concatenate([bdtile(c["bt"]), bdtile(c["kt"])], axis=0))
            c["gb"], c["gk"] = gram[:, :gw], gram[:, gw:]

    def s_vprod(par):
        for c in par:
            c["m_rb"] = jnp.where(c["incl"], c["gb"][L:], 0.0).astype(BF16)
            c["p"] = jnp.where(c["strict"], c["gb"][:L], 0.0)
            m_k = jnp.concatenate([jnp.where(c["strict"], c["gk"][:L], 0.0), jnp.where(c["incl"], c["gk"][L:], 0.0)],
                                  axis=0)
            res = _dot(m_k.astype(BF16), bdtile(c["vv"]))
            c["makv"], c["mrkv"] = res[:L], res[L:]

    levels = int(math.log2(L))

    def s_level(lvl):
        def run(par):
            for c in par:
                ops = ([] if lvl == 0 else [bdtile(c["t"])]) + ([] if lvl == levels - 1 else [bdtile(c["p"])])
                c["res"] = _dot(c["p"].astype(BF16), ops[0] if len(ops) == 1 else jnp.concatenate(ops, axis=1))
            for c in par:
                if lvl == 0:
                    c["t"] = jnp.where(sig == tau, 1.0, c["p"])
                    c["p"] = c["res"]
                else:
                    c["t"] = c["t"] + c["res"][:, :gw]
                    if lvl < levels - 1:
                        c["p"] = c["res"][:, gw:]
        return run

    def s_apply(par):
        for c in par:
            res = _dot(c["t"].astype(BF16), jnp.concatenate([bdtile(c["at"]), bdtile(c["makv"])], axis=1))
            c["ah"], c["uloc"] = res[:, :gw], res[:, gw:]

    def s_g(par):
        for c in par:
            c["bkh"] = jnp.concatenate([c["bh"], c["kh"]], axis=0).astype(BF16)
            c["g"] = jnp.where(bd, _dot_tn(c["ah"].astype(BF16), c["bkh"][:L]), 0.0).astype(BF16)

    def s_c(par):
        for c in par:
            c["cst"] = jnp.where(bd, _dot_tn(jnp.concatenate([c["uloc"].astype(BF16), c["vv"]], axis=0), c["bkh"]),
                                 0.0)

    def s_q(par):
        for c in par:
            res = _dot(c["m_rb"], jnp.concatenate([bdtile(c["ah"]), bdtile(c["uloc"])], axis=1))
            c["qh"] = (c["rt"] + res[:, :gw]).astype(BF16)
            c["yloc"] = res[:, gw:] + c["mrkv"]

    stages = [s_gram, s_vprod] + [s_level(lvl) for lvl in range(levels)] + [s_apply, s_g, s_c, s_q]

    state = {(d, g): ht_ref[d, g] for d in range(2) for g in range(ng)}
    recs = {}

    def seq_step(step):
        cur = [(d, g, recs[(step if d == 0 else nch - 1 - step, d, g)]) for d in range(2) for g in range(ng)]
        for d, g, c in cur:
            c["htb"] = state[(d, g)].astype(BF16)
        for d, g, c in cur:
            state[(d, g)] = state[(d, g)] * c["dend"] + c["cst"] + _dot(c["htb"], c["g"])
        for d, g, c in cur:
            c["y"] = c["yloc"] + _dot_nt(c["qh"], c["htb"])

    first = build(0)
    recs.update(first)
    for s, fn in enumerate(stages):
        if s == 2:
            second = build(1)
            recs.update(second)
        fn(list(first.values()))
    slots = {(i + 1) * len(stages) // (half + 1): i for i in range(half)}
    for s, fn in enumerate(stages):
        fn(list(second.values()))
        if s in slots:
            seq_step(slots[s])
    for step in range(half, nch):
        seq_step(step)
    for (d, g), h in state.items():
        ht_ref[d, g] = h
    for d, y_ref in enumerate((y0_ref, y1_ref)):
        y_ref[0] = jnp.concatenate(
            [jnp.concatenate([recs[(j, d, g)]["y"] for g in range(ng)], axis=1) for j in range(nch)], axis=0)


def _split_dot_left(w_bf16, x):
    hi = x.astype(BF16)
    lo = (x - hi.astype(F32)).astype(BF16)
    return _dot(w_bf16, hi) + _dot(w_bf16, lo)


def _scan(r, v, a, lw0, lw1, k0, k1, b0, b1):
    b, s, c = r.shape
    rows = CHUNK * SCAN_CHUNKS
    nblk = s // rows
    fwd = pl.BlockSpec((1, rows, c), lambda bi, i: (bi, i, 0))
    bwd = pl.BlockSpec((1, rows, c), lambda bi, i: (bi, nblk - 1 - i, 0))
    ng = c // GROUP_W
    return pl.pallas_call(
        _scan_kernel,
        grid=(b, nblk),
        in_specs=[fwd] * 6 + [bwd] * 6,
        out_specs=[fwd, bwd],
        out_shape=[jax.ShapeDtypeStruct((b, s, c), F32)] * 2,
        scratch_shapes=[pltpu.VMEM((2, ng, GROUP_W, GROUP_W), F32)],
        compiler_params=_cparams(("parallel", "arbitrary")),
        name="scan",
    )(r, v, a, lw0, k0, b0, r, v, a, lw1, k1, b1)


def _merge_kernel(x_ref, gain_ref, wg_ref, attn_ref, y0_ref, y1_ref, bonus_ref, g_ref, lnw_ref, lnb_ref,
                  bd_ref, wua_ref, wur_ref, wout_ref, o_ref):
    x = x_ref[0]
    d = x.shape[1]
    h = _rms(x, gain_ref[...]).astype(BF16)
    gates = _sigmoid(_dot(h, wg_ref[...]))
    bd = bd_ref[...]
    y = y0_ref[0] + y1_ref[0]
    yc = y - _head_sum(y, bd) * (1.0 / HEAD_DIM)
    var = _head_sum(yc * yc, bd) * (1.0 / HEAD_DIM)
    yn = yc * lax.rsqrt(var + GN_EPS) * lnw_ref[...] + lnb_ref[...]
    rw = (yn + bonus_ref[0]) * g_ref[0]
    merged = (gates[:, :d] * _dot(attn_ref[0], wua_ref[...])
              + gates[:, d:] * _dot(rw.astype(BF16), wur_ref[...]))
    o_ref[0] = x + _dot(merged.astype(BF16), wout_ref[...])


def _merge(x, gain, wg, attn, y0, y1, bonus, g, lnw, lnb, bd, wua, wur, wout, tm):
    b, s, d = x.shape
    c = RWKV_WIDTH
    const = lambda shape: pl.BlockSpec(shape, lambda bi, i: (0,) * len(shape))
    tok = lambda width: pl.BlockSpec((1, tm, width), lambda bi, i: (bi, i, 0))
    return pl.pallas_call(
        _merge_kernel,
        grid=(b, s // tm),
        in_specs=[tok(d), const((1, d)), const((d, 2 * d)), tok(ATTN_WIDTH), tok(c), tok(c), tok(c), tok(c),
                  const((1, c)), const((1, c)), const((LANES, LANES)), const((ATTN_WIDTH, d)), const((c, d)),
                  const((d, d))],
        out_specs=tok(d),
        out_shape=jax.ShapeDtypeStruct((b, s, d), F32),
        compiler_params=_cparams(("parallel", "parallel")),
        name="merge",
    )(x, gain, wg, attn, y0, y1, bonus, g, lnw, lnb, bd, wua, wur, wout)


FF_SLAB = 1024


def _ffn_kernel(x_ref, p_ref, gf_ref, w1_ref, w2_ref, gp_ref, wpg_ref, wple_ref, o_ref):
    x = x_ref[0]
    h = _rms(x, gf_ref[...]).astype(BF16)
    acc = x
    for j in range(0, w1_ref.shape[1], FF_SLAB):
        hid = jnp.maximum(_dot(h, w1_ref[:, j:j + FF_SLAB]), 0.0)
        acc = acc + _dot((hid * hid).astype(BF16), w2_ref[j:j + FF_SLAB, :])
    hp = _rms(acc, gp_ref[...]).astype(BF16)
    o_ref[0] = acc + _dot(p_ref[0].astype(BF16), wple_ref[...]) * _sigmoid(_dot(hp, wpg_ref[...]))


def _ffn(x, p, gf, w1, w2, gp, wpg, wple, tm):
    b, s, d = x.shape
    dff = w1.shape[1]
    pd = p.shape[-1]
    const = lambda shape: pl.BlockSpec(shape, lambda bi, i: (0,) * len(shape))
    tok = lambda width: pl.BlockSpec((1, tm, width), lambda bi, i: (bi, i, 0))
    return pl.pallas_call(
        _ffn_kernel,
        grid=(b, s // tm),
        in_specs=[tok(d), tok(pd), const((1, d)), const((d, dff)), const((dff, d)), const((1, d)),
                  const((d, d)), const((pd, d))],
        out_specs=tok(d),
        out_shape=jax.ShapeDtypeStruct((b, s, d), F32),
        compiler_params=_cparams(("parallel", "parallel")),
        name="ffn",
    )(x, p, gf, w1, w2, gp, wpg, wple)


def _rotary_tables(s):
    half = ROT_DIM // 2
    inv_freq = jnp.power(jnp.float32(ROPE_THETA), -jnp.arange(half, dtype=F32) * 2.0 / ROT_DIM)
    ang = jnp.arange(s).astype(F32)[:, None] * inv_freq[None, :]
    cos, sin = jnp.cos(ang), jnp.sin(ang)
    pad = jnp.zeros((s, HEAD_DIM - ROT_DIM), F32)
    zero = jnp.zeros((s, half), F32)
    c = jnp.concatenate([cos, cos, pad + 1.0], axis=1)
    s1 = jnp.concatenate([-sin, zero, pad], axis=1)
    s2 = jnp.concatenate([zero, sin, pad], axis=1)
    rep = LANES // HEAD_DIM
    return jnp.tile(c, (1, rep)), jnp.tile(s1, (1, rep)), jnp.tile(s2, (1, rep))


def _swap_halves(w):
    return jnp.concatenate([w[:, HEAD_DIM:], w[:, :HEAD_DIM]], axis=1)


def kernel(x, p, norm_mix, w_in, shift_mu, q_norm, k_norm, sink, w0, w2, a0, a2, g2, k_k, k_a, r_k, lnx_w, lnx_b,
           w_up_attn, w_up_rwkv, w_out, norm_ffn, w_ff1, w_ff2, norm_ple, w_ple_gate, w_ple):
    bsz, s, d = x.shape
    depth = w_in.shape[0]
    c = RWKV_WIDTH
    tm = min(512, s)
    tt = min(512, s)
    cos, s1, s2 = _rotary_tables(s)
    lane = jnp.arange(LANES)
    bd = ((lane[:, None] // HEAD_DIM) == (lane[None, :] // HEAD_DIM)).astype(BF16)
    o_k, o_v, o_r, o_g = ATTN_WIDTH, ATTN_WIDTH + KV_WIDTH, ATTN_WIDTH + 2 * KV_WIDTH, ATTN_WIDTH + 2 * KV_WIDTH + RWKV_COLS
    for i in range(depth):
        wi = w_in[i]
        wk, wv = wi[:, o_k:o_v], wi[:, o_v:o_r]
        w_proj = jnp.concatenate([wi[:, :o_k], wk, _swap_halves(wk), wv, _swap_halves(wv), wi[:, o_r:o_g]],
                                 axis=1).astype(BF16)
        qkg = jnp.concatenate([jnp.tile(q_norm[i], Q_HEADS), jnp.tile(k_norm[i], 2 * KV_HEADS)])[None, :]
        q, k2, v2, zr = _proj(x, norm_mix[i][None, :], w_proj, cos, s1, s2, qkg, bd, tm)
        attn = _attn(sink[i], q, k2, v2)

        zpad = jnp.zeros((DECAY_RANK, c), F32)
        wlo = jnp.concatenate([jnp.concatenate([w2[i, 0], zpad], axis=0), jnp.concatenate([w2[i, 1], zpad], axis=0),
                               jnp.concatenate([zpad, a2[i, 0]], axis=0), jnp.concatenate([zpad, a2[i, 1]], axis=0)],
                              axis=1).astype(BF16)
        (r, v, a, lw0, lw1, k0, k1, b0, b1, g, bonus) = _prep(
            zr, shift_mu[i][None, :], wlo, g2[i].astype(BF16), w0[i], a0[i], k_k[i][None, :], k_a[i][None, :],
            r_k[i].reshape(2, c), bd, tt)
        y0, y1 = _scan(r, v, a, lw0, lw1, k0, k1, b0, b1)

        x = _merge(x, norm_mix[i][None, :], wi[:, o_g:].astype(BF16), attn, y0, y1, bonus, g,
                   lnx_w[i][None, :], lnx_b[i][None, :], bd, w_up_attn[i].astype(BF16), w_up_rwkv[i].astype(BF16),
                   w_out[i].astype(BF16), tm)
        x = _ffn(x, p[i], norm_ffn[i][None, :], w_ff1[i].astype(BF16), w_ff2[i].astype(BF16), norm_ple[i][None, :],
                 w_ple_gate[i].astype(BF16), w_ple[i].astype(BF16), tm)
    return x
```

```python
import functools
import math

import jax
import jax.numpy as jnp
from jax import lax
from jax.experimental import pallas as pl
from jax.experimental.pallas import tpu as pltpu

F32 = jnp.float32
BF16 = jnp.bfloat16

LANES = 128
HEAD_DIM = 64
Q_HEADS = 8
KV_HEADS = 2
ATTN_WIDTH = Q_HEADS * HEAD_DIM
KV_WIDTH = KV_HEADS * HEAD_DIM
WINDOW = 128
ATTN_BLOCK = 128
ATTN_SUB = 2
ROPE_THETA = 500000.0
ROT_DIM = HEAD_DIM // 4
RWKV_WIDTH = 512
DECAY_RANK = 64
ICLR_RANK = 64
GATE_RANK = 128
RWKV_COLS = 3 * RWKV_WIDTH + DECAY_RANK + ICLR_RANK + GATE_RANK
NORM_EPS = 1e-6
GN_EPS = 64e-5
CHUNK = 64
GROUP_W = 128
SCAN_CHUNKS = 8
VMEM_LIMIT = 56 * 1024 * 1024


def _cparams(sem):
    return pltpu.CompilerParams(dimension_semantics=sem, vmem_limit_bytes=VMEM_LIMIT)


def _dot(a, b):
    return jnp.dot(a, b, preferred_element_type=F32)


def _dot_nt(a, b):
    return lax.dot_general(a, b, (((1,), (1,)), ((), ())), preferred_element_type=F32)


def _dot_tn(a, b):
    return lax.dot_general(a, b, (((0,), (0,)), ((), ())), preferred_element_type=F32)


def _split_dot(x, w_bf16):
    hi = x.astype(BF16)
    lo = (x - hi.astype(F32)).astype(BF16)
    return _dot(hi, w_bf16) + _dot(lo, w_bf16)


def _head_sum(x, bd, split=True):
    one = _split_dot if split else (lambda t, w: _dot(t.astype(BF16), w))
    cols = [one(x[:, i:i + LANES], bd) for i in range(0, x.shape[1], LANES)]
    return cols[0] if len(cols) == 1 else jnp.concatenate(cols, axis=1)


def _rms(x, gain):
    return x * lax.rsqrt(jnp.mean(x * x, axis=-1, keepdims=True) + NORM_EPS) * gain


def _sigmoid(x):
    return 0.5 * jnp.tanh(0.5 * x) + 0.5


QK_W = ATTN_WIDTH + 2 * KV_WIDTH
QKV_W = QK_W + 2 * KV_WIDTH
PROJ_W = QKV_W + RWKV_COLS
HALO = 8
FRONT_SUB = 128


def _qkv_rows(proj, cos, s1, s2, qkg, bd):
    qk = proj[:, :QK_W]
    ms = _head_sum(qk * qk, bd, split=False) * (1.0 / HEAD_DIM)
    qk = qk * lax.rsqrt(ms + NORM_EPS) * qkg
    tiles = []
    for i in range(0, QK_W, LANES):
        t = qk[:, i:i + LANES]
        t = t * cos + pltpu.roll(t, LANES - ROT_DIM // 2, 1) * s1 + pltpu.roll(t, ROT_DIM // 2, 1) * s2
        tiles.append(t)
    q = (jnp.concatenate(tiles[:4], axis=1) * (HEAD_DIM ** -0.5)).astype(BF16)
    return q, jnp.concatenate(tiles[4:], axis=1).astype(BF16), proj[:, QK_W:QKV_W].astype(BF16)


def _rwkv_rows(z, prev_row, next_row, mu, wlo, g2, w0, a0, k_k, k_a, r_k, bd):
    n = z.shape[0]
    row = lax.broadcasted_iota(jnp.int32, (HALO, 1), 0)
    ri = lax.broadcasted_iota(jnp.int32, (n, n), 0)
    ci = lax.broadcasted_iota(jnp.int32, (n, n), 1)
    adjacent = ((ri - ci == 1) | (ci - ri == 1)).astype(BF16)
    mix = 0.5 * mu
    shifted = z * (1.0 - mu) + _dot(adjacent, z.astype(BF16)) * mix
    top = shifted[:HALO] + jnp.where(row == 0, prev_row * mix, 0.0)
    bot = shifted[n - HALO:] + jnp.where(row == HALO - 1, next_row * mix, 0.0)
    z = jnp.concatenate([top, shifted[HALO:n - HALO], bot], axis=0)

    c = RWKV_WIDTH
    r, k, v = z[:, :c], z[:, c:2 * c], z[:, 2 * c:3 * c]
    lowrank = z[:, 3 * c:3 * c + LANES]
    left = lax.broadcasted_iota(jnp.int32, (1, LANES), 1) < DECAY_RANK
    lowrank = jnp.where(left, jnp.tanh(lowrank), lowrank).astype(BF16)
    wa = _dot(lowrank, wlo)
    g = _dot(_sigmoid(z[:, 3 * c + LANES:]).astype(BF16), g2)
    kk = k * k_k
    kk = kk * lax.rsqrt(jnp.maximum(_head_sum(kk * kk, bd, split=False), 1e-24))
    lws, ks, bs = [], [], []
    dot_rk = None
    for d in range(2):
        w_raw = w0[d:d + 1, :] + wa[:, d * c:(d + 1) * c]
        lws.append(-math.exp(-0.5) * _sigmoid(w_raw))
        rate = _sigmoid(a0[d:d + 1, :] + wa[:, (2 + d) * c:(3 + d) * c])
        k_dir = k * (1.0 + (rate - 1.0) * k_a)
        ks.append(k_dir.astype(BF16))
        bs.append((kk * rate).astype(BF16))
        term = k_dir * r_k[d:d + 1, :]
        dot_rk = term if dot_rk is None else dot_rk + term
    bonus = _head_sum(r * dot_rk, bd) * v
    return (r.astype(BF16), v.astype(BF16), (-kk).astype(BF16), lws[0], lws[1], ks[0], ks[1], bs[0], bs[1], g, bonus)


def _front_kernel(x_ref, xp_ref, xn_ref, gain_ref, w_ref, cos_ref, s1_ref, s2_ref, qkg_ref, bd_ref, mu_ref, wlo_ref,
                  g2_ref, w0_ref, a0_ref, kk_ref, ka_ref, rk_ref, q_ref, k_ref, v_ref, *rwkv_refs):
    i = pl.program_id(1)
    nt = pl.num_programs(1)
    sub = FRONT_SUB
    nsub = x_ref.shape[1] // sub
    gain = gain_ref[...]
    bd = bd_ref[...]

    def project(c):
        return _dot(_rms(x_ref[0, c * sub:(c + 1) * sub, :], gain).astype(BF16), w_ref[...])

    halo = jnp.concatenate([xp_ref[0], xn_ref[0]], axis=0)
    zh = _dot(_rms(halo, gain).astype(BF16), w_ref[:, QKV_W:])
    prev_row = jnp.where(i > 0, zh[HALO - 1:HALO], 0.0)
    next_row = jnp.where(i < nt - 1, zh[HALO:HALO + 1], 0.0)

    projs = {0: project(0)}
    for c in range(nsub):
        if c + 1 < nsub:
            projs[c + 1] = project(c + 1)
        rows = slice(c * sub, (c + 1) * sub)
        q, k, v = _qkv_rows(projs[c], cos_ref[rows, :], s1_ref[rows, :], s2_ref[rows, :], qkg_ref[...], bd)
        q_ref[0, rows, :] = q
        k_ref[0, rows, :] = k
        v_ref[0, rows, :] = v
        before = projs[c - 1][sub - 1:sub, QKV_W:] if c > 0 else prev_row
        after = projs[c + 1][0:1, QKV_W:] if c + 1 < nsub else next_row
        outs = _rwkv_rows(projs[c][:, QKV_W:], before, after, mu_ref[...], wlo_ref[...], g2_ref[...], w0_ref[...],
                          a0_ref[...], kk_ref[...], ka_ref[...], rk_ref[...], bd)
        for ref, val in zip(rwkv_refs, outs):
            ref[0, rows, :] = val


def _front(x, gain, w, cos, s1, s2, qkg, bd, mu, wlo, g2, w0, a0, k_k, k_a, r_k, tm):
    b, s, d = x.shape
    c = RWKV_WIDTH
    nh = tm // HALO
    const = lambda shape: pl.BlockSpec(shape, lambda bi, i: (0,) * len(shape))
    tok = lambda width: pl.BlockSpec((1, tm, width), lambda bi, i: (bi, i, 0))
    tab = pl.BlockSpec((tm, LANES), lambda bi, i: (i, 0))
    halo_p = pl.BlockSpec((1, HALO, d), lambda bi, i: (bi, jnp.maximum(i * nh - 1, 0), 0))
    halo_n = pl.BlockSpec((1, HALO, d), lambda bi, i: (bi, jnp.minimum((i + 1) * nh, s // HALO - 1), 0))
    rwkv_dtypes = (BF16,) * 3 + (F32,) * 2 + (BF16,) * 4 + (F32,) * 2
    return pl.pallas_call(
        _front_kernel,
        grid=(b, s // tm),
        in_specs=[tok(d), halo_p, halo_n, const((1, d)), const((d, PROJ_W)), tab, tab, tab, const((1, QK_W)),
                  const((LANES, LANES)), const((1, RWKV_COLS)), const((LANES, 4 * c)), const((GATE_RANK, c)),
                  const((2, c)), const((2, c)), const((1, c)), const((1, c)), const((2, c))],
        out_specs=[tok(ATTN_WIDTH), tok(2 * KV_WIDTH), tok(2 * KV_WIDTH)] + [tok(c)] * 11,
        out_shape=[jax.ShapeDtypeStruct((b, s, ATTN_WIDTH), BF16),
                   jax.ShapeDtypeStruct((b, s, 2 * KV_WIDTH), BF16),
                   jax.ShapeDtypeStruct((b, s, 2 * KV_WIDTH), BF16)]
                  + [jax.ShapeDtypeStruct((b, s, c), dt) for dt in rwkv_dtypes],
        compiler_params=_cparams(("parallel", "parallel")),
        name="front",
    )(x, x, x, gain, w, cos, s1, s2, qkg, bd, mu, wlo, g2, w0, a0, k_k, k_a, r_k)


def _attn_kernel(sink_ref, q_ref, kp_ref, kc_ref, kn_ref, vp_ref, vc_ref, vn_ref, o_ref):
    n = pl.program_id(1)
    nb = pl.num_programs(1)
    blk, nsub = ATTN_BLOCK, ATTN_SUB
    k = jnp.concatenate([kp_ref[0], kc_ref[0], kn_ref[0]], axis=0)
    v = jnp.concatenate([vp_ref[0], vc_ref[0], vn_ref[0]], axis=0)
    qi = lax.broadcasted_iota(jnp.int32, (2 * blk, blk), 0) % blk
    ki = lax.broadcasted_iota(jnp.int32, (2 * blk, blk), 1)
    first = lax.broadcasted_iota(jnp.int32, (2 * blk, 1), 0) < blk
    left = lax.broadcasted_iota(jnp.int32, (1, LANES), 1) < HEAD_DIM
    zero = jnp.zeros((), BF16)
    recs = []
    for j in range(nsub):
        rows = slice(j * blk, (j + 1) * blk)
        win = slice(j * blk, (j + 3) * blk)
        mask_prev = (ki >= qi) & ((n > 0) if j == 0 else True)
        mask_next = (ki <= qi) & ((n < nb - 1) if j == nsub - 1 else True)
        for g in range(KV_HEADS):
            pairs = jnp.concatenate([q_ref[0, rows, p * LANES:(p + 1) * LANES] for p in (2 * g, 2 * g + 1)], axis=0)
            for side in range(2):
                off = LANES * ((g + side) % 2)
                keep = left if side == 0 else jnp.logical_not(left)
                sk = jnp.where(first, sink_ref[4 * g + side], sink_ref[4 * g + 2 + side])
                recs.append(dict(q=jnp.where(keep, pairs, zero), k=k[win, off:off + LANES],
                                 v=jnp.where(keep, v[win, off:off + LANES], zero), sk=sk,
                                 mask_prev=mask_prev, mask_next=mask_next))
    for c in recs:
        sc = _dot_nt(c["q"], c["k"])
        c["sc"] = jnp.concatenate([jnp.where(c["mask_prev"], sc[:, :blk], -1e30), sc[:, blk:2 * blk],
                                   jnp.where(c["mask_next"], sc[:, 2 * blk:], -1e30)], axis=1)
    for c in recs:
        m = jnp.maximum(jnp.max(c["sc"], axis=-1, keepdims=True), c["sk"])
        e = jnp.exp(c["sc"] - m)
        c["den"] = jnp.sum(e, axis=-1, keepdims=True) + jnp.exp(c["sk"] - m)
        c["e"] = e.astype(BF16)
    for c in recs:
        c["o"] = _dot(c["e"], c["v"]) / c["den"]
    for j in range(nsub):
        outs = []
        for g in range(KV_HEADS):
            both = recs[4 * j + 2 * g]["o"] + recs[4 * j + 2 * g + 1]["o"]
            outs += [both[:blk], both[blk:]]
        o_ref[0, j * blk:(j + 1) * blk, :] = jnp.concatenate(outs, axis=1).astype(BF16)


def _attn(sink, q, k2, v2):
    b, s, _ = q.shape
    nsub = ATTN_SUB
    nb = s // (ATTN_BLOCK * nsub)
    last = s // ATTN_BLOCK - 1
    edge = lambda fn: pl.BlockSpec((1, ATTN_BLOCK, 2 * KV_WIDTH), fn)
    prev = lambda bi, n: (bi, jnp.maximum(n * nsub - 1, 0), 0)
    cur = lambda bi, n: (bi, n, 0)
    nxt = lambda bi, n: (bi, jnp.minimum((n + 1) * nsub, last), 0)
    mid = pl.BlockSpec((1, ATTN_BLOCK * nsub, 2 * KV_WIDTH), cur)
    return pl.pallas_call(
        _attn_kernel,
        grid=(b, nb),
        in_specs=[pl.BlockSpec(memory_space=pltpu.SMEM),
                  pl.BlockSpec((1, ATTN_BLOCK * nsub, ATTN_WIDTH), cur),
                  edge(prev), mid, edge(nxt), edge(prev), mid, edge(nxt)],
        out_specs=pl.BlockSpec((1, ATTN_BLOCK * nsub, ATTN_WIDTH), cur),
        out_shape=jax.ShapeDtypeStruct((b, s, ATTN_WIDTH), BF16),
        compiler_params=_cparams(("parallel", "parallel")),
        name="attn",
    )(sink, q, k2, k2, k2, v2, v2, v2)


def _scan_kernel(r0_ref, v0_ref, a0_ref, lw0_ref, k0_ref, b0_ref,
                 r1_ref, v1_ref, a1_ref, lw1_ref, k1_ref, b1_ref,
                 y0_ref, y1_ref, ht_ref):
    L, gw, nch = CHUNK, GROUP_W, SCAN_CHUNKS
    rep = gw // L
    ng = RWKV_WIDTH // gw
    rows = nch * L

    @pl.when(pl.program_id(1) == 0)
    def _():
        ht_ref[...] = jnp.zeros_like(ht_ref)

    tau = lax.broadcasted_iota(jnp.int32, (L, gw), 0)
    sig = lax.broadcasted_iota(jnp.int32, (L, gw), 1) % L
    bd = (lax.broadcasted_iota(jnp.int32, (gw, gw), 0) // L) == (lax.broadcasted_iota(jnp.int32, (gw, gw), 1) // L)
    half = 2
    ngrp = nch // half
    hrows = half * L
    ti = lax.broadcasted_iota(jnp.int32, (hrows, hrows), 0)
    si = lax.broadcasted_iota(jnp.int32, (hrows, hrows), 1)
    same_chunk = (ti // L) == (si // L)
    tris = ((same_chunk & (si <= ti)).astype(BF16), (same_chunk & (si >= ti)).astype(BF16))
    masks = ((sig < tau, sig <= tau), (sig > tau, sig >= tau))

    def bdtile(x):
        xb = x.astype(BF16)
        return jnp.where(bd, jnp.concatenate([xb] * rep, axis=0), jnp.zeros((), BF16))

    dirs = ((r0_ref, v0_ref, a0_ref, lw0_ref, k0_ref, b0_ref),
            (r1_ref, v1_ref, a1_ref, lw1_ref, k1_ref, b1_ref))

    def build(grp):
        out = {}
        for d, (r_ref, v_ref, a_ref, lw_ref, k_ref, b_ref) in enumerate(dirs):
            base = (grp if d == 0 else ngrp - 1 - grp) * hrows
            rsl = slice(base, base + hrows)
            strict, incl = masks[d]
            lw = lw_ref[0, rsl, :]
            cl = _split_dot_left(tris[d], lw)
            e_in = jnp.exp(cl)
            e_inv = jnp.exp(-cl)
            e_ex = jnp.exp(cl - lw)
            a_t = a_ref[0, rsl, :].astype(F32) * e_ex
            r_t = r_ref[0, rsl, :].astype(F32) * e_in
            b_t = b_ref[0, rsl, :].astype(F32) * e_inv
            k_t = k_ref[0, rsl, :].astype(F32) * e_inv
            v_all = v_ref[0, rsl, :]
            for jj in range(half):
                rs = slice(jj * L, (jj + 1) * L)
                last = (jj + 1) * L - 1 if d == 0 else jj * L
                d_end = jnp.exp(cl[last:last + 1, :])
                b_h = b_t[rs] * d_end
                k_h = k_t[rs] * d_end
                for g in range(ng):
                    sl = slice(g * gw, (g + 1) * gw)
                    out[(base // L + jj, d, g)] = dict(
                        strict=strict, incl=incl, at=a_t[rs, sl], rt=r_t[rs, sl], vv=v_all[rs, sl], bt=b_t[rs, sl],
                        kt=k_t[rs, sl], bh=b_h[:, sl], kh=k_h[:, sl], dend=d_end[:, sl])
        return out

    def s_gram(par):
        for c in par:
            lhs = jnp.concatenate([c["at"], c["rt"]], axis=0).astype(BF16)
            gram = _dot_nt(lhs, jnp.concatenate([bdtile(c["bt"]), bdtile(c["kt"])], axis=0))
            c["gb"], c["gk"] = gram[:, :gw], gram[:, gw:]

    def s_vprod(par):
        for c in par:
            c["m_rb"] = jnp.where(c["incl"], c["gb"][L:], 0.0).astype(BF16)
            c["p"] = jnp.where(c["strict"], c["gb"][:L], 0.0)
            m_k = jnp.concatenate([jnp.where(c["strict"], c["gk"][:L], 0.0), jnp.where(c["incl"], c["gk"][L:], 0.0)],
                                  axis=0)
            res = _dot(m_k.astype(BF16), bdtile(c["vv"]))
            c["makv"], c["mrkv"] = res[:L], res[L:]

    levels = int(math.log2(L))

    def s_level(lvl):
        def run(par):
            for c in par:
                ops = ([] if lvl == 0 else [bdtile(c["t"])]) + ([] if lvl == levels - 1 else [bdtile(c["p"])])
                c["res"] = _dot(c["p"].astype(BF16), ops[0] if len(ops) == 1 else jnp.concatenate(ops, axis=1))
            for c in par:
                if lvl == 0:
                    c["t"] = jnp.where(sig == tau, 1.0, c["p"])
                    c["p"] = c["res"]
                else:
                    c["t"] = c["t"] + c["res"][:, :gw]
                    if lvl < levels - 1:
                        c["p"] = c["res"][:, gw:]
        return run

    def s_apply(par):
        for c in par:
            res = _dot(c["t"].astype(BF16), jnp.concatenate([bdtile(c["at"]), bdtile(c["makv"])], axis=1))
            c["ah"], c["uloc"] = res[:, :gw], res[:, gw:]

    def s_g(par):
        for c in par:
            c["bkh"] = jnp.concatenate([c["bh"], c["kh"]], axis=0).astype(BF16)
            c["g"] = jnp.where(bd, _dot_tn(c["ah"].astype(BF16), c["bkh"][:L]), 0.0).astype(BF16)

    def s_c(par):
        for c in par:
            c["cst"] = jnp.where(bd, _dot_tn(jnp.concatenate([c["uloc"].astype(BF16), c["vv"]], axis=0), c["bkh"]),
                                 0.0)

    def s_q(par):
        for c in par:
            res = _dot(c["m_rb"], jnp.concatenate([bdtile(c["ah"]), bdtile(c["uloc"])], axis=1))
            c["qh"] = (c["rt"] + res[:, :gw]).astype(BF16)
            c["yloc"] = res[:, gw:] + c["mrkv"]

    stages = [s_gram, s_vprod] + [s_level(lvl) for lvl in range(levels)] + [s_apply, s_g, s_c, s_q]

    state = {(d, g): ht_ref[d, g] for d in range(2) for g in range(ng)}
    recs = {}

    def seq_step(step):
        cur = [(d, g, recs[(step if d == 0 else nch - 1 - step, d, g)]) for d in range(2) for g in range(ng)]
        for d, g, c in cur:
            c["htb"] = state[(d, g)].astype(BF16)
        for d, g, c in cur:
            state[(d, g)] = state[(d, g)] * c["dend"] + c["cst"] + _dot(c["htb"], c["g"])
        for d, g, c in cur:
            c["y"] = c["yloc"] + _dot_nt(c["qh"], c["htb"])

    slots = {(i + 1) * len(stages) // (half + 1): i for i in range(half)}
    groups = [build(0)]
    recs.update(groups[0])
    for k in range(ngrp):
        for s, fn in enumerate(stages):
            if s == 2 and k + 1 < ngrp:
                groups.append(build(k + 1))
                recs.update(groups[k + 1])
            fn(list(groups[k].values()))
            if k > 0 and s in slots:
                seq_step((k - 1) * half + slots[s])
    for step in range(nch - half, nch):
        seq_step(step)
    for (d, g), h in state.items():
        ht_ref[d, g] = h
    for d, y_ref in enumerate((y0_ref, y1_ref)):
        y_ref[0] = jnp.concatenate(
            [jnp.concatenate([recs[(j, d, g)]["y"] for g in range(ng)], axis=1) for j in range(nch)], axis=0)


def _split_dot_left(w_bf16, x):
    hi = x.astype(BF16)
    lo = (x - hi.astype(F32)).astype(BF16)
    return _dot(w_bf16, hi) + _dot(w_bf16, lo)


def _scan(r, v, a, lw0, lw1, k0, k1, b0, b1):
    b, s, c = r.shape
    rows = CHUNK * SCAN_CHUNKS
    nblk = s // rows
    fwd = pl.BlockSpec((1, rows, c), lambda bi, i: (bi, i, 0))
    bwd = pl.BlockSpec((1, rows, c), lambda bi, i: (bi, nblk - 1 - i, 0))
    ng = c // GROUP_W
    return pl.pallas_call(
        _scan_kernel,
        grid=(b, nblk),
        in_specs=[fwd] * 6 + [bwd] * 6,
        out_specs=[fwd, bwd],
        out_shape=[jax.ShapeDtypeStruct((b, s, c), F32)] * 2,
        scratch_shapes=[pltpu.VMEM((2, ng, GROUP_W, GROUP_W), F32)],
        compiler_params=_cparams(("parallel", "arbitrary")),
        name="scan",
    )(r, v, a, lw0, k0, b0, r, v, a, lw1, k1, b1)


def _merge_kernel(x_ref, gain_ref, wg_ref, attn_ref, y0_ref, y1_ref, bonus_ref, g_ref, lnw_ref, lnb_ref,
                  bd_ref, wua_ref, wur_ref, wout_ref, o_ref):
    x = x_ref[0]
    d = x.shape[1]
    h = _rms(x, gain_ref[...]).astype(BF16)
    gates = _sigmoid(_dot(h, wg_ref[...]))
    bd = bd_ref[...]
    y = y0_ref[0] + y1_ref[0]
    yc = y - _head_sum(y, bd) * (1.0 / HEAD_DIM)
    var = _head_sum(yc * yc, bd, split=False) * (1.0 / HEAD_DIM)
    yn = yc * lax.rsqrt(var + GN_EPS) * lnw_ref[...] + lnb_ref[...]
    rw = (yn + bonus_ref[0]) * g_ref[0]
    merged = (gates[:, :d] * _dot(attn_ref[0], wua_ref[...])
              + gates[:, d:] * _dot(rw.astype(BF16), wur_ref[...]))
    o_ref[0] = x + _dot(merged.astype(BF16), wout_ref[...])


def _merge(x, gain, wg, attn, y0, y1, bonus, g, lnw, lnb, bd, wua, wur, wout, tm):
    b, s, d = x.shape
    c = RWKV_WIDTH
    const = lambda shape: pl.BlockSpec(shape, lambda bi, i: (0,) * len(shape))
    tok = lambda width: pl.BlockSpec((1, tm, width), lambda bi, i: (bi, i, 0))
    return pl.pallas_call(
        _merge_kernel,
        grid=(b, s // tm),
        in_specs=[tok(d), const((1, d)), const((d, 2 * d)), tok(ATTN_WIDTH), tok(c), tok(c), tok(c), tok(c),
                  const((1, c)), const((1, c)), const((LANES, LANES)), const((ATTN_WIDTH, d)), const((c, d)),
                  const((d, d))],
        out_specs=tok(d),
        out_shape=jax.ShapeDtypeStruct((b, s, d), F32),
        compiler_params=_cparams(("parallel", "parallel")),
        name="merge",
    )(x, gain, wg, attn, y0, y1, bonus, g, lnw, lnb, bd, wua, wur, wout)


FF_SLAB = 1024


def _ffn_kernel(x_ref, p_ref, gf_ref, w1_ref, w2_ref, gp_ref, wpg_ref, wple_ref, o_ref):
    x = x_ref[0]
    h = _rms(x, gf_ref[...]).astype(BF16)
    acc = x
    for j in range(0, w1_ref.shape[1], FF_SLAB):
        hid = jnp.maximum(_dot(h, w1_ref[:, j:j + FF_SLAB]), 0.0)
        acc = acc + _dot((hid * hid).astype(BF16), w2_ref[j:j + FF_SLAB, :])
    hp = _rms(acc, gp_ref[...]).astype(BF16)
    o_ref[0] = acc + _dot(p_ref[0].astype(BF16), wple_ref[...]) * _sigmoid(_dot(hp, wpg_ref[...]))


def _ffn(x, p, gf, w1, w2, gp, wpg, wple, tm):
    b, s, d = x.shape
    dff = w1.shape[1]
    pd = p.shape[-1]
    const = lambda shape: pl.BlockSpec(shape, lambda bi, i: (0,) * len(shape))
    tok = lambda width: pl.BlockSpec((1, tm, width), lambda bi, i: (bi, i, 0))
    return pl.pallas_call(
        _ffn_kernel,
        grid=(b, s // tm),
        in_specs=[tok(d), tok(pd), const((1, d)), const((d, dff)), const((dff, d)), const((1, d)),
                  const((d, d)), const((pd, d))],
        out_specs=tok(d),
        out_shape=jax.ShapeDtypeStruct((b, s, d), F32),
        compiler_params=_cparams(("parallel", "parallel")),
        name="ffn",
    )(x, p, gf, w1, w2, gp, wpg, wple)


def _rotary_tables(s):
    half = ROT_DIM // 2
    inv_freq = jnp.power(jnp.float32(ROPE_THETA), -jnp.arange(half, dtype=F32) * 2.0 / ROT_DIM)
    ang = jnp.arange(s).astype(F32)[:, None] * inv_freq[None, :]
    cos, sin = jnp.cos(ang), jnp.sin(ang)
    pad = jnp.zeros((s, HEAD_DIM - ROT_DIM), F32)
    zero = jnp.zeros((s, half), F32)
    c = jnp.concatenate([cos, cos, pad + 1.0], axis=1)
    s1 = jnp.concatenate([-sin, zero, pad], axis=1)
    s2 = jnp.concatenate([zero, sin, pad], axis=1)
    rep = LANES // HEAD_DIM
    return jnp.tile(c, (1, rep)), jnp.tile(s1, (1, rep)), jnp.tile(s2, (1, rep))


def _swap_halves(w):
    return jnp.concatenate([w[:, HEAD_DIM:], w[:, :HEAD_DIM]], axis=1)


def kernel(x, p, norm_mix, w_in, shift_mu, q_norm, k_norm, sink, w0, w2, a0, a2, g2, k_k, k_a, r_k, lnx_w, lnx_b,
           w_up_attn, w_up_rwkv, w_out, norm_ffn, w_ff1, w_ff2, norm_ple, w_ple_gate, w_ple):
    bsz, s, d = x.shape
    depth = w_in.shape[0]
    c = RWKV_WIDTH
    tm = min(512, s)
    tt = min(512, s)
    cos, s1, s2 = _rotary_tables(s)
    lane = jnp.arange(LANES)
    bd = ((lane[:, None] // HEAD_DIM) == (lane[None, :] // HEAD_DIM)).astype(BF16)
    o_k, o_v, o_r, o_g = ATTN_WIDTH, ATTN_WIDTH + KV_WIDTH, ATTN_WIDTH + 2 * KV_WIDTH, ATTN_WIDTH + 2 * KV_WIDTH + RWKV_COLS
    for i in range(depth):
        wi = w_in[i]
        wk, wv = wi[:, o_k:o_v], wi[:, o_v:o_r]
        w_proj = jnp.concatenate([wi[:, :o_k], wk, _swap_halves(wk), wv, _swap_halves(wv), wi[:, o_r:o_g]],
                                 axis=1).astype(BF16)
        qkg = jnp.concatenate([jnp.tile(q_norm[i], Q_HEADS), jnp.tile(k_norm[i], 2 * KV_HEADS)])[None, :]
        zpad = jnp.zeros((DECAY_RANK, c), F32)
        wlo = jnp.concatenate([jnp.concatenate([w2[i, 0], zpad], axis=0), jnp.concatenate([w2[i, 1], zpad], axis=0),
                               jnp.concatenate([zpad, a2[i, 0]], axis=0), jnp.concatenate([zpad, a2[i, 1]], axis=0)],
                              axis=1).astype(BF16)
        (q, k2, v2, r, v, a, lw0, lw1, k0, k1, b0, b1, g, bonus) = _front(
            x, norm_mix[i][None, :], w_proj, cos, s1, s2, qkg, bd, shift_mu[i][None, :], wlo, g2[i].astype(BF16),
            w0[i], a0[i], k_k[i][None, :], k_a[i][None, :], r_k[i].reshape(2, c), tm)
        attn = _attn(sink[i], q, k2, v2)
        y0, y1 = _scan(r, v, a, lw0, lw1, k0, k1, b0, b1)

        x = _merge(x, norm_mix[i][None, :], wi[:, o_g:].astype(BF16), attn, y0, y1, bonus, g,
                   lnx_w[i][None, :], lnx_b[i][None, :], bd, w_up_attn[i].astype(BF16), w_up_rwkv[i].astype(BF16),
                   w_out[i].astype(BF16), tm)
        x = _ffn(x, p[i], norm_ffn[i][None, :], w_ff1[i].astype(BF16), w_ff2[i].astype(BF16), norm_ple[i][None, :],
                 w_ple_gate[i].astype(BF16), w_ple[i].astype(BF16), tm)
    return x
```

```python
import functools
import math

import jax
import jax.numpy as jnp
from jax import lax
from jax.experimental import pallas as pl
from jax.experimental.pallas import tpu as pltpu

F32 = jnp.float32
BF16 = jnp.bfloat16

LANES = 128
HEAD_DIM = 64
Q_HEADS = 8
KV_HEADS = 2
ATTN_WIDTH = Q_HEADS * HEAD_DIM
KV_WIDTH = KV_HEADS * HEAD_DIM
WINDOW = 128
ATTN_BLOCK = 128
ATTN_SUB = 2
ROPE_THETA = 500000.0
ROT_DIM = HEAD_DIM // 4
RWKV_WIDTH = 512
DECAY_RANK = 64
ICLR_RANK = 64
GATE_RANK = 128
RWKV_COLS = 3 * RWKV_WIDTH + DECAY_RANK + ICLR_RANK + GATE_RANK
NORM_EPS = 1e-6
GN_EPS = 64e-5
CHUNK = 64
GROUP_W = 128
SCAN_CHUNKS = 8
VMEM_LIMIT = 56 * 1024 * 1024


def _cparams(sem):
    return pltpu.CompilerParams(dimension_semantics=sem, vmem_limit_bytes=VMEM_LIMIT)


def _dot(a, b):
    return jnp.dot(a, b, preferred_element_type=F32)


def _dot_nt(a, b):
    return lax.dot_general(a, b, (((1,), (1,)), ((), ())), preferred_element_type=F32)


def _dot_tn(a, b):
    return lax.dot_general(a, b, (((0,), (0,)), ((), ())), preferred_element_type=F32)


def _split_dot(x, w_bf16):
    hi = x.astype(BF16)
    lo = (x - hi.astype(F32)).astype(BF16)
    return _dot(hi, w_bf16) + _dot(lo, w_bf16)


def _head_sum(x, bd, split=True):
    one = _split_dot if split else (lambda t, w: _dot(t.astype(BF16), w))
    cols = [one(x[:, i:i + LANES], bd) for i in range(0, x.shape[1], LANES)]
    return cols[0] if len(cols) == 1 else jnp.concatenate(cols, axis=1)


def _rms(x, gain):
    return x * lax.rsqrt(jnp.mean(x * x, axis=-1, keepdims=True) + NORM_EPS) * gain


def _sigmoid(x):
    return 0.5 * jnp.tanh(0.5 * x) + 0.5


QK_W = ATTN_WIDTH + 2 * KV_WIDTH
QKV_W = QK_W + 2 * KV_WIDTH
PROJ_W = QKV_W + RWKV_COLS
HALO = 8
FRONT_SUB = 128


def _qkv_rows(proj, cos, s1, s2, qkg, bd):
    qk = proj[:, :QK_W]
    ms = _head_sum(qk * qk, bd, split=False) * (1.0 / HEAD_DIM)
    qk = qk * lax.rsqrt(ms + NORM_EPS) * qkg
    tiles = []
    for i in range(0, QK_W, LANES):
        t = qk[:, i:i + LANES]
        t = t * cos + pltpu.roll(t, LANES - ROT_DIM // 2, 1) * s1 + pltpu.roll(t, ROT_DIM // 2, 1) * s2
        tiles.append(t)
    q = (jnp.concatenate(tiles[:4], axis=1) * (HEAD_DIM ** -0.5)).astype(BF16)
    return q, jnp.concatenate(tiles[4:], axis=1).astype(BF16), proj[:, QK_W:QKV_W].astype(BF16)


def _rwkv_rows(z, prev_row, next_row, mu, wlo, g2, w0, a0, k_k, k_a, r_k, bd):
    n = z.shape[0]
    row = lax.broadcasted_iota(jnp.int32, (HALO, 1), 0)
    ri = lax.broadcasted_iota(jnp.int32, (n, n), 0)
    ci = lax.broadcasted_iota(jnp.int32, (n, n), 1)
    adjacent = ((ri - ci == 1) | (ci - ri == 1)).astype(BF16)
    mix = 0.5 * mu
    shifted = z * (1.0 - mu) + _dot(adjacent, z.astype(BF16)) * mix
    top = shifted[:HALO] + jnp.where(row == 0, prev_row * mix, 0.0)
    bot = shifted[n - HALO:] + jnp.where(row == HALO - 1, next_row * mix, 0.0)
    z = jnp.concatenate([top, shifted[HALO:n - HALO], bot], axis=0)

    c = RWKV_WIDTH
    r, k, v = z[:, :c], z[:, c:2 * c], z[:, 2 * c:3 * c]
    lowrank = z[:, 3 * c:3 * c + LANES]
    left = lax.broadcasted_iota(jnp.int32, (1, LANES), 1) < DECAY_RANK
    lowrank = jnp.where(left, jnp.tanh(lowrank), lowrank).astype(BF16)
    wa = _dot(lowrank, wlo)
    g = _dot(_sigmoid(z[:, 3 * c + LANES:]).astype(BF16), g2)
    kk = k * k_k
    kk = kk * lax.rsqrt(jnp.maximum(_head_sum(kk * kk, bd, split=False), 1e-24))
    lws, ks, bs = [], [], []
    dot_rk = None
    for d in range(2):
        w_raw = w0[d:d + 1, :] + wa[:, d * c:(d + 1) * c]
        lws.append(-math.exp(-0.5) * _sigmoid(w_raw))
        rate = _sigmoid(a0[d:d + 1, :] + wa[:, (2 + d) * c:(3 + d) * c])
        k_dir = k * (1.0 + (rate - 1.0) * k_a)
        ks.append(k_dir.astype(BF16))
        bs.append((kk * rate).astype(BF16))
        term = k_dir * r_k[d:d + 1, :]
        dot_rk = term if dot_rk is None else dot_rk + term
    bonus = _head_sum(r * dot_rk, bd) * v
    return (r.astype(BF16), v.astype(BF16), (-kk).astype(BF16), lws[0], lws[1], ks[0], ks[1], bs[0], bs[1], g, bonus)


def _qk_rows(qk, cos, s1, s2, gains, bd):
    ms = _head_sum(qk * qk, bd, split=False) * (1.0 / HEAD_DIM)
    qk = qk * lax.rsqrt(ms + NORM_EPS) * gains
    tiles = []
    for i in range(0, qk.shape[1], LANES):
        t = qk[:, i:i + LANES]
        t = t * cos + pltpu.roll(t, LANES - ROT_DIM // 2, 1) * s1 + pltpu.roll(t, ROT_DIM // 2, 1) * s2
        tiles.append(t)
    return jnp.concatenate(tiles, axis=1)


def _front_kernel(x_ref, xp_ref, xn_ref, gain_ref, w_ref, cos_ref, s1_ref, s2_ref, qkg_ref, bd_ref, mu_ref, wlo_ref,
                  g2_ref, w0_ref, a0_ref, kk_ref, ka_ref, rk_ref, q_ref, k_ref, v_ref,
                  r_ref, vv_ref, a_ref, lw0_ref, lw1_ref, k0_ref, k1_ref, b0_ref, b1_ref, g_ref, bonus_ref):
    i = pl.program_id(1)
    nt = pl.num_programs(1)
    tm = x_ref.shape[1]
    sub = FRONT_SUB
    c = RWKV_WIDTH
    gain = gain_ref[...]
    bd = bd_ref[...]
    o_r, o_k, o_v, o_low = QKV_W, QKV_W + c, QKV_W + 2 * c, QKV_W + 3 * c
    halo = jnp.concatenate([xp_ref[0], xn_ref[0]], axis=0)
    h_blocks = [_rms(halo, gain).astype(BF16)]
    low_blocks = [_dot(h_blocks[0], w_ref[:, o_low:PROJ_W])]
    for s in range(0, tm, sub):
        h_blocks.insert(-1, _rms(x_ref[0, s:s + sub, :], gain).astype(BF16))
        low_blocks.insert(-1, _dot(h_blocks[-2], w_ref[:, o_low:PROJ_W]))
    h_ext = jnp.concatenate(h_blocks, axis=0)
    h = h_ext[:tm]
    z_low = jnp.concatenate(low_blocks, axis=0)

    row = lax.broadcasted_iota(jnp.int32, (HALO, 1), 0)
    ri = lax.broadcasted_iota(jnp.int32, (sub, sub), 0)
    ci = lax.broadcasted_iota(jnp.int32, (sub, sub), 1)
    adjacent = ((ri - ci == 1) | (ci - ri == 1)).astype(BF16)

    def project(lo, hi):
        return _dot(h_ext if lo >= QKV_W else h, w_ref[:, lo:hi])

    def shift(z, lo, hi):
        mu = mu_ref[:, lo:hi]
        mix = 0.5 * mu
        keep = 1.0 - mu
        prev_row = jnp.where(i > 0, z[tm + HALO - 1:tm + HALO], 0.0)
        next_row = jnp.where(i < nt - 1, z[tm + HALO:tm + HALO + 1], 0.0)
        blocks = []
        for s in range(0, tm, sub):
            zs = z[s:s + sub]
            part = zs * keep + _dot(adjacent, zs.astype(BF16)) * mix
            before = z[s - 1:s] if s > 0 else prev_row
            after = z[s + sub:s + sub + 1] if s + sub < tm else next_row
            blocks += [part[:HALO] + jnp.where(row == 0, before * mix, 0.0), part[HALO:sub - HALO],
                       part[sub - HALO:] + jnp.where(row == HALO - 1, after * mix, 0.0)]
        return jnp.concatenate(blocks, axis=0)

    half = c // 2
    cos, s1, s2 = cos_ref[...], s1_ref[...], s2_ref[...]

    def q_rows(p, j):
        q = _qk_rows(p, cos, s1, s2, qkg_ref[:, j * half:(j + 1) * half], bd)
        q_ref[0, :, j * half:(j + 1) * half] = (q * (HEAD_DIM ** -0.5)).astype(BF16)

    z_k0 = project(o_k, o_k + half)
    z_low = shift(z_low, 3 * c, RWKV_COLS)
    lowrank = z_low[:, :LANES]
    left = lax.broadcasted_iota(jnp.int32, (1, LANES), 1) < DECAY_RANK
    lowrank = jnp.where(left, jnp.tanh(lowrank), lowrank).astype(BF16)
    gate_in = _sigmoid(z_low[:, LANES:]).astype(BF16)

    z_k1 = project(o_k + half, o_k + c)
    wa = _dot(lowrank, wlo_ref[...])
    g_ref[0] = _dot(gate_in, g2_ref[...])
    p_q0 = project(0, half)

    lw0_ref[0] = -math.exp(-0.5) * _sigmoid(w0_ref[0:1, :] + wa[:, :c])
    p_q1 = project(half, c)
    lw1_ref[0] = -math.exp(-0.5) * _sigmoid(w0_ref[1:2, :] + wa[:, c:2 * c])
    z_r0 = project(o_r, o_r + half)

    k = shift(jnp.concatenate([z_k0, z_k1], axis=1), c, 2 * c)
    kk = k * kk_ref[...]
    kk = kk * lax.rsqrt(jnp.maximum(_head_sum(kk * kk, bd, split=False), 1e-24))
    a_ref[0] = (-kk).astype(BF16)
    z_r1 = project(o_r + half, o_r + c)

    later = [lambda: project(o_v, o_v + half), lambda: project(o_v + half, o_v + c)]
    z_v = []
    dot_rk = None
    for d, (kd_ref, bdir_ref) in enumerate(((k0_ref, b0_ref), (k1_ref, b1_ref))):
        rate = _sigmoid(a0_ref[d:d + 1, :] + wa[:, (2 + d) * c:(3 + d) * c])
        k_dir = k * (1.0 + (rate - 1.0) * ka_ref[...])
        kd_ref[0] = k_dir.astype(BF16)
        bdir_ref[0] = (kk * rate).astype(BF16)
        term = k_dir * rk_ref[d:d + 1, :]
        dot_rk = term if dot_rk is None else dot_rk + term
        z_v.append(later[d]())
        q_rows((p_q0, p_q1)[d], d)

    p_k2 = project(ATTN_WIDTH, ATTN_WIDTH + 2 * KV_WIDTH)
    r = shift(jnp.concatenate([z_r0, z_r1], axis=1), 0, c)
    r_ref[0] = r.astype(BF16)
    r_dot = _head_sum(r * dot_rk, bd, split=False)
    p_v2 = project(ATTN_WIDTH + 2 * KV_WIDTH, QKV_W)
    k_ref[0] = _qk_rows(p_k2, cos, s1, s2, qkg_ref[:, ATTN_WIDTH:], bd).astype(BF16)
    v = shift(jnp.concatenate(z_v, axis=1), 2 * c, 3 * c)
    vv_ref[0] = v.astype(BF16)
    bonus_ref[0] = r_dot * v
    v_ref[0] = p_v2.astype(BF16)


def _front(x, gain, w, cos, s1, s2, qkg, bd, mu, wlo, g2, w0, a0, k_k, k_a, r_k, tm):
    b, s, d = x.shape
    c = RWKV_WIDTH
    nh = tm // HALO
    const = lambda shape: pl.BlockSpec(shape, lambda bi, i: (0,) * len(shape))
    tok = lambda width: pl.BlockSpec((1, tm, width), lambda bi, i: (bi, i, 0))
    tab = pl.BlockSpec((tm, LANES), lambda bi, i: (i, 0))
    halo_p = pl.BlockSpec((1, HALO, d), lambda bi, i: (bi, jnp.maximum(i * nh - 1, 0), 0))
    halo_n = pl.BlockSpec((1, HALO, d), lambda bi, i: (bi, jnp.minimum((i + 1) * nh, s // HALO - 1), 0))
    rwkv_dtypes = (BF16,) * 3 + (F32,) * 2 + (BF16,) * 4 + (F32,) * 2
    return pl.pallas_call(
        _front_kernel,
        grid=(b, s // tm),
        in_specs=[tok(d), halo_p, halo_n, const((1, d)), const((d, PROJ_W)), tab, tab, tab, const((1, QK_W)),
                  const((LANES, LANES)), const((1, RWKV_COLS)), const((LANES, 4 * c)), const((GATE_RANK, c)),
                  const((2, c)), const((2, c)), const((1, c)), const((1, c)), const((2, c))],
        out_specs=[tok(ATTN_WIDTH), tok(2 * KV_WIDTH), tok(2 * KV_WIDTH)] + [tok(c)] * 11,
        out_shape=[jax.ShapeDtypeStruct((b, s, ATTN_WIDTH), BF16),
                   jax.ShapeDtypeStruct((b, s, 2 * KV_WIDTH), BF16),
                   jax.ShapeDtypeStruct((b, s, 2 * KV_WIDTH), BF16)]
                  + [jax.ShapeDtypeStruct((b, s, c), dt) for dt in rwkv_dtypes],
        compiler_params=_cparams(("parallel", "parallel")),
        name="front",
    )(x, x, x, gain, w, cos, s1, s2, qkg, bd, mu, wlo, g2, w0, a0, k_k, k_a, r_k)


def _attn_kernel(sink_ref, q_ref, kp_ref, kc_ref, kn_ref, vp_ref, vc_ref, vn_ref, o_ref):
    n = pl.program_id(1)
    nb = pl.num_programs(1)
    blk, nsub = ATTN_BLOCK, ATTN_SUB
    k = jnp.concatenate([kp_ref[0], kc_ref[0], kn_ref[0]], axis=0)
    v = jnp.concatenate([vp_ref[0], vc_ref[0], vn_ref[0]], axis=0)
    qi = lax.broadcasted_iota(jnp.int32, (2 * blk, blk), 0) % blk
    ki = lax.broadcasted_iota(jnp.int32, (2 * blk, blk), 1)
    first = lax.broadcasted_iota(jnp.int32, (2 * blk, 1), 0) < blk
    left = lax.broadcasted_iota(jnp.int32, (1, LANES), 1) < HEAD_DIM
    zero = jnp.zeros((), BF16)
    recs = []
    for j in range(nsub):
        rows = slice(j * blk, (j + 1) * blk)
        win = slice(j * blk, (j + 3) * blk)
        mask_prev = (ki >= qi) & ((n > 0) if j == 0 else True)
        mask_next = (ki <= qi) & ((n < nb - 1) if j == nsub - 1 else True)
        for g in range(KV_HEADS):
            pairs = jnp.concatenate([q_ref[0, rows, p * LANES:(p + 1) * LANES] for p in (2 * g, 2 * g + 1)], axis=0)
            for side in range(2):
                off = LANES * ((g + side) % 2)
                keep = left if side == 0 else jnp.logical_not(left)
                sk = jnp.where(first, sink_ref[4 * g + side], sink_ref[4 * g + 2 + side])
                recs.append(dict(q=jnp.where(keep, pairs, zero), k=k[win, off:off + LANES],
                                 v=jnp.where(keep, v[win, off:off + LANES], zero), sk=sk,
                                 mask_prev=mask_prev, mask_next=mask_next))
    for c in recs:
        sc = _dot_nt(c["q"], c["k"])
        c["sc"] = jnp.concatenate([jnp.where(c["mask_prev"], sc[:, :blk], -1e30), sc[:, blk:2 * blk],
                                   jnp.where(c["mask_next"], sc[:, 2 * blk:], -1e30)], axis=1)
    for c in recs:
        m = jnp.maximum(jnp.max(c["sc"], axis=-1, keepdims=True), c["sk"])
        e = jnp.exp(c["sc"] - m)
        c["den"] = jnp.sum(e, axis=-1, keepdims=True) + jnp.exp(c["sk"] - m)
        c["e"] = e.astype(BF16)
    for c in recs:
        c["o"] = _dot(c["e"], c["v"]) / c["den"]
    for j in range(nsub):
        outs = []
        for g in range(KV_HEADS):
            both = recs[4 * j + 2 * g]["o"] + recs[4 * j + 2 * g + 1]["o"]
            outs += [both[:blk], both[blk:]]
        o_ref[0, j * blk:(j + 1) * blk, :] = jnp.concatenate(outs, axis=1).astype(BF16)


def _attn(sink, q, k2, v2):
    b, s, _ = q.shape
    nsub = ATTN_SUB
    nb = s // (ATTN_BLOCK * nsub)
    last = s // ATTN_BLOCK - 1
    edge = lambda fn: pl.BlockSpec((1, ATTN_BLOCK, 2 * KV_WIDTH), fn)
    prev = lambda bi, n: (bi, jnp.maximum(n * nsub - 1, 0), 0)
    cur = lambda bi, n: (bi, n, 0)
    nxt = lambda bi, n: (bi, jnp.minimum((n + 1) * nsub, last), 0)
    mid = pl.BlockSpec((1, ATTN_BLOCK * nsub, 2 * KV_WIDTH), cur)
    return pl.pallas_call(
        _attn_kernel,
        grid=(b, nb),
        in_specs=[pl.BlockSpec(memory_space=pltpu.SMEM),
                  pl.BlockSpec((1, ATTN_BLOCK * nsub, ATTN_WIDTH), cur),
                  edge(prev), mid, edge(nxt), edge(prev), mid, edge(nxt)],
        out_specs=pl.BlockSpec((1, ATTN_BLOCK * nsub, ATTN_WIDTH), cur),
        out_shape=jax.ShapeDtypeStruct((b, s, ATTN_WIDTH), BF16),
        compiler_params=_cparams(("parallel", "parallel")),
        name="attn",
    )(sink, q, k2, k2, k2, v2, v2, v2)


def _scan_kernel(r0_ref, v0_ref, a0_ref, lw0_ref, k0_ref, b0_ref,
                 r1_ref, v1_ref, a1_ref, lw1_ref, k1_ref, b1_ref,
                 y0_ref, y1_ref, ht_ref):
    L, gw, nch = CHUNK, GROUP_W, SCAN_CHUNKS
    rep = gw // L
    ng = RWKV_WIDTH // gw
    rows = nch * L

    @pl.when(pl.program_id(1) == 0)
    def _():
        ht_ref[...] = jnp.zeros_like(ht_ref)

    tau = lax.broadcasted_iota(jnp.int32, (L, gw), 0)
    sig = lax.broadcasted_iota(jnp.int32, (L, gw), 1) % L
    bd = (lax.broadcasted_iota(jnp.int32, (gw, gw), 0) // L) == (lax.broadcasted_iota(jnp.int32, (gw, gw), 1) // L)
    half = 2
    ngrp = nch // half
    hrows = half * L
    ti = lax.broadcasted_iota(jnp.int32, (hrows, hrows), 0)
    si = lax.broadcasted_iota(jnp.int32, (hrows, hrows), 1)
    same_chunk = (ti // L) == (si // L)
    tris = ((same_chunk & (si <= ti)).astype(BF16), (same_chunk & (si >= ti)).astype(BF16))
    masks = ((sig < tau, sig <= tau), (sig > tau, sig >= tau))

    def bdtile(x):
        xb = x.astype(BF16)
        return jnp.where(bd, jnp.concatenate([xb] * rep, axis=0), jnp.zeros((), BF16))

    dirs = ((r0_ref, v0_ref, a0_ref, lw0_ref, k0_ref, b0_ref),
            (r1_ref, v1_ref, a1_ref, lw1_ref, k1_ref, b1_ref))

    def build(grp):
        out = {}
        for d, (r_ref, v_ref, a_ref, lw_ref, k_ref, b_ref) in enumerate(dirs):
            base = (grp if d == 0 else ngrp - 1 - grp) * hrows
            rsl = slice(base, base + hrows)
            strict, incl = masks[d]
            lw = lw_ref[0, rsl, :]
            cl = _split_dot_left(tris[d], lw)
            e_in = jnp.exp(cl)
            e_inv = jnp.exp(-cl)
            e_ex = jnp.exp(cl - lw)
            a_t = a_ref[0, rsl, :].astype(F32) * e_ex
            r_t = r_ref[0, rsl, :].astype(F32) * e_in
            b_t = b_ref[0, rsl, :].astype(F32) * e_inv
            k_t = k_ref[0, rsl, :].astype(F32) * e_inv
            v_all = v_ref[0, rsl, :]
            for jj in range(half):
                rs = slice(jj * L, (jj + 1) * L)
                last = (jj + 1) * L - 1 if d == 0 else jj * L
                d_end = jnp.exp(cl[last:last + 1, :])
                b_h = b_t[rs] * d_end
                k_h = k_t[rs] * d_end
                for g in range(ng):
                    sl = slice(g * gw, (g + 1) * gw)
                    out[(base // L + jj, d, g)] = dict(
                        strict=strict, incl=incl, at=a_t[rs, sl], rt=r_t[rs, sl], vv=v_all[rs, sl], bt=b_t[rs, sl],
                        kt=k_t[rs, sl], bh=b_h[:, sl], kh=k_h[:, sl], dend=d_end[:, sl])
        return out

    def s_gram(par):
        for c in par:
            lhs = jnp.concatenate([c["at"], c["rt"]], axis=0).astype(BF16)
            gram = _dot_nt(lhs, jnp.concatenate([bdtile(c["bt"]), bdtile(c["kt"])], axis=0))
            c["gb"], c["gk"] = gram[:, :gw], gram[:, gw:]

    def s_vprod(par):
        for c in par:
            c["m_rb"] = jnp.where(c["incl"], c["gb"][L:], 0.0).astype(BF16)
            c["p"] = jnp.where(c["strict"], c["gb"][:L], 0.0)
            m_k = jnp.concatenate([jnp.where(c["strict"], c["gk"][:L], 0.0), jnp.where(c["incl"], c["gk"][L:], 0.0)],
                                  axis=0)
            res = _dot(m_k.astype(BF16), bdtile(c["vv"]))
            c["makv"], c["mrkv"] = res[:L], res[L:]

    levels = int(math.log2(L))

    def s_level(lvl):
        def run(par):
            for c in par:
                ops = ([] if lvl == 0 else [bdtile(c["t"])]) + ([] if lvl == levels - 1 else [bdtile(c["p"])])
                c["res"] = _dot(c["p"].astype(BF16), ops[0] if len(ops) == 1 else jnp.concatenate(ops, axis=1))
            for c in par:
                if lvl == 0:
                    c["t"] = jnp.where(sig == tau, 1.0, c["p"])
                    c["p"] = c["res"]
                else:
                    c["t"] = c["t"] + c["res"][:, :gw]
                    if lvl < levels - 1:
                        c["p"] = c["res"][:, gw:]
        return run

    def s_apply(par):
        for c in par:
            res = _dot(c["t"].astype(BF16), jnp.concatenate([bdtile(c["at"]), bdtile(c["makv"])], axis=1))
            c["ah"], c["uloc"] = res[:, :gw], res[:, gw:]

    def s_g(par):
        for c in par:
            c["bkh"] = jnp.concatenate([c["bh"], c["kh"]], axis=0).astype(BF16)
            c["g"] = jnp.where(bd, _dot_tn(c["ah"].astype(BF16), c["bkh"][:L]), 0.0).astype(BF16)

    def s_c(par):
        for c in par:
            c["cst"] = jnp.where(bd, _dot_tn(jnp.concatenate([c["uloc"].astype(BF16), c["vv"]], axis=0), c["bkh"]),
                                 0.0)

    def s_q(par):
        for c in par:
            res = _dot(c["m_rb"], jnp.concatenate([bdtile(c["ah"]), bdtile(c["uloc"])], axis=1))
            c["qh"] = (c["rt"] + res[:, :gw]).astype(BF16)
            c["yloc"] = res[:, gw:] + c["mrkv"]

    stages = [s_gram, s_vprod] + [s_level(lvl) for lvl in range(levels)] + [s_apply, s_g, s_c, s_q]

    state = {(d, g): ht_ref[d, g] for d in range(2) for g in range(ng)}
    recs = {}

    def seq_step(step):
        cur = [(d, g, recs[(step if d == 0 else nch - 1 - step, d, g)]) for d in range(2) for g in range(ng)]
        for d, g, c in cur:
            c["htb"] = state[(d, g)].astype(BF16)
        for d, g, c in cur:
            state[(d, g)] = state[(d, g)] * c["dend"] + c["cst"] + _dot(c["htb"], c["g"])
        for d, g, c in cur:
            c["y"] = c["yloc"] + _dot_nt(c["qh"], c["htb"])

    slots = {(i + 1) * len(stages) // (half + 1): i for i in range(half)}
    groups = [build(0)]
    recs.update(groups[0])
    for k in range(ngrp):
        for s, fn in enumerate(stages):
            if s == 2 and k + 1 < ngrp:
                groups.append(build(k + 1))
                recs.update(groups[k + 1])
            fn(list(groups[k].values()))
            if k > 0 and s in slots:
                seq_step((k - 1) * half + slots[s])
    for step in range(nch - half, nch):
        seq_step(step)
    for (d, g), h in state.items():
        ht_ref[d, g] = h
    for d, y_ref in enumerate((y0_ref, y1_ref)):
        y_ref[0] = jnp.concatenate(
            [jnp.concatenate([recs[(j, d, g)]["y"] for g in range(ng)], axis=1) for j in range(nch)], axis=0)


def _split_dot_left(w_bf16, x):
    hi = x.astype(BF16)
    lo = (x - hi.astype(F32)).astype(BF16)
    return _dot(w_bf16, hi) + _dot(w_bf16, lo)


def _scan(r, v, a, lw0, lw1, k0, k1, b0, b1):
    b, s, c = r.shape
    rows = CHUNK * SCAN_CHUNKS
    nblk = s // rows
    fwd = pl.BlockSpec((1, rows, c), lambda bi, i: (bi, i, 0))
    bwd = pl.BlockSpec((1, rows, c), lambda bi, i: (bi, nblk - 1 - i, 0))
    ng = c // GROUP_W
    return pl.pallas_call(
        _scan_kernel,
        grid=(b, nblk),
        in_specs=[fwd] * 6 + [bwd] * 6,
        out_specs=[fwd, bwd],
        out_shape=[jax.ShapeDtypeStruct((b, s, c), F32)] * 2,
        scratch_shapes=[pltpu.VMEM((2, ng, GROUP_W, GROUP_W), F32)],
        compiler_params=_cparams(("parallel", "arbitrary")),
        name="scan",
    )(r, v, a, lw0, k0, b0, r, v, a, lw1, k1, b1)


def _merge_kernel(x_ref, gain_ref, wg_ref, attn_ref, y0_ref, y1_ref, bonus_ref, g_ref, lnw_ref, lnb_ref,
                  bd_ref, wua_ref, wur_ref, wout_ref, o_ref):
    x = x_ref[0]
    d = x.shape[1]
    h = _rms(x, gain_ref[...]).astype(BF16)
    gates = _sigmoid(_dot(h, wg_ref[...]))
    bd = bd_ref[...]
    y = y0_ref[0] + y1_ref[0]
    yc = y - _head_sum(y, bd) * (1.0 / HEAD_DIM)
    var = _head_sum(yc * yc, bd, split=False) * (1.0 / HEAD_DIM)
    yn = yc * lax.rsqrt(var + GN_EPS) * lnw_ref[...] + lnb_ref[...]
    rw = (yn + bonus_ref[0]) * g_ref[0]
    merged = (gates[:, :d] * _dot(attn_ref[0], wua_ref[...])
              + gates[:, d:] * _dot(rw.astype(BF16), wur_ref[...]))
    o_ref[0] = x + _dot(merged.astype(BF16), wout_ref[...])


def _merge(x, gain, wg, attn, y0, y1, bonus, g, lnw, lnb, bd, wua, wur, wout, tm):
    b, s, d = x.shape
    c = RWKV_WIDTH
    const = lambda shape: pl.BlockSpec(shape, lambda bi, i: (0,) * len(shape))
    tok = lambda width: pl.BlockSpec((1, tm, width), lambda bi, i: (bi, i, 0))
    return pl.pallas_call(
        _merge_kernel,
        grid=(b, s // tm),
        in_specs=[tok(d), const((1, d)), const((d, 2 * d)), tok(ATTN_WIDTH), tok(c), tok(c), tok(c), tok(c),
                  const((1, c)), const((1, c)), const((LANES, LANES)), const((ATTN_WIDTH, d)), const((c, d)),
                  const((d, d))],
        out_specs=tok(d),
        out_shape=jax.ShapeDtypeStruct((b, s, d), F32),
        compiler_params=_cparams(("parallel", "parallel")),
        name="merge",
    )(x, gain, wg, attn, y0, y1, bonus, g, lnw, lnb, bd, wua, wur, wout)


FF_SLAB = 1024


def _ffn_kernel(x_ref, p_ref, gf_ref, w1_ref, w2_ref, gp_ref, wpg_ref, wple_ref, o_ref):
    x = x_ref[0]
    h = _rms(x, gf_ref[...]).astype(BF16)
    acc = x
    for j in range(0, w1_ref.shape[1], FF_SLAB):
        hid = jnp.maximum(_dot(h, w1_ref[:, j:j + FF_SLAB]), 0.0)
        acc = acc + _dot((hid * hid).astype(BF16), w2_ref[j:j + FF_SLAB, :])
    hp = _rms(acc, gp_ref[...]).astype(BF16)
    o_ref[0] = acc + _dot(p_ref[0].astype(BF16), wple_ref[...]) * _sigmoid(_dot(hp, wpg_ref[...]))


def _ffn(x, p, gf, w1, w2, gp, wpg, wple, tm):
    b, s, d = x.shape
    dff = w1.shape[1]
    pd = p.shape[-1]
    const = lambda shape: pl.BlockSpec(shape, lambda bi, i: (0,) * len(shape))
    tok = lambda width: pl.BlockSpec((1, tm, width), lambda bi, i: (bi, i, 0))
    return pl.pallas_call(
        _ffn_kernel,
        grid=(b, s // tm),
        in_specs=[tok(d), tok(pd), const((1, d)), const((d, dff)), const((dff, d)), const((1, d)),
                  const((d, d)), const((pd, d))],
        out_specs=tok(d),
        out_shape=jax.ShapeDtypeStruct((b, s, d), F32),
        compiler_params=_cparams(("parallel", "parallel")),
        name="ffn",
    )(x, p, gf, w1, w2, gp, wpg, wple)


def _rotary_tables(s):
    half = ROT_DIM // 2
    inv_freq = jnp.power(jnp.float32(ROPE_THETA), -jnp.arange(half, dtype=F32) * 2.0 / ROT_DIM)
    ang = jnp.arange(s).astype(F32)[:, None] * inv_freq[None, :]
    cos, sin = jnp.cos(ang), jnp.sin(ang)
    pad = jnp.zeros((s, HEAD_DIM - ROT_DIM), F32)
    zero = jnp.zeros((s, half), F32)
    c = jnp.concatenate([cos, cos, pad + 1.0], axis=1)
    s1 = jnp.concatenate([-sin, zero, pad], axis=1)
    s2 = jnp.concatenate([zero, sin, pad], axis=1)
    rep = LANES // HEAD_DIM
    return jnp.tile(c, (1, rep)), jnp.tile(s1, (1, rep)), jnp.tile(s2, (1, rep))


def _swap_halves(w):
    return jnp.concatenate([w[:, HEAD_DIM:], w[:, :HEAD_DIM]], axis=1)


def kernel(x, p, norm_mix, w_in, shift_mu, q_norm, k_norm, sink, w0, w2, a0, a2, g2, k_k, k_a, r_k, lnx_w, lnx_b,
           w_up_attn, w_up_rwkv, w_out, norm_ffn, w_ff1, w_ff2, norm_ple, w_ple_gate, w_ple):
    bsz, s, d = x.shape
    depth = w_in.shape[0]
    c = RWKV_WIDTH
    tm = min(512, s)
    tt = min(512, s)
    cos, s1, s2 = _rotary_tables(s)
    lane = jnp.arange(LANES)
    bd = ((lane[:, None] // HEAD_DIM) == (lane[None, :] // HEAD_DIM)).astype(BF16)
    o_k, o_v, o_r, o_g = ATTN_WIDTH, ATTN_WIDTH + KV_WIDTH, ATTN_WIDTH + 2 * KV_WIDTH, ATTN_WIDTH + 2 * KV_WIDTH + RWKV_COLS
    for i in range(depth):
        wi = w_in[i]
        wk, wv = wi[:, o_k:o_v], wi[:, o_v:o_r]
        w_proj = jnp.concatenate([wi[:, :o_k], wk, _swap_halves(wk), wv, _swap_halves(wv), wi[:, o_r:o_g]],
                                 axis=1).astype(BF16)
        qkg = jnp.concatenate([jnp.tile(q_norm[i], Q_HEADS), jnp.tile(k_norm[i], 2 * KV_HEADS)])[None, :]
        zpad = jnp.zeros((DECAY_RANK, c), F32)
        wlo = jnp.concatenate([jnp.concatenate([w2[i, 0], zpad], axis=0), jnp.concatenate([w2[i, 1], zpad], axis=0),
                               jnp.concatenate([zpad, a2[i, 0]], axis=0), jnp.concatenate([zpad, a2[i, 1]], axis=0)],
                              axis=1).astype(BF16)
        (q, k2, v2, r, v, a, lw0, lw1, k0, k1, b0, b1, g, bonus) = _front(
            x, norm_mix[i][None, :], w_proj, cos, s1, s2, qkg, bd, shift_mu[i][None, :], wlo, g2[i].astype(BF16),
            w0[i], a0[i], k_k[i][None, :], k_a[i][None, :], r_k[i].reshape(2, c), tm)
        attn = _attn(sink[i], q, k2, v2)
        y0, y1 = _scan(r, v, a, lw0, lw1, k0, k1, b0, b1)

        x = _merge(x, norm_mix[i][None, :], wi[:, o_g:].astype(BF16), attn, y0, y1, bonus, g,
                   lnx_w[i][None, :], lnx_b[i][None, :], bd, w_up_attn[i].astype(BF16), w_up_rwkv[i].astype(BF16),
                   w_out[i].astype(BF16), tm)
        x = _ffn(x, p[i], norm_ffn[i][None, :], w_ff1[i].astype(BF16), w_ff2[i].astype(BF16), norm_ple[i][None, :],
                 w_ple_gate[i].astype(BF16), w_ple[i].astype(BF16), tm)
    return x
```

```python
import functools
import math

import jax
import jax.numpy as jnp
from jax import lax
from jax.experimental import pallas as pl
from jax.experimental.pallas import tpu as pltpu

F32 = jnp.float32
BF16 = jnp.bfloat16

LANES = 128
HEAD_DIM = 64
Q_HEADS = 8
KV_HEADS = 2
ATTN_WIDTH = Q_HEADS * HEAD_DIM
KV_WIDTH = KV_HEADS * HEAD_DIM
WINDOW = 128
ATTN_BLOCK = 128
ATTN_SUB = 2
ROPE_THETA = 500000.0
ROT_DIM = HEAD_DIM // 4
RWKV_WIDTH = 512
DECAY_RANK = 64
ICLR_RANK = 64
GATE_RANK = 128
RWKV_COLS = 3 * RWKV_WIDTH + DECAY_RANK + ICLR_RANK + GATE_RANK
NORM_EPS = 1e-6
GN_EPS = 64e-5
CHUNK = 64
GROUP_W = 128
SCAN_CHUNKS = 8
VMEM_LIMIT = 56 * 1024 * 1024


def _cparams(sem):
    return pltpu.CompilerParams(dimension_semantics=sem, vmem_limit_bytes=VMEM_LIMIT)


def _dot(a, b):
    return jnp.dot(a, b, preferred_element_type=F32)


def _dot_nt(a, b):
    return lax.dot_general(a, b, (((1,), (1,)), ((), ())), preferred_element_type=F32)


def _dot_tn(a, b):
    return lax.dot_general(a, b, (((0,), (0,)), ((), ())), preferred_element_type=F32)


def _split_dot(x, w_bf16):
    hi = x.astype(BF16)
    lo = (x - hi.astype(F32)).astype(BF16)
    return _dot(hi, w_bf16) + _dot(lo, w_bf16)


def _head_sum(x, bd, split=True):
    one = _split_dot if split else (lambda t, w: _dot(t.astype(BF16), w))
    cols = [one(x[:, i:i + LANES], bd) for i in range(0, x.shape[1], LANES)]
    return cols[0] if len(cols) == 1 else jnp.concatenate(cols, axis=1)


def _rms(x, gain):
    return x * lax.rsqrt(jnp.mean(x * x, axis=-1, keepdims=True) + NORM_EPS) * gain


def _sigmoid(x):
    return 0.5 * jnp.tanh(0.5 * x) + 0.5


QK_W = ATTN_WIDTH + 2 * KV_WIDTH
QKV_W = QK_W + 2 * KV_WIDTH
PROJ_W = QKV_W + RWKV_COLS
HALO = 8
FRONT_SUB = 128


def _qkv_rows(proj, cos, s1, s2, qkg, bd):
    qk = proj[:, :QK_W]
    ms = _head_sum(qk * qk, bd, split=False) * (1.0 / HEAD_DIM)
    qk = qk * lax.rsqrt(ms + NORM_EPS) * qkg
    tiles = []
    for i in range(0, QK_W, LANES):
        t = qk[:, i:i + LANES]
        t = t * cos + pltpu.roll(t, LANES - ROT_DIM // 2, 1) * s1 + pltpu.roll(t, ROT_DIM // 2, 1) * s2
        tiles.append(t)
    q = (jnp.concatenate(tiles[:4], axis=1) * (HEAD_DIM ** -0.5)).astype(BF16)
    return q, jnp.concatenate(tiles[4:], axis=1).astype(BF16), proj[:, QK_W:QKV_W].astype(BF16)


def _rwkv_rows(z, prev_row, next_row, mu, wlo, g2, w0, a0, k_k, k_a, r_k, bd):
    n = z.shape[0]
    row = lax.broadcasted_iota(jnp.int32, (HALO, 1), 0)
    ri = lax.broadcasted_iota(jnp.int32, (n, n), 0)
    ci = lax.broadcasted_iota(jnp.int32, (n, n), 1)
    adjacent = ((ri - ci == 1) | (ci - ri == 1)).astype(BF16)
    mix = 0.5 * mu
    shifted = z * (1.0 - mu) + _dot(adjacent, z.astype(BF16)) * mix
    top = shifted[:HALO] + jnp.where(row == 0, prev_row * mix, 0.0)
    bot = shifted[n - HALO:] + jnp.where(row == HALO - 1, next_row * mix, 0.0)
    z = jnp.concatenate([top, shifted[HALO:n - HALO], bot], axis=0)

    c = RWKV_WIDTH
    r, k, v = z[:, :c], z[:, c:2 * c], z[:, 2 * c:3 * c]
    lowrank = z[:, 3 * c:3 * c + LANES]
    left = lax.broadcasted_iota(jnp.int32, (1, LANES), 1) < DECAY_RANK
    lowrank = jnp.where(left, jnp.tanh(lowrank), lowrank).astype(BF16)
    wa = _dot(lowrank, wlo)
    g = _dot(_sigmoid(z[:, 3 * c + LANES:]).astype(BF16), g2)
    kk = k * k_k
    kk = kk * lax.rsqrt(jnp.maximum(_head_sum(kk * kk, bd, split=False), 1e-24))
    lws, ks, bs = [], [], []
    dot_rk = None
    for d in range(2):
        w_raw = w0[d:d + 1, :] + wa[:, d * c:(d + 1) * c]
        lws.append(-math.exp(-0.5) * _sigmoid(w_raw))
        rate = _sigmoid(a0[d:d + 1, :] + wa[:, (2 + d) * c:(3 + d) * c])
        k_dir = k * (1.0 + (rate - 1.0) * k_a)
        ks.append(k_dir.astype(BF16))
        bs.append((kk * rate).astype(BF16))
        term = k_dir * r_k[d:d + 1, :]
        dot_rk = term if dot_rk is None else dot_rk + term
    bonus = _head_sum(r * dot_rk, bd) * v
    return (r.astype(BF16), v.astype(BF16), (-kk).astype(BF16), lws[0], lws[1], ks[0], ks[1], bs[0], bs[1], g, bonus)


def _qk_rows(qk, cos, s1, s2, gains, bd):
    ms = _head_sum(qk * qk, bd, split=False) * (1.0 / HEAD_DIM)
    qk = qk * lax.rsqrt(ms + NORM_EPS) * gains
    tiles = []
    for i in range(0, qk.shape[1], LANES):
        t = qk[:, i:i + LANES]
        t = t * cos + pltpu.roll(t, LANES - ROT_DIM // 2, 1) * s1 + pltpu.roll(t, ROT_DIM // 2, 1) * s2
        tiles.append(t)
    return jnp.concatenate(tiles, axis=1)


def _front_kernel(x_ref, xp_ref, xn_ref, gain_ref, w_ref, cos_ref, s1_ref, s2_ref, qkg_ref, bd_ref, mu_ref, wlo_ref,
                  g2_ref, w0_ref, a0_ref, kk_ref, ka_ref, rk_ref, q_ref, k_ref, v_ref,
                  r_ref, vv_ref, a_ref, lw0_ref, lw1_ref, k0_ref, k1_ref, b0_ref, b1_ref, g_ref, bonus_ref):
    i = pl.program_id(1)
    nt = pl.num_programs(1)
    tm = x_ref.shape[1]
    sub = FRONT_SUB
    c = RWKV_WIDTH
    gain = gain_ref[...]
    bd = bd_ref[...]
    o_r, o_k, o_v, o_low = QKV_W, QKV_W + c, QKV_W + 2 * c, QKV_W + 3 * c
    halo = jnp.concatenate([xp_ref[0], xn_ref[0]], axis=0)
    h_blocks = [_rms(halo, gain).astype(BF16)]
    low_blocks = [_dot(h_blocks[0], w_ref[:, o_low:PROJ_W])]
    for s in range(0, tm, sub):
        h_blocks.insert(-1, _rms(x_ref[0, s:s + sub, :], gain).astype(BF16))
        low_blocks.insert(-1, _dot(h_blocks[-2], w_ref[:, o_low:PROJ_W]))
    h_ext = jnp.concatenate(h_blocks, axis=0)
    h = h_ext[:tm]
    z_low = jnp.concatenate(low_blocks, axis=0)

    row = lax.broadcasted_iota(jnp.int32, (HALO, 1), 0)
    ri = lax.broadcasted_iota(jnp.int32, (sub, sub), 0)
    ci = lax.broadcasted_iota(jnp.int32, (sub, sub), 1)
    adjacent = ((ri - ci == 1) | (ci - ri == 1)).astype(BF16)

    def project(lo, hi):
        return _dot(h_ext if lo >= QKV_W else h, w_ref[:, lo:hi])

    def shift(z, lo, hi):
        mu = mu_ref[:, lo:hi]
        mix = 0.5 * mu
        keep = 1.0 - mu
        prev_row = jnp.where(i > 0, z[tm + HALO - 1:tm + HALO], 0.0)
        next_row = jnp.where(i < nt - 1, z[tm + HALO:tm + HALO + 1], 0.0)
        blocks = []
        for s in range(0, tm, sub):
            zs = z[s:s + sub]
            part = zs * keep + _dot(adjacent, zs.astype(BF16)) * mix
            before = z[s - 1:s] if s > 0 else prev_row
            after = z[s + sub:s + sub + 1] if s + sub < tm else next_row
            blocks += [part[:HALO] + jnp.where(row == 0, before * mix, 0.0), part[HALO:sub - HALO],
                       part[sub - HALO:] + jnp.where(row == HALO - 1, after * mix, 0.0)]
        return jnp.concatenate(blocks, axis=0)

    half = c // 2
    cos, s1, s2 = cos_ref[...], s1_ref[...], s2_ref[...]

    def q_rows(p, j):
        q = _qk_rows(p, cos, s1, s2, qkg_ref[:, j * half:(j + 1) * half], bd)
        q_ref[0, :, j * half:(j + 1) * half] = (q * (HEAD_DIM ** -0.5)).astype(BF16)

    z_k0 = project(o_k, o_k + half)
    z_low = shift(z_low, 3 * c, RWKV_COLS)
    lowrank = z_low[:, :LANES]
    left = lax.broadcasted_iota(jnp.int32, (1, LANES), 1) < DECAY_RANK
    lowrank = jnp.where(left, jnp.tanh(lowrank), lowrank).astype(BF16)
    gate_in = _sigmoid(z_low[:, LANES:]).astype(BF16)

    z_k1 = project(o_k + half, o_k + c)
    wa = _dot(lowrank, wlo_ref[...])
    g_ref[0] = _dot(gate_in, g2_ref[...])
    p_q0 = project(0, half)

    lw0_ref[0] = -math.exp(-0.5) * _sigmoid(w0_ref[0:1, :] + wa[:, :c])
    p_q1 = project(half, c)
    lw1_ref[0] = -math.exp(-0.5) * _sigmoid(w0_ref[1:2, :] + wa[:, c:2 * c])
    z_r0 = project(o_r, o_r + half)

    k = shift(jnp.concatenate([z_k0, z_k1], axis=1), c, 2 * c)
    kk = k * kk_ref[...]
    kk = kk * lax.rsqrt(jnp.maximum(_head_sum(kk * kk, bd, split=False), 1e-24))
    a_ref[0] = (-kk).astype(BF16)
    z_r1 = project(o_r + half, o_r + c)

    later = [lambda: project(o_v, o_v + half), lambda: project(o_v + half, o_v + c)]
    z_v = []
    dot_rk = None
    for d, (kd_ref, bdir_ref) in enumerate(((k0_ref, b0_ref), (k1_ref, b1_ref))):
        rate = _sigmoid(a0_ref[d:d + 1, :] + wa[:, (2 + d) * c:(3 + d) * c])
        k_dir = k * (1.0 + (rate - 1.0) * ka_ref[...])
        kd_ref[0] = k_dir.astype(BF16)
        bdir_ref[0] = (kk * rate).astype(BF16)
        term = k_dir * rk_ref[d:d + 1, :]
        dot_rk = term if dot_rk is None else dot_rk + term
        z_v.append(later[d]())
        q_rows((p_q0, p_q1)[d], d)

    p_k2 = project(ATTN_WIDTH, ATTN_WIDTH + 2 * KV_WIDTH)
    r = shift(jnp.concatenate([z_r0, z_r1], axis=1), 0, c)
    r_ref[0] = r.astype(BF16)
    r_dot = _head_sum(r * dot_rk, bd, split=False)
    p_v2 = project(ATTN_WIDTH + 2 * KV_WIDTH, QKV_W)
    k_ref[0] = _qk_rows(p_k2, cos, s1, s2, qkg_ref[:, ATTN_WIDTH:], bd).astype(BF16)
    v = shift(jnp.concatenate(z_v, axis=1), 2 * c, 3 * c)
    vv_ref[0] = v.astype(BF16)
    bonus_ref[0] = r_dot * v
    v_ref[0] = p_v2.astype(BF16)


def _front(x, gain, w, cos, s1, s2, qkg, bd, mu, wlo, g2, w0, a0, k_k, k_a, r_k, tm):
    b, s, d = x.shape
    c = RWKV_WIDTH
    nh = tm // HALO
    const = lambda shape: pl.BlockSpec(shape, lambda bi, i: (0,) * len(shape))
    tok = lambda width: pl.BlockSpec((1, tm, width), lambda bi, i: (bi, i, 0))
    tab = pl.BlockSpec((tm, LANES), lambda bi, i: (i, 0))
    halo_p = pl.BlockSpec((1, HALO, d), lambda bi, i: (bi, jnp.maximum(i * nh - 1, 0), 0))
    halo_n = pl.BlockSpec((1, HALO, d), lambda bi, i: (bi, jnp.minimum((i + 1) * nh, s // HALO - 1), 0))
    rwkv_dtypes = (BF16,) * 3 + (F32,) * 2 + (BF16,) * 4 + (F32,) * 2
    return pl.pallas_call(
        _front_kernel,
        grid=(b, s // tm),
        in_specs=[tok(d), halo_p, halo_n, const((1, d)), const((d, PROJ_W)), tab, tab, tab, const((1, QK_W)),
                  const((LANES, LANES)), const((1, RWKV_COLS)), const((LANES, 4 * c)), const((GATE_RANK, c)),
                  const((2, c)), const((2, c)), const((1, c)), const((1, c)), const((2, c))],
        out_specs=[tok(ATTN_WIDTH), tok(2 * KV_WIDTH), tok(2 * KV_WIDTH)] + [tok(c)] * 11,
        out_shape=[jax.ShapeDtypeStruct((b, s, ATTN_WIDTH), BF16),
                   jax.ShapeDtypeStruct((b, s, 2 * KV_WIDTH), BF16),
                   jax.ShapeDtypeStruct((b, s, 2 * KV_WIDTH), BF16)]
                  + [jax.ShapeDtypeStruct((b, s, c), dt) for dt in rwkv_dtypes],
        compiler_params=_cparams(("parallel", "parallel")),
        name="front",
    )(x, x, x, gain, w, cos, s1, s2, qkg, bd, mu, wlo, g2, w0, a0, k_k, k_a, r_k)


def _attn_kernel(col_ranges, sink_ref, q_ref, kp_ref, kc_ref, kn_ref, vp_ref, vc_ref, vn_ref, *refs):
    nw = len(col_ranges)
    o_ref = refs[nw]
    for src_ref, dst_ref, (lo, hi) in zip(refs[:nw], refs[nw + 1:], col_ranges):
        dst_ref[...] = src_ref[:, :, lo:hi].astype(BF16)
    n = pl.program_id(1)
    nb = pl.num_programs(1)
    blk, nsub = ATTN_BLOCK, ATTN_SUB
    k = jnp.concatenate([kp_ref[0], kc_ref[0], kn_ref[0]], axis=0)
    v = jnp.concatenate([vp_ref[0], vc_ref[0], vn_ref[0]], axis=0)
    qi = lax.broadcasted_iota(jnp.int32, (2 * blk, blk), 0) % blk
    ki = lax.broadcasted_iota(jnp.int32, (2 * blk, blk), 1)
    first = lax.broadcasted_iota(jnp.int32, (2 * blk, 1), 0) < blk
    left = lax.broadcasted_iota(jnp.int32, (1, LANES), 1) < HEAD_DIM
    zero = jnp.zeros((), BF16)
    recs = []
    for j in range(nsub):
        rows = slice(j * blk, (j + 1) * blk)
        win = slice(j * blk, (j + 3) * blk)
        mask_prev = (ki >= qi) & ((n > 0) if j == 0 else True)
        mask_next = (ki <= qi) & ((n < nb - 1) if j == nsub - 1 else True)
        for g in range(KV_HEADS):
            pairs = jnp.concatenate([q_ref[0, rows, p * LANES:(p + 1) * LANES] for p in (2 * g, 2 * g + 1)], axis=0)
            for side in range(2):
                off = LANES * ((g + side) % 2)
                keep = left if side == 0 else jnp.logical_not(left)
                sk = jnp.where(first, sink_ref[4 * g + side], sink_ref[4 * g + 2 + side])
                recs.append(dict(q=jnp.where(keep, pairs, zero), k=k[win, off:off + LANES],
                                 v=jnp.where(keep, v[win, off:off + LANES], zero), sk=sk,
                                 mask_prev=mask_prev, mask_next=mask_next))
    for c in recs:
        sc = _dot_nt(c["q"], c["k"])
        c["sc"] = jnp.concatenate([jnp.where(c["mask_prev"], sc[:, :blk], -1e30), sc[:, blk:2 * blk],
                                   jnp.where(c["mask_next"], sc[:, 2 * blk:], -1e30)], axis=1)
    for c in recs:
        m = jnp.maximum(jnp.max(c["sc"], axis=-1, keepdims=True), c["sk"])
        e = jnp.exp(c["sc"] - m)
        c["den"] = jnp.sum(e, axis=-1, keepdims=True) + jnp.exp(c["sk"] - m)
        c["e"] = e.astype(BF16)
    for c in recs:
        c["o"] = _dot(c["e"], c["v"]) / c["den"]
    for j in range(nsub):
        outs = []
        for g in range(KV_HEADS):
            both = recs[4 * j + 2 * g]["o"] + recs[4 * j + 2 * g + 1]["o"]
            outs += [both[:blk], both[blk:]]
        o_ref[0, j * blk:(j + 1) * blk, :] = jnp.concatenate(outs, axis=1).astype(BF16)


def _attn(sink, q, k2, v2, weights):
    b, s, _ = q.shape
    nsub = ATTN_SUB
    nb = s // (ATTN_BLOCK * nsub)
    last = s // ATTN_BLOCK - 1
    edge = lambda fn: pl.BlockSpec((1, ATTN_BLOCK, 2 * KV_WIDTH), fn)
    prev = lambda bi, n: (bi, jnp.maximum(n * nsub - 1, 0), 0)
    cur = lambda bi, n: (bi, n, 0)
    nxt = lambda bi, n: (bi, jnp.minimum((n + 1) * nsub, last), 0)
    mid = pl.BlockSpec((1, ATTN_BLOCK * nsub, 2 * KV_WIDTH), cur)
    steps = b * nb
    slab = lambda rows, cols: pl.BlockSpec((1, rows, cols), lambda bi, n: (bi * nb + n, 0, 0))
    w_in, w_specs, w_out_specs, w_shapes, col_ranges = [], [], [], [], []
    for w, lo, hi in weights:
        rows, cols = w.shape
        assert rows % steps == 0, (rows, steps)
        w_in.append(w.reshape(steps, rows // steps, cols))
        w_specs.append(slab(rows // steps, cols))
        w_out_specs.append(slab(rows // steps, hi - lo))
        w_shapes.append(jax.ShapeDtypeStruct((steps, rows // steps, hi - lo), BF16))
        col_ranges.append((lo, hi))
    outs = pl.pallas_call(
        functools.partial(_attn_kernel, col_ranges),
        grid=(b, nb),
        in_specs=[pl.BlockSpec(memory_space=pltpu.SMEM),
                  pl.BlockSpec((1, ATTN_BLOCK * nsub, ATTN_WIDTH), cur),
                  edge(prev), mid, edge(nxt), edge(prev), mid, edge(nxt)] + w_specs,
        out_specs=[pl.BlockSpec((1, ATTN_BLOCK * nsub, ATTN_WIDTH), cur)] + w_out_specs,
        out_shape=[jax.ShapeDtypeStruct((b, s, ATTN_WIDTH), BF16)] + w_shapes,
        compiler_params=_cparams(("parallel", "parallel")),
        name="attn",
    )(sink, q, k2, k2, k2, v2, v2, v2, *w_in)
    return outs[0], [o.reshape(w.shape[0], hi - lo) for o, (w, lo, hi) in zip(outs[1:], weights)]


def _scan_kernel(r0_ref, v0_ref, a0_ref, lw0_ref, k0_ref, b0_ref,
                 r1_ref, v1_ref, a1_ref, lw1_ref, k1_ref, b1_ref,
                 y0_ref, y1_ref, ht_ref):
    L, gw, nch = CHUNK, GROUP_W, SCAN_CHUNKS
    rep = gw // L
    ng = RWKV_WIDTH // gw
    rows = nch * L

    @pl.when(pl.program_id(1) == 0)
    def _():
        ht_ref[...] = jnp.zeros_like(ht_ref)

    tau = lax.broadcasted_iota(jnp.int32, (L, gw), 0)
    sig = lax.broadcasted_iota(jnp.int32, (L, gw), 1) % L
    bd = (lax.broadcasted_iota(jnp.int32, (gw, gw), 0) // L) == (lax.broadcasted_iota(jnp.int32, (gw, gw), 1) // L)
    half = 2
    ngrp = nch // half
    hrows = half * L
    ti = lax.broadcasted_iota(jnp.int32, (hrows, hrows), 0)
    si = lax.broadcasted_iota(jnp.int32, (hrows, hrows), 1)
    same_chunk = (ti // L) == (si // L)
    tris = ((same_chunk & (si <= ti)).astype(BF16), (same_chunk & (si >= ti)).astype(BF16))
    masks = ((sig < tau, sig <= tau), (sig > tau, sig >= tau))

    def bdtile(x):
        xb = x.astype(BF16)
        return jnp.where(bd, jnp.concatenate([xb] * rep, axis=0), jnp.zeros((), BF16))

    dirs = ((r0_ref, v0_ref, a0_ref, lw0_ref, k0_ref, b0_ref),
            (r1_ref, v1_ref, a1_ref, lw1_ref, k1_ref, b1_ref))

    def build(grp):
        out = {}
        for d, (r_ref, v_ref, a_ref, lw_ref, k_ref, b_ref) in enumerate(dirs):
            base = (grp if d == 0 else ngrp - 1 - grp) * hrows
            rsl = slice(base, base + hrows)
            strict, incl = masks[d]
            lw = lw_ref[0, rsl, :]
            cl = _split_dot_left(tris[d], lw)
            e_in = jnp.exp(cl)
            e_inv = jnp.exp(-cl)
            e_ex = jnp.exp(cl - lw)
            a_t = a_ref[0, rsl, :].astype(F32) * e_ex
            r_t = r_ref[0, rsl, :].astype(F32) * e_in
            b_t = b_ref[0, rsl, :].astype(F32) * e_inv
            k_t = k_ref[0, rsl, :].astype(F32) * e_inv
            v_all = v_ref[0, rsl, :]
            for jj in range(half):
                rs = slice(jj * L, (jj + 1) * L)
                last = (jj + 1) * L - 1 if d == 0 else jj * L
                d_end = jnp.exp(cl[last:last + 1, :])
                b_h = b_t[rs] * d_end
                k_h = k_t[rs] * d_end
                for g in range(ng):
                    sl = slice(g * gw, (g + 1) * gw)
                    out[(base // L + jj, d, g)] = dict(
                        strict=strict, incl=incl, at=a_t[rs, sl], rt=r_t[rs, sl], vv=v_all[rs, sl], bt=b_t[rs, sl],
                        kt=k_t[rs, sl], bh=b_h[:, sl], kh=k_h[:, sl], dend=d_end[:, sl])
        return out

    def s_gram(par):
        for c in par:
            lhs = jnp.concatenate([c["at"], c["rt"]], axis=0).astype(BF16)
            gram = _dot_nt(lhs, jnp.concatenate([bdtile(c["bt"]), bdtile(c["kt"])], axis=0))
            c["gb"], c["gk"] = gram[:, :gw], gram[:, gw:]

    def s_vprod(par):
        for c in par:
            c["m_rb"] = jnp.where(c["incl"], c["gb"][L:], 0.0).astype(BF16)
            c["p"] = jnp.where(c["strict"], c["gb"][:L], 0.0)
            m_k = jnp.concatenate([jnp.where(c["strict"], c["gk"][:L], 0.0), jnp.where(c["incl"], c["gk"][L:], 0.0)],
                                  axis=0)
            res = _dot(m_k.astype(BF16), bdtile(c["vv"]))
            c["makv"], c["mrkv"] = res[:L], res[L:]

    levels = int(math.log2(L))

    def s_level(lvl):
        def run(par):
            for c in par:
                ops = ([] if lvl == 0 else [bdtile(c["t"])]) + ([] if lvl == levels - 1 else [bdtile(c["p"])])
                c["res"] = _dot(c["p"].astype(BF16), ops[0] if len(ops) == 1 else jnp.concatenate(ops, axis=1))
            for c in par:
                if lvl == 0:
                    c["t"] = jnp.where(sig == tau, 1.0, c["p"])
                    c["p"] = c["res"]
                else:
                    c["t"] = c["t"] + c["res"][:, :gw]
                    if lvl < levels - 1:
                        c["p"] = c["res"][:, gw:]
        return run

    def s_apply(par):
        for c in par:
            res = _dot(c["t"].astype(BF16), jnp.concatenate([bdtile(c["at"]), bdtile(c["makv"])], axis=1))
            c["ah"], c["uloc"] = res[:, :gw], res[:, gw:]

    def s_g(par):
        for c in par:
            c["bkh"] = jnp.concatenate([c["bh"], c["kh"]], axis=0).astype(BF16)
            c["g"] = jnp.where(bd, _dot_tn(c["ah"].astype(BF16), c["bkh"][:L]), 0.0).astype(BF16)

    def s_c(par):
        for c in par:
            c["cst"] = jnp.where(bd, _dot_tn(jnp.concatenate([c["uloc"].astype(BF16), c["vv"]], axis=0), c["bkh"]),
                                 0.0)

    def s_q(par):
        for c in par:
            res = _dot(c["m_rb"], jnp.concatenate([bdtile(c["ah"]), bdtile(c["uloc"])], axis=1))
            c["qh"] = (c["rt"] + res[:, :gw]).astype(BF16)
            c["yloc"] = res[:, gw:] + c["mrkv"]

    stages = [s_gram, s_vprod] + [s_level(lvl) for lvl in range(levels)] + [s_apply, s_g, s_c, s_q]

    state = {(d, g): ht_ref[d, g] for d in range(2) for g in range(ng)}
    recs = {}

    def seq_step(step):
        cur = [(d, g, recs[(step if d == 0 else nch - 1 - step, d, g)]) for d in range(2) for g in range(ng)]
        for d, g, c in cur:
            c["htb"] = state[(d, g)].astype(BF16)
        for d, g, c in cur:
            state[(d, g)] = state[(d, g)] * c["dend"] + c["cst"] + _dot(c["htb"], c["g"])
        for d, g, c in cur:
            c["y"] = c["yloc"] + _dot_nt(c["qh"], c["htb"])

    slots = {(i + 1) * len(stages) // (half + 1): i for i in range(half)}
    groups = [build(0)]
    recs.update(groups[0])
    for k in range(ngrp):
        for s, fn in enumerate(stages):
            if s == 2 and k + 1 < ngrp:
                groups.append(build(k + 1))
                recs.update(groups[k + 1])
            fn(list(groups[k].values()))
            if k > 0 and s in slots:
                seq_step((k - 1) * half + slots[s])
    for step in range(nch - half, nch):
        seq_step(step)
    for (d, g), h in state.items():
        ht_ref[d, g] = h
    for d, y_ref in enumerate((y0_ref, y1_ref)):
        y_ref[0] = jnp.concatenate(
            [jnp.concatenate([recs[(j, d, g)]["y"] for g in range(ng)], axis=1) for j in range(nch)], axis=0)


def _split_dot_left(w_bf16, x):
    hi = x.astype(BF16)
    lo = (x - hi.astype(F32)).astype(BF16)
    return _dot(w_bf16, hi) + _dot(w_bf16, lo)


def _scan(r, v, a, lw0, lw1, k0, k1, b0, b1):
    b, s, c = r.shape
    rows = CHUNK * SCAN_CHUNKS
    nblk = s // rows
    fwd = pl.BlockSpec((1, rows, c), lambda bi, i: (bi, i, 0))
    bwd = pl.BlockSpec((1, rows, c), lambda bi, i: (bi, nblk - 1 - i, 0))
    ng = c // GROUP_W
    return pl.pallas_call(
        _scan_kernel,
        grid=(b, nblk),
        in_specs=[fwd] * 6 + [bwd] * 6,
        out_specs=[fwd, bwd],
        out_shape=[jax.ShapeDtypeStruct((b, s, c), F32)] * 2,
        scratch_shapes=[pltpu.VMEM((2, ng, GROUP_W, GROUP_W), F32)],
        compiler_params=_cparams(("parallel", "arbitrary")),
        name="scan",
    )(r, v, a, lw0, k0, b0, r, v, a, lw1, k1, b1)


def _merge_kernel(x_ref, gain_ref, wg_ref, attn_ref, y0_ref, y1_ref, bonus_ref, g_ref, lnw_ref, lnb_ref,
                  bd_ref, wua_ref, wur_ref, wout_ref, o_ref):
    x = x_ref[0]
    d = x.shape[1]
    h = _rms(x, gain_ref[...]).astype(BF16)
    gates = _sigmoid(_dot(h, wg_ref[...]))
    bd = bd_ref[...]
    y = y0_ref[0] + y1_ref[0]
    yc = y - _head_sum(y, bd) * (1.0 / HEAD_DIM)
    var = _head_sum(yc * yc, bd, split=False) * (1.0 / HEAD_DIM)
    yn = yc * lax.rsqrt(var + GN_EPS) * lnw_ref[...] + lnb_ref[...]
    rw = (yn + bonus_ref[0]) * g_ref[0]
    merged = (gates[:, :d] * _dot(attn_ref[0], wua_ref[...])
              + gates[:, d:] * _dot(rw.astype(BF16), wur_ref[...]))
    o_ref[0] = x + _dot(merged.astype(BF16), wout_ref[...])


def _merge(x, gain, wg, attn, y0, y1, bonus, g, lnw, lnb, bd, wua, wur, wout, tm):
    b, s, d = x.shape
    c = RWKV_WIDTH
    const = lambda shape: pl.BlockSpec(shape, lambda bi, i: (0,) * len(shape))
    tok = lambda width: pl.BlockSpec((1, tm, width), lambda bi, i: (bi, i, 0))
    return pl.pallas_call(
        _merge_kernel,
        grid=(b, s // tm),
        in_specs=[tok(d), const((1, d)), const((d, 2 * d)), tok(ATTN_WIDTH), tok(c), tok(c), tok(c), tok(c),
                  const((1, c)), const((1, c)), const((LANES, LANES)), const((ATTN_WIDTH, d)), const((c, d)),
                  const((d, d))],
        out_specs=tok(d),
        out_shape=jax.ShapeDtypeStruct((b, s, d), F32),
        compiler_params=_cparams(("parallel", "parallel")),
        name="merge",
    )(x, gain, wg, attn, y0, y1, bonus, g, lnw, lnb, bd, wua, wur, wout)


FF_SLAB = 1024


def _ffn_kernel(x_ref, p_ref, gf_ref, w1_ref, w2_ref, gp_ref, wpg_ref, wple_ref, o_ref):
    x = x_ref[0]
    h = _rms(x, gf_ref[...]).astype(BF16)
    acc = x
    for j in range(0, w1_ref.shape[1], FF_SLAB):
        hid = jnp.maximum(_dot(h, w1_ref[:, j:j + FF_SLAB]), 0.0)
        acc = acc + _dot((hid * hid).astype(BF16), w2_ref[j:j + FF_SLAB, :])
    hp = _rms(acc, gp_ref[...]).astype(BF16)
    o_ref[0] = acc + _dot(p_ref[0].astype(BF16), wple_ref[...]) * _sigmoid(_dot(hp, wpg_ref[...]))


def _ffn(x, p, gf, w1, w2, gp, wpg, wple, tm):
    b, s, d = x.shape
    dff = w1.shape[1]
    pd = p.shape[-1]
    const = lambda shape: pl.BlockSpec(shape, lambda bi, i: (0,) * len(shape))
    tok = lambda width: pl.BlockSpec((1, tm, width), lambda bi, i: (bi, i, 0))
    return pl.pallas_call(
        _ffn_kernel,
        grid=(b, s // tm),
        in_specs=[tok(d), tok(pd), const((1, d)), const((d, dff)), const((dff, d)), const((1, d)),
                  const((d, d)), const((pd, d))],
        out_specs=tok(d),
        out_shape=jax.ShapeDtypeStruct((b, s, d), F32),
        compiler_params=_cparams(("parallel", "parallel")),
        name="ffn",
    )(x, p, gf, w1, w2, gp, wpg, wple)


def _rotary_tables(s):
    half = ROT_DIM // 2
    inv_freq = jnp.power(jnp.float32(ROPE_THETA), -jnp.arange(half, dtype=F32) * 2.0 / ROT_DIM)
    ang = jnp.arange(s).astype(F32)[:, None] * inv_freq[None, :]
    cos, sin = jnp.cos(ang), jnp.sin(ang)
    pad = jnp.zeros((s, HEAD_DIM - ROT_DIM), F32)
    zero = jnp.zeros((s, half), F32)
    c = jnp.concatenate([cos, cos, pad + 1.0], axis=1)
    s1 = jnp.concatenate([-sin, zero, pad], axis=1)
    s2 = jnp.concatenate([zero, sin, pad], axis=1)
    rep = LANES // HEAD_DIM
    return jnp.tile(c, (1, rep)), jnp.tile(s1, (1, rep)), jnp.tile(s2, (1, rep))


def _swap_halves(w):
    return jnp.concatenate([w[:, HEAD_DIM:], w[:, :HEAD_DIM]], axis=1)


def kernel(x, p, norm_mix, w_in, shift_mu, q_norm, k_norm, sink, w0, w2, a0, a2, g2, k_k, k_a, r_k, lnx_w, lnx_b,
           w_up_attn, w_up_rwkv, w_out, norm_ffn, w_ff1, w_ff2, norm_ple, w_ple_gate, w_ple):
    bsz, s, d = x.shape
    depth = w_in.shape[0]
    c = RWKV_WIDTH
    tm = min(512, s)
    tt = min(512, s)
    cos, s1, s2 = _rotary_tables(s)
    lane = jnp.arange(LANES)
    bd = ((lane[:, None] // HEAD_DIM) == (lane[None, :] // HEAD_DIM)).astype(BF16)
    o_k, o_v, o_r, o_g = ATTN_WIDTH, ATTN_WIDTH + KV_WIDTH, ATTN_WIDTH + 2 * KV_WIDTH, ATTN_WIDTH + 2 * KV_WIDTH + RWKV_COLS
    for i in range(depth):
        wi = w_in[i]
        wk, wv = wi[:, o_k:o_v], wi[:, o_v:o_r]
        w_proj = jnp.concatenate([wi[:, :o_k], wk, _swap_halves(wk), wv, _swap_halves(wv), wi[:, o_r:o_g]],
                                 axis=1).astype(BF16)
        qkg = jnp.concatenate([jnp.tile(q_norm[i], Q_HEADS), jnp.tile(k_norm[i], 2 * KV_HEADS)])[None, :]
        zpad = jnp.zeros((DECAY_RANK, c), F32)
        wlo = jnp.concatenate([jnp.concatenate([w2[i, 0], zpad], axis=0), jnp.concatenate([w2[i, 1], zpad], axis=0),
                               jnp.concatenate([zpad, a2[i, 0]], axis=0), jnp.concatenate([zpad, a2[i, 1]], axis=0)],
                              axis=1).astype(BF16)
        (q, k2, v2, r, v, a, lw0, lw1, k0, k1, b0, b1, g, bonus) = _front(
            x, norm_mix[i][None, :], w_proj, cos, s1, s2, qkg, bd, shift_mu[i][None, :], wlo, g2[i].astype(BF16),
            w0[i], a0[i], k_k[i][None, :], k_a[i][None, :], r_k[i].reshape(2, c), tm)
        full = lambda w: (w, 0, w.shape[1])
        attn, (wg, wua, wur, wo, wf1, wf2, wpg, wpl) = _attn(
            sink[i], q, k2, v2,
            [(wi, o_g, wi.shape[1]), full(w_up_attn[i]), full(w_up_rwkv[i]), full(w_out[i]), full(w_ff1[i]),
             full(w_ff2[i]), full(w_ple_gate[i]), full(w_ple[i])])
        y0, y1 = _scan(r, v, a, lw0, lw1, k0, k1, b0, b1)

        x = _merge(x, norm_mix[i][None, :], wg, attn, y0, y1, bonus, g, lnx_w[i][None, :], lnx_b[i][None, :], bd,
                   wua, wur, wo, tm)
        x = _ffn(x, p[i], norm_ffn[i][None, :], wf1, wf2, norm_ple[i][None, :], wpg, wpl, tm)
    return x
```

```python
import functools
import math

import jax
import jax.numpy as jnp
from jax import lax
from jax.experimental import pallas as pl
from jax.experimental.pallas import tpu as pltpu

F32 = jnp.float32
BF16 = jnp.bfloat16

LANES = 128
HEAD_DIM = 64
Q_HEADS = 8
KV_HEADS = 2
ATTN_WIDTH = Q_HEADS * HEAD_DIM
KV_WIDTH = KV_HEADS * HEAD_DIM
WINDOW = 128
ATTN_BLOCK = 128
ROPE_THETA = 500000.0
ROT_DIM = HEAD_DIM // 4
RWKV_WIDTH = 512
DECAY_RANK = 64
ICLR_RANK = 64
GATE_RANK = 128
RWKV_COLS = 3 * RWKV_WIDTH + DECAY_RANK + ICLR_RANK + GATE_RANK
NORM_EPS = 1e-6
GN_EPS = 64e-5
CHUNK = 64
GROUP_W = 128
SCAN_CHUNKS = 8
VMEM_LIMIT = 56 * 1024 * 1024


def _cparams(sem):
    return pltpu.CompilerParams(dimension_semantics=sem, vmem_limit_bytes=VMEM_LIMIT)


def _dot(a, b):
    return jnp.dot(a, b, preferred_element_type=F32)


def _dot_nt(a, b):
    return lax.dot_general(a, b, (((1,), (1,)), ((), ())), preferred_element_type=F32)


def _dot_tn(a, b):
    return lax.dot_general(a, b, (((0,), (0,)), ((), ())), preferred_element_type=F32)


def _split_dot(x, w_bf16):
    hi = x.astype(BF16)
    lo = (x - hi.astype(F32)).astype(BF16)
    return _dot(hi, w_bf16) + _dot(lo, w_bf16)


def _head_sum(x, bd, split=True):
    one = _split_dot if split else (lambda t, w: _dot(t.astype(BF16), w))
    cols = [one(x[:, i:i + LANES], bd) for i in range(0, x.shape[1], LANES)]
    return cols[0] if len(cols) == 1 else jnp.concatenate(cols, axis=1)


def _rms(x, gain):
    return x * lax.rsqrt(jnp.mean(x * x, axis=-1, keepdims=True) + NORM_EPS) * gain


def _sigmoid(x):
    return 0.5 * jnp.tanh(0.5 * x) + 0.5


QK_W = ATTN_WIDTH + 2 * KV_WIDTH
QKV_W = QK_W + 2 * KV_WIDTH
PROJ_W = QKV_W + RWKV_COLS
HALO = 8
FRONT_SUB = 128


def _qk_rows(qk, cos, s1, s2, gains, bd):
    ms = _head_sum(qk * qk, bd, split=False) * (1.0 / HEAD_DIM)
    qk = qk * lax.rsqrt(ms + NORM_EPS) * gains
    tiles = []
    for i in range(0, qk.shape[1], LANES):
        t = qk[:, i:i + LANES]
        t = t * cos + pltpu.roll(t, LANES - ROT_DIM // 2, 1) * s1 + pltpu.roll(t, ROT_DIM // 2, 1) * s2
        tiles.append(t)
    return jnp.concatenate(tiles, axis=1)


def _slab_plan(weights, steps, step_index):
    ins, in_specs, out_specs, out_shapes, cols = [], [], [], [], []
    for w, lo, hi in weights:
        rows, width = w.shape
        assert rows % steps == 0, (rows, steps)
        n = rows // steps
        ins.append(w.reshape(steps, n, width))
        in_specs.append(pl.BlockSpec((1, n, width), lambda *g: (step_index(*g), 0, 0)))
        out_specs.append(pl.BlockSpec((1, n, hi - lo), lambda *g: (step_index(*g), 0, 0)))
        out_shapes.append(jax.ShapeDtypeStruct((steps, n, hi - lo), BF16))
        cols.append((lo, hi))
    return ins, in_specs, out_specs, out_shapes, cols


N_FRONT_IN, N_FRONT_OUT = 18, 14


def _front_kernel(cols, *refs):
    nw = len(cols)
    (x_ref, xp_ref, xn_ref, gain_ref, w_ref, cos_ref, s1_ref, s2_ref, qkg_ref, bd_ref, mu_ref, wlo_ref,
     g2_ref, w0_ref, a0_ref, kk_ref, ka_ref, rk_ref) = refs[:N_FRONT_IN]
    (q_ref, k_ref, v_ref, r_ref, vv_ref, a_ref, lw0_ref, lw1_ref, k0_ref, k1_ref, b0_ref, b1_ref, g_ref,
     bonus_ref) = refs[N_FRONT_IN + nw:N_FRONT_IN + nw + N_FRONT_OUT]
    for src_ref, dst_ref, (lo, hi) in zip(refs[N_FRONT_IN:], refs[N_FRONT_IN + nw + N_FRONT_OUT:], cols):
        dst_ref[...] = src_ref[:, :, lo:hi].astype(BF16)
    i = pl.program_id(1)
    nt = pl.num_programs(1)
    tm = x_ref.shape[1]
    sub = FRONT_SUB
    c = RWKV_WIDTH
    gain = gain_ref[...]
    bd = bd_ref[...]
    o_r, o_k, o_v, o_low = QKV_W, QKV_W + c, QKV_W + 2 * c, QKV_W + 3 * c
    halo = jnp.concatenate([xp_ref[0], xn_ref[0]], axis=0)
    h_blocks = [_rms(halo, gain).astype(BF16)]
    low_blocks = [_dot(h_blocks[0], w_ref[:, o_low:PROJ_W])]
    for s in range(0, tm, sub):
        h_blocks.insert(-1, _rms(x_ref[0, s:s + sub, :], gain).astype(BF16))
        low_blocks.insert(-1, _dot(h_blocks[-2], w_ref[:, o_low:PROJ_W]))
    h_ext = jnp.concatenate(h_blocks, axis=0)
    h = h_ext[:tm]
    z_low = jnp.concatenate(low_blocks, axis=0)

    row = lax.broadcasted_iota(jnp.int32, (HALO, 1), 0)
    ri = lax.broadcasted_iota(jnp.int32, (sub, sub), 0)
    ci = lax.broadcasted_iota(jnp.int32, (sub, sub), 1)
    adjacent = ((ri - ci == 1) | (ci - ri == 1)).astype(BF16)

    def project(lo, hi):
        return _dot(h_ext if lo >= QKV_W else h, w_ref[:, lo:hi])

    def shift(z, lo, hi):
        mu = mu_ref[:, lo:hi]
        mix = 0.5 * mu
        keep = 1.0 - mu
        prev_row = jnp.where(i > 0, z[tm + HALO - 1:tm + HALO], 0.0)
        next_row = jnp.where(i < nt - 1, z[tm + HALO:tm + HALO + 1], 0.0)
        blocks = []
        for s in range(0, tm, sub):
            zs = z[s:s + sub]
            part = zs * keep + _dot(adjacent, zs.astype(BF16)) * mix
            before = z[s - 1:s] if s > 0 else prev_row
            after = z[s + sub:s + sub + 1] if s + sub < tm else next_row
            blocks += [part[:HALO] + jnp.where(row == 0, before * mix, 0.0), part[HALO:sub - HALO],
                       part[sub - HALO:] + jnp.where(row == HALO - 1, after * mix, 0.0)]
        return jnp.concatenate(blocks, axis=0)

    half = c // 2
    cos, s1, s2 = cos_ref[...], s1_ref[...], s2_ref[...]

    def q_rows(p, j):
        q = _qk_rows(p, cos, s1, s2, qkg_ref[:, j * half:(j + 1) * half], bd)
        q_ref[0, :, j * half:(j + 1) * half] = (q * (HEAD_DIM ** -0.5)).astype(BF16)

    z_k0 = project(o_k, o_k + half)
    z_low = shift(z_low, 3 * c, RWKV_COLS)
    lowrank = z_low[:, :LANES]
    left = lax.broadcasted_iota(jnp.int32, (1, LANES), 1) < DECAY_RANK
    lowrank = jnp.where(left, jnp.tanh(lowrank), lowrank).astype(BF16)
    gate_in = _sigmoid(z_low[:, LANES:]).astype(BF16)

    z_k1 = project(o_k + half, o_k + c)
    wa = _dot(lowrank, wlo_ref[...])
    g_ref[0] = _dot(gate_in, g2_ref[...])
    p_q0 = project(0, half)

    lw0_ref[0] = -math.exp(-0.5) * _sigmoid(w0_ref[0:1, :] + wa[:, :c])
    p_q1 = project(half, c)
    lw1_ref[0] = -math.exp(-0.5) * _sigmoid(w0_ref[1:2, :] + wa[:, c:2 * c])
    z_r0 = project(o_r, o_r + half)

    k = shift(jnp.concatenate([z_k0, z_k1], axis=1), c, 2 * c)
    kk = k * kk_ref[...]
    kk = kk * lax.rsqrt(jnp.maximum(_head_sum(kk * kk, bd, split=False), 1e-24))
    a_ref[0] = (-kk).astype(BF16)
    z_r1 = project(o_r + half, o_r + c)

    later = [lambda: project(o_v, o_v + half), lambda: project(o_v + half, o_v + c)]
    z_v = []
    dot_rk = None
    for d, (kd_ref, bdir_ref) in enumerate(((k0_ref, b0_ref), (k1_ref, b1_ref))):
        rate = _sigmoid(a0_ref[d:d + 1, :] + wa[:, (2 + d) * c:(3 + d) * c])
        k_dir = k * (1.0 + (rate - 1.0) * ka_ref[...])
        kd_ref[0] = k_dir.astype(BF16)
        bdir_ref[0] = (kk * rate).astype(BF16)
        term = k_dir * rk_ref[d:d + 1, :]
        dot_rk = term if dot_rk is None else dot_rk + term
        z_v.append(later[d]())
        q_rows((p_q0, p_q1)[d], d)

    p_k2 = project(ATTN_WIDTH, ATTN_WIDTH + 2 * KV_WIDTH)
    r = shift(jnp.concatenate([z_r0, z_r1], axis=1), 0, c)
    r_ref[0] = r.astype(BF16)
    r_dot = _head_sum(r * dot_rk, bd, split=False)
    p_v2 = project(ATTN_WIDTH + 2 * KV_WIDTH, QKV_W)
    k_ref[0] = _qk_rows(p_k2, cos, s1, s2, qkg_ref[:, ATTN_WIDTH:], bd).astype(BF16)
    v = shift(jnp.concatenate(z_v, axis=1), 2 * c, 3 * c)
    vv_ref[0] = v.astype(BF16)
    bonus_ref[0] = r_dot * v
    v_ref[0] = p_v2.astype(BF16)


def _front(x, gain, w, cos, s1, s2, qkg, bd, mu, wlo, g2, w0, a0, k_k, k_a, r_k, tm, weights):
    b, s, d = x.shape
    nt = s // tm
    w_ins, w_in_specs, w_out_specs, w_out_shapes, cols = _slab_plan(weights, b * nt, lambda bi, i: bi * nt + i)
    c = RWKV_WIDTH
    nh = tm // HALO
    const = lambda shape: pl.BlockSpec(shape, lambda bi, i: (0,) * len(shape))
    tok = lambda width: pl.BlockSpec((1, tm, width), lambda bi, i: (bi, i, 0))
    tab = pl.BlockSpec((tm, LANES), lambda bi, i: (i, 0))
    halo_p = pl.BlockSpec((1, HALO, d), lambda bi, i: (bi, jnp.maximum(i * nh - 1, 0), 0))
    halo_n = pl.BlockSpec((1, HALO, d), lambda bi, i: (bi, jnp.minimum((i + 1) * nh, s // HALO - 1), 0))
    rwkv_dtypes = (BF16,) * 3 + (F32,) * 2 + (BF16,) * 4 + (F32,) * 2
    outs = pl.pallas_call(
        functools.partial(_front_kernel, cols),
        grid=(b, nt),
        in_specs=[tok(d), halo_p, halo_n, const((1, d)), const((d, PROJ_W)), tab, tab, tab, const((1, QK_W)),
                  const((LANES, LANES)), const((1, RWKV_COLS)), const((LANES, 4 * c)), const((GATE_RANK, c)),
                  const((2, c)), const((2, c)), const((1, c)), const((1, c)), const((2, c))] + w_in_specs,
        out_specs=[tok(ATTN_WIDTH), tok(2 * KV_WIDTH), tok(2 * KV_WIDTH)] + [tok(c)] * 11 + w_out_specs,
        out_shape=[jax.ShapeDtypeStruct((b, s, ATTN_WIDTH), BF16),
                   jax.ShapeDtypeStruct((b, s, 2 * KV_WIDTH), BF16),
                   jax.ShapeDtypeStruct((b, s, 2 * KV_WIDTH), BF16)]
                  + [jax.ShapeDtypeStruct((b, s, c), dt) for dt in rwkv_dtypes] + w_out_shapes,
        compiler_params=_cparams(("parallel", "parallel")),
        name="front",
    )(x, x, x, gain, w, cos, s1, s2, qkg, bd, mu, wlo, g2, w0, a0, k_k, k_a, r_k, *w_ins)
    converted = [o.reshape(wt.shape[0], hi - lo) for o, (wt, lo, hi) in zip(outs[N_FRONT_OUT:], weights)]
    return outs[:N_FRONT_OUT], converted


def _scan_kernel(r0_ref, v0_ref, a0_ref, lw0_ref, k0_ref, b0_ref,
                 r1_ref, v1_ref, a1_ref, lw1_ref, k1_ref, b1_ref,
                 y0_ref, y1_ref, ht_ref):
    L, gw, nch = CHUNK, GROUP_W, SCAN_CHUNKS
    rep = gw // L
    ng = RWKV_WIDTH // gw
    rows = nch * L

    @pl.when(pl.program_id(1) == 0)
    def _():
        ht_ref[...] = jnp.zeros_like(ht_ref)

    tau = lax.broadcasted_iota(jnp.int32, (L, gw), 0)
    sig = lax.broadcasted_iota(jnp.int32, (L, gw), 1) % L
    bd = (lax.broadcasted_iota(jnp.int32, (gw, gw), 0) // L) == (lax.broadcasted_iota(jnp.int32, (gw, gw), 1) // L)
    half = 2
    ngrp = nch // half
    hrows = half * L
    ti = lax.broadcasted_iota(jnp.int32, (hrows, hrows), 0)
    si = lax.broadcasted_iota(jnp.int32, (hrows, hrows), 1)
    same_chunk = (ti // L) == (si // L)
    tris = ((same_chunk & (si <= ti)).astype(BF16), (same_chunk & (si >= ti)).astype(BF16))
    masks = ((sig < tau, sig <= tau), (sig > tau, sig >= tau))

    def bdtile(x):
        xb = x.astype(BF16)
        return jnp.where(bd, jnp.concatenate([xb] * rep, axis=0), jnp.zeros((), BF16))

    dirs = ((r0_ref, v0_ref, a0_ref, lw0_ref, k0_ref, b0_ref),
            (r1_ref, v1_ref, a1_ref, lw1_ref, k1_ref, b1_ref))

    def build(grp):
        out = {}
        for d, (r_ref, v_ref, a_ref, lw_ref, k_ref, b_ref) in enumerate(dirs):
            base = (grp if d == 0 else ngrp - 1 - grp) * hrows
            rsl = slice(base, base + hrows)
            strict, incl = masks[d]
            lw = lw_ref[0, rsl, :]
            cl = _split_dot_left(tris[d], lw)
            e_in = jnp.exp(cl)
            e_inv = jnp.exp(-cl)
            e_ex = jnp.exp(cl - lw)
            a_t = a_ref[0, rsl, :].astype(F32) * e_ex
            r_t = r_ref[0, rsl, :].astype(F32) * e_in
            b_t = b_ref[0, rsl, :].astype(F32) * e_inv
            k_t = k_ref[0, rsl, :].astype(F32) * e_inv
            v_all = v_ref[0, rsl, :]
            for jj in range(half):
                rs = slice(jj * L, (jj + 1) * L)
                last = (jj + 1) * L - 1 if d == 0 else jj * L
                d_end = jnp.exp(cl[last:last + 1, :])
                b_h = b_t[rs] * d_end
                k_h = k_t[rs] * d_end
                for g in range(ng):
                    sl = slice(g * gw, (g + 1) * gw)
                    out[(base // L + jj, d, g)] = dict(
                        strict=strict, incl=incl, at=a_t[rs, sl], rt=r_t[rs, sl], vv=v_all[rs, sl], bt=b_t[rs, sl],
                        kt=k_t[rs, sl], bh=b_h[:, sl], kh=k_h[:, sl], dend=d_end[:, sl])
        return out

    def s_gram(par):
        for c in par:
            lhs = jnp.concatenate([c["at"], c["rt"]], axis=0).astype(BF16)
            gram = _dot_nt(lhs, jnp.concatenate([bdtile(c["bt"]), bdtile(c["kt"])], axis=0))
            c["gb"], c["gk"] = gram[:, :gw], gram[:, gw:]

    def s_vprod(par):
        for c in par:
            c["m_rb"] = jnp.where(c["incl"], c["gb"][L:], 0.0).astype(BF16)
            c["p"] = jnp.where(c["strict"], c["gb"][:L], 0.0)
            m_k = jnp.concatenate([jnp.where(c["strict"], c["gk"][:L], 0.0), jnp.where(c["incl"], c["gk"][L:], 0.0)],
                                  axis=0)
            res = _dot(m_k.astype(BF16), bdtile(c["vv"]))
            c["makv"], c["mrkv"] = res[:L], res[L:]

    levels = int(math.log2(L))

    def s_level(lvl):
        def run(par):
            for c in par:
                ops = ([] if lvl == 0 else [bdtile(c["t"])]) + ([] if lvl == levels - 1 else [bdtile(c["p"])])
                c["res"] = _dot(c["p"].astype(BF16), ops[0] if len(ops) == 1 else jnp.concatenate(ops, axis=1))
            for c in par:
                if lvl == 0:
                    c["t"] = jnp.where(sig == tau, 1.0, c["p"])
                    c["p"] = c["res"]
                else:
                    c["t"] = c["t"] + c["res"][:, :gw]
                    if lvl < levels - 1:
                        c["p"] = c["res"][:, gw:]
        return run

    def s_apply(par):
        for c in par:
            res = _dot(c["t"].astype(BF16), jnp.concatenate([bdtile(c["at"]), bdtile(c["makv"])], axis=1))
            c["ah"], c["uloc"] = res[:, :gw], res[:, gw:]

    def s_g(par):
        for c in par:
            c["bkh"] = jnp.concatenate([c["bh"], c["kh"]], axis=0).astype(BF16)
            c["g"] = jnp.where(bd, _dot_tn(c["ah"].astype(BF16), c["bkh"][:L]), 0.0).astype(BF16)

    def s_c(par):
        for c in par:
            c["cst"] = jnp.where(bd, _dot_tn(jnp.concatenate([c["uloc"].astype(BF16), c["vv"]], axis=0), c["bkh"]),
                                 0.0)

    def s_q(par):
        for c in par:
            res = _dot(c["m_rb"], jnp.concatenate([bdtile(c["ah"]), bdtile(c["uloc"])], axis=1))
            c["qh"] = (c["rt"] + res[:, :gw]).astype(BF16)
            c["yloc"] = res[:, gw:] + c["mrkv"]

    stages = [s_gram, s_vprod] + [s_level(lvl) for lvl in range(levels)] + [s_apply, s_g, s_c, s_q]

    state = {(d, g): ht_ref[d, g] for d in range(2) for g in range(ng)}
    recs = {}

    def seq_step(step):
        cur = [(d, g, recs[(step if d == 0 else nch - 1 - step, d, g)]) for d in range(2) for g in range(ng)]
        for d, g, c in cur:
            c["htb"] = state[(d, g)].astype(BF16)
        for d, g, c in cur:
            state[(d, g)] = state[(d, g)] * c["dend"] + c["cst"] + _dot(c["htb"], c["g"])
        for d, g, c in cur:
            c["y"] = c["yloc"] + _dot_nt(c["qh"], c["htb"])

    slots = {(i + 1) * len(stages) // (half + 1): i for i in range(half)}
    groups = [build(0)]
    recs.update(groups[0])
    for k in range(ngrp):
        for s, fn in enumerate(stages):
            if s == 2 and k + 1 < ngrp:
                groups.append(build(k + 1))
                recs.update(groups[k + 1])
            fn(list(groups[k].values()))
            if k > 0 and s in slots:
                seq_step((k - 1) * half + slots[s])
    for step in range(nch - half, nch):
        seq_step(step)
    for (d, g), h in state.items():
        ht_ref[d, g] = h
    for d, y_ref in enumerate((y0_ref, y1_ref)):
        y_ref[0] = jnp.concatenate(
            [jnp.concatenate([recs[(j, d, g)]["y"] for g in range(ng)], axis=1) for j in range(nch)], axis=0)


def _split_dot_left(w_bf16, x):
    hi = x.astype(BF16)
    lo = (x - hi.astype(F32)).astype(BF16)
    return _dot(w_bf16, hi) + _dot(w_bf16, lo)


def _scan(r, v, a, lw0, lw1, k0, k1, b0, b1):
    b, s, c = r.shape
    rows = CHUNK * SCAN_CHUNKS
    nblk = s // rows
    fwd = pl.BlockSpec((1, rows, c), lambda bi, i: (bi, i, 0))
    bwd = pl.BlockSpec((1, rows, c), lambda bi, i: (bi, nblk - 1 - i, 0))
    ng = c // GROUP_W
    return pl.pallas_call(
        _scan_kernel,
        grid=(b, nblk),
        in_specs=[fwd] * 6 + [bwd] * 6,
        out_specs=[fwd, bwd],
        out_shape=[jax.ShapeDtypeStruct((b, s, c), F32)] * 2,
        scratch_shapes=[pltpu.VMEM((2, ng, GROUP_W, GROUP_W), F32)],
        compiler_params=_cparams(("parallel", "arbitrary")),
        name="scan",
    )(r, v, a, lw0, k0, b0, r, v, a, lw1, k1, b1)


MID_BLOCKS = 4


def _mid_kernel(sink_ref, q_ref, kp_ref, kc_ref, kn_ref, vp_ref, vc_ref, vn_ref, x_ref, gain_ref, wg_ref, y0_ref,
                y1_ref, bonus_ref, g_ref, lnw_ref, lnb_ref, bd_ref, wua_ref, wur_ref, wout_ref, o_ref):
    n = pl.program_id(1)
    nb = pl.num_programs(1)
    blk, nsub = ATTN_BLOCK, MID_BLOCKS
    d = x_ref.shape[2]
    k = jnp.concatenate([kp_ref[0], kc_ref[0], kn_ref[0]], axis=0)
    v = jnp.concatenate([vp_ref[0], vc_ref[0], vn_ref[0]], axis=0)
    qi = lax.broadcasted_iota(jnp.int32, (2 * blk, blk), 0) % blk
    ki = lax.broadcasted_iota(jnp.int32, (2 * blk, blk), 1)
    first = lax.broadcasted_iota(jnp.int32, (2 * blk, 1), 0) < blk
    left = lax.broadcasted_iota(jnp.int32, (1, LANES), 1) < HEAD_DIM
    zero = jnp.zeros((), BF16)
    bd = bd_ref[...]

    def scores(j):
        rows = slice(j * blk, (j + 1) * blk)
        win = slice(j * blk, (j + 3) * blk)
        mask_prev = (ki >= qi) & ((n > 0) if j == 0 else True)
        mask_next = (ki <= qi) & ((n < nb - 1) if j == nsub - 1 else True)
        recs = []
        for g in range(KV_HEADS):
            pairs = jnp.concatenate([q_ref[0, rows, p * LANES:(p + 1) * LANES] for p in (2 * g, 2 * g + 1)], axis=0)
            for side in range(2):
                off = LANES * ((g + side) % 2)
                keep = left if side == 0 else jnp.logical_not(left)
                sk = jnp.where(first, sink_ref[4 * g + side], sink_ref[4 * g + 2 + side])
                sc = _dot_nt(jnp.where(keep, pairs, zero), k[win, off:off + LANES])
                sc = jnp.concatenate([jnp.where(mask_prev, sc[:, :blk], -1e30), sc[:, blk:2 * blk],
                                      jnp.where(mask_next, sc[:, 2 * blk:], -1e30)], axis=1)
                recs.append(dict(sc=sc, sk=sk, v=jnp.where(keep, v[win, off:off + LANES], zero)))
        return recs

    def softmax(recs):
        for c in recs:
            m = jnp.maximum(jnp.max(c["sc"], axis=-1, keepdims=True), c["sk"])
            e = jnp.exp(c["sc"] - m)
            c["den"] = jnp.sum(e, axis=-1, keepdims=True) + jnp.exp(c["sk"] - m)
            c["e"] = e.astype(BF16)

    def weighted(recs):
        outs = []
        for g in range(KV_HEADS):
            both = sum(_dot(c["e"], c["v"]) / c["den"] for c in recs[2 * g:2 * g + 2])
            outs += [both[:blk], both[blk:]]
        return jnp.concatenate(outs, axis=1).astype(BF16)

    quarter = wg_ref.shape[1] // 4
    x = x_ref[0]
    blocks = [scores(0), scores(1)]
    h = _rms(x, gain_ref[...]).astype(BF16)
    gate_pre = [_dot(h, wg_ref[:, :quarter])]
    softmax(blocks[0])
    gate_pre.append(_dot(h, wg_ref[:, quarter:2 * quarter]))
    softmax(blocks[1])
    attn = [weighted(blocks[0])]
    blocks.append(scores(2))
    gates = [_sigmoid(gate_pre[0])]
    gate_pre.append(_dot(h, wg_ref[:, 2 * quarter:3 * quarter]))
    attn.append(weighted(blocks[1]))
    blocks.append(scores(3))
    softmax(blocks[2])
    gate_pre.append(_dot(h, wg_ref[:, 3 * quarter:]))
    gates.append(_sigmoid(gate_pre[1]))
    softmax(blocks[3])
    attn.append(weighted(blocks[2]))

    y = y0_ref[0] + y1_ref[0]
    yc = y - _head_sum(y, bd) * (1.0 / HEAD_DIM)
    var = _head_sum(yc * yc, bd, split=False) * (1.0 / HEAD_DIM)
    gates.append(_sigmoid(gate_pre[2]))
    attn.append(weighted(blocks[3]))
    yn = yc * lax.rsqrt(var + GN_EPS) * lnw_ref[...] + lnb_ref[...]
    rw = ((yn + bonus_ref[0]) * g_ref[0]).astype(BF16)
    up_attn = _dot(jnp.concatenate(attn, axis=0), wua_ref[...])
    gates.append(_sigmoid(gate_pre[3]))
    up_rwkv = _dot(rw, wur_ref[...])
    half = d // 2
    merged = (jnp.concatenate(gates[:2], axis=1) * up_attn + jnp.concatenate(gates[2:], axis=1) * up_rwkv).astype(BF16)
    for col in range(0, d, half):
        o_ref[0, :, col:col + half] = x[:, col:col + half] + _dot(merged, wout_ref[:, col:col + half])


def _mid(sink, q, k2, v2, x, gain, wg, y0, y1, bonus, g, lnw, lnb, bd, wua, wur, wout):
    b, s, d = x.shape
    c = RWKV_WIDTH
    tm = ATTN_BLOCK * MID_BLOCKS
    nb = s // tm
    last = s // ATTN_BLOCK - 1
    const = lambda shape: pl.BlockSpec(shape, lambda bi, n: (0,) * len(shape))
    tok = lambda width: pl.BlockSpec((1, tm, width), lambda bi, n: (bi, n, 0))
    prev = pl.BlockSpec((1, ATTN_BLOCK, 2 * KV_WIDTH), lambda bi, n: (bi, jnp.maximum(n * MID_BLOCKS - 1, 0), 0))
    nxt = pl.BlockSpec((1, ATTN_BLOCK, 2 * KV_WIDTH), lambda bi, n: (bi, jnp.minimum((n + 1) * MID_BLOCKS, last), 0))
    kv = [prev, tok(2 * KV_WIDTH), nxt]
    return pl.pallas_call(
        _mid_kernel,
        grid=(b, nb),
        in_specs=[pl.BlockSpec(memory_space=pltpu.SMEM), tok(ATTN_WIDTH)] + kv + kv
                 + [tok(d), const((1, d)), const((d, 2 * d)), tok(c), tok(c), tok(c), tok(c), const((1, c)),
                    const((1, c)), const((LANES, LANES)), const((ATTN_WIDTH, d)), const((c, d)), const((d, d))],
        out_specs=tok(d),
        out_shape=jax.ShapeDtypeStruct((b, s, d), F32),
        compiler_params=_cparams(("parallel", "parallel")),
        name="mid",
    )(sink, q, k2, k2, k2, v2, v2, v2, x, gain, wg, y0, y1, bonus, g, lnw, lnb, bd, wua, wur, wout)


FF_SLAB = 1024


def _ffn_kernel(x_ref, p_ref, gf_ref, w1_ref, w2_ref, gp_ref, wpg_ref, wple_ref, o_ref):
    x = x_ref[0]
    h = _rms(x, gf_ref[...]).astype(BF16)
    acc = x
    for j in range(0, w1_ref.shape[1], FF_SLAB):
        hid = jnp.maximum(_dot(h, w1_ref[:, j:j + FF_SLAB]), 0.0)
        acc = acc + _dot((hid * hid).astype(BF16), w2_ref[j:j + FF_SLAB, :])
    hp = _rms(acc, gp_ref[...]).astype(BF16)
    o_ref[0] = acc + _dot(p_ref[0].astype(BF16), wple_ref[...]) * _sigmoid(_dot(hp, wpg_ref[...]))


def _ffn(x, p, gf, w1, w2, gp, wpg, wple, tm):
    b, s, d = x.shape
    dff = w1.shape[1]
    pd = p.shape[-1]
    const = lambda shape: pl.BlockSpec(shape, lambda bi, i: (0,) * len(shape))
    tok = lambda width: pl.BlockSpec((1, tm, width), lambda bi, i: (bi, i, 0))
    return pl.pallas_call(
        _ffn_kernel,
        grid=(b, s // tm),
        in_specs=[tok(d), tok(pd), const((1, d)), const((d, dff)), const((dff, d)), const((1, d)),
                  const((d, d)), const((pd, d))],
        out_specs=tok(d),
        out_shape=jax.ShapeDtypeStruct((b, s, d), F32),
        compiler_params=_cparams(("parallel", "parallel")),
        name="ffn",
    )(x, p, gf, w1, w2, gp, wpg, wple)


def _rotary_tables(s):
    half = ROT_DIM // 2
    inv_freq = jnp.power(jnp.float32(ROPE_THETA), -jnp.arange(half, dtype=F32) * 2.0 / ROT_DIM)
    ang = jnp.arange(s).astype(F32)[:, None] * inv_freq[None, :]
    cos, sin = jnp.cos(ang), jnp.sin(ang)
    pad = jnp.zeros((s, HEAD_DIM - ROT_DIM), F32)
    zero = jnp.zeros((s, half), F32)
    c = jnp.concatenate([cos, cos, pad + 1.0], axis=1)
    s1 = jnp.concatenate([-sin, zero, pad], axis=1)
    s2 = jnp.concatenate([zero, sin, pad], axis=1)
    rep = LANES // HEAD_DIM
    return jnp.tile(c, (1, rep)), jnp.tile(s1, (1, rep)), jnp.tile(s2, (1, rep))


def _swap_halves(w):
    return jnp.concatenate([w[:, HEAD_DIM:], w[:, :HEAD_DIM]], axis=1)


def kernel(x, p, norm_mix, w_in, shift_mu, q_norm, k_norm, sink, w0, w2, a0, a2, g2, k_k, k_a, r_k, lnx_w, lnx_b,
           w_up_attn, w_up_rwkv, w_out, norm_ffn, w_ff1, w_ff2, norm_ple, w_ple_gate, w_ple):
    bsz, s, d = x.shape
    depth = w_in.shape[0]
    c = RWKV_WIDTH
    tm = min(512, s)
    tt = min(512, s)
    cos, s1, s2 = _rotary_tables(s)
    lane = jnp.arange(LANES)
    bd = ((lane[:, None] // HEAD_DIM) == (lane[None, :] // HEAD_DIM)).astype(BF16)
    o_k, o_v, o_r, o_g = ATTN_WIDTH, ATTN_WIDTH + KV_WIDTH, ATTN_WIDTH + 2 * KV_WIDTH, ATTN_WIDTH + 2 * KV_WIDTH + RWKV_COLS
    for i in range(depth):
        wi = w_in[i]
        wk, wv = wi[:, o_k:o_v], wi[:, o_v:o_r]
        w_proj = jnp.concatenate([wi[:, :o_k], wk, _swap_halves(wk), wv, _swap_halves(wv), wi[:, o_r:o_g]],
                                 axis=1).astype(BF16)
        qkg = jnp.concatenate([jnp.tile(q_norm[i], Q_HEADS), jnp.tile(k_norm[i], 2 * KV_HEADS)])[None, :]
        zpad = jnp.zeros((DECAY_RANK, c), F32)
        wlo = jnp.concatenate([jnp.concatenate([w2[i, 0], zpad], axis=0), jnp.concatenate([w2[i, 1], zpad], axis=0),
                               jnp.concatenate([zpad, a2[i, 0]], axis=0), jnp.concatenate([zpad, a2[i, 1]], axis=0)],
                              axis=1).astype(BF16)
        full = lambda w: (w, 0, w.shape[1])
        ((q, k2, v2, r, v, a, lw0, lw1, k0, k1, b0, b1, g, bonus), (wg, wua, wur, wo, wf1, wf2, wpg, wpl)) = _front(
            x, norm_mix[i][None, :], w_proj, cos, s1, s2, qkg, bd, shift_mu[i][None, :], wlo, g2[i].astype(BF16),
            w0[i], a0[i], k_k[i][None, :], k_a[i][None, :], r_k[i].reshape(2, c), tm,
            [(wi, o_g, wi.shape[1]), full(w_up_attn[i]), full(w_up_rwkv[i]), full(w_out[i]), full(w_ff1[i]),
             full(w_ff2[i]), full(w_ple_gate[i]), full(w_ple[i])])
        y0, y1 = _scan(r, v, a, lw0, lw1, k0, k1, b0, b1)
        x = _mid(sink[i], q, k2, v2, x, norm_mix[i][None, :], wg, y0, y1, bonus, g, lnx_w[i][None, :],
                 lnx_b[i][None, :], bd, wua, wur, wo)
        x = _ffn(x, p[i], norm_ffn[i][None, :], wf1, wf2, norm_ple[i][None, :], wpg, wpl, tm)
    return x
```

```python
import functools
import math

import jax
import jax.numpy as jnp
from jax import lax
from jax.experimental import pallas as pl
from jax.experimental.pallas import tpu as pltpu

F32 = jnp.float32
BF16 = jnp.bfloat16

LANES = 128
HEAD_DIM = 64
Q_HEADS = 8
KV_HEADS = 2
ATTN_WIDTH = Q_HEADS * HEAD_DIM
KV_WIDTH = KV_HEADS * HEAD_DIM
WINDOW = 128
ATTN_BLOCK = 128
ROPE_THETA = 500000.0
ROT_DIM = HEAD_DIM // 4
RWKV_WIDTH = 512
DECAY_RANK = 64
ICLR_RANK = 64
GATE_RANK = 128
RWKV_COLS = 3 * RWKV_WIDTH + DECAY_RANK + ICLR_RANK + GATE_RANK
NORM_EPS = 1e-6
GN_EPS = 64e-5
CHUNK = 64
GROUP_W = 128
SCAN_CHUNKS = 8
VMEM_LIMIT = 56 * 1024 * 1024


def _cparams(sem):
    return pltpu.CompilerParams(dimension_semantics=sem, vmem_limit_bytes=VMEM_LIMIT)


def _dot(a, b):
    return jnp.dot(a, b, preferred_element_type=F32)


def _dot_nt(a, b):
    return lax.dot_general(a, b, (((1,), (1,)), ((), ())), preferred_element_type=F32)


def _dot_tn(a, b):
    return lax.dot_general(a, b, (((0,), (0,)), ((), ())), preferred_element_type=F32)


def _split_dot(x, w_bf16):
    hi = x.astype(BF16)
    lo = (x - hi.astype(F32)).astype(BF16)
    return _dot(hi, w_bf16) + _dot(lo, w_bf16)


def _head_sum(x, bd, split=True):
    one = _split_dot if split else (lambda t, w: _dot(t.astype(BF16), w))
    cols = [one(x[:, i:i + LANES], bd) for i in range(0, x.shape[1], LANES)]
    return cols[0] if len(cols) == 1 else jnp.concatenate(cols, axis=1)


def _rms(x, gain):
    return x * lax.rsqrt(jnp.mean(x * x, axis=-1, keepdims=True) + NORM_EPS) * gain


def _sigmoid(x):
    return 0.5 * jnp.tanh(0.5 * x) + 0.5


QK_W = ATTN_WIDTH + KV_WIDTH
QKV_W = QK_W + KV_WIDTH
PROJ_W = QKV_W + RWKV_COLS
HALO = 8
FRONT_SUB = 128


def _qk_rows(qk, cos, s1, s2, gains, bd):
    ms = _head_sum(qk * qk, bd, split=False) * (1.0 / HEAD_DIM)
    qk = qk * lax.rsqrt(ms + NORM_EPS) * gains
    tiles = []
    for i in range(0, qk.shape[1], LANES):
        t = qk[:, i:i + LANES]
        t = t * cos + pltpu.roll(t, LANES - ROT_DIM // 2, 1) * s1 + pltpu.roll(t, ROT_DIM // 2, 1) * s2
        tiles.append(t)
    return jnp.concatenate(tiles, axis=1)


def _slab_plan(weights, steps, step_index):
    ins, in_specs, out_specs, out_shapes, cols = [], [], [], [], []
    for w, lo, hi in weights:
        rows, width = w.shape
        assert rows % steps == 0, (rows, steps)
        n = rows // steps
        ins.append(w.reshape(steps, n, width))
        in_specs.append(pl.BlockSpec((1, n, width), lambda *g: (step_index(*g), 0, 0)))
        out_specs.append(pl.BlockSpec((1, n, hi - lo), lambda *g: (step_index(*g), 0, 0)))
        out_shapes.append(jax.ShapeDtypeStruct((steps, n, hi - lo), BF16))
        cols.append((lo, hi))
    return ins, in_specs, out_specs, out_shapes, cols


N_FRONT_IN, N_FRONT_OUT = 18, 14


def _front_kernel(cols, *refs):
    nw = len(cols)
    (x_ref, xp_ref, xn_ref, gain_ref, w_ref, cos_ref, s1_ref, s2_ref, qkg_ref, bd_ref, mu_ref, wlo_ref,
     g2_ref, w0_ref, a0_ref, kk_ref, ka_ref, rk_ref) = refs[:N_FRONT_IN]
    (q_ref, k_ref, v_ref, r_ref, vv_ref, a_ref, lw0_ref, lw1_ref, k0_ref, k1_ref, b0_ref, b1_ref, g_ref,
     bonus_ref) = refs[N_FRONT_IN + nw:N_FRONT_IN + nw + N_FRONT_OUT]
    for src_ref, dst_ref, (lo, hi) in zip(refs[N_FRONT_IN:], refs[N_FRONT_IN + nw + N_FRONT_OUT:], cols):
        dst_ref[...] = src_ref[:, :, lo:hi].astype(BF16)
    i = pl.program_id(1)
    nt = pl.num_programs(1)
    tm = x_ref.shape[1]
    sub = FRONT_SUB
    c = RWKV_WIDTH
    gain = gain_ref[...]
    bd = bd_ref[...]
    o_r, o_k, o_v, o_low = QKV_W, QKV_W + c, QKV_W + 2 * c, QKV_W + 3 * c
    halo = jnp.concatenate([xp_ref[0], xn_ref[0]], axis=0)
    w_low = w_ref[:, o_low:PROJ_W].astype(BF16)
    h_blocks = [_rms(halo, gain).astype(BF16)]
    low_blocks = [_dot(h_blocks[0], w_low)]
    for s in range(0, tm, sub):
        h_blocks.insert(-1, _rms(x_ref[0, s:s + sub, :], gain).astype(BF16))
        low_blocks.insert(-1, _dot(h_blocks[-2], w_low))
    h_ext = jnp.concatenate(h_blocks, axis=0)
    h = h_ext[:tm]
    z_low = jnp.concatenate(low_blocks, axis=0)

    row = lax.broadcasted_iota(jnp.int32, (HALO, 1), 0)
    ri = lax.broadcasted_iota(jnp.int32, (sub, sub), 0)
    ci = lax.broadcasted_iota(jnp.int32, (sub, sub), 1)
    adjacent = ((ri - ci == 1) | (ci - ri == 1)).astype(BF16)

    def project(lo, hi):
        return _dot(h_ext if lo >= QKV_W else h, w_ref[:, lo:hi].astype(BF16))

    def shift(z, lo, hi):
        mu = mu_ref[:, lo:hi]
        mix = 0.5 * mu
        keep = 1.0 - mu
        prev_row = jnp.where(i > 0, z[tm + HALO - 1:tm + HALO], 0.0)
        next_row = jnp.where(i < nt - 1, z[tm + HALO:tm + HALO + 1], 0.0)
        blocks = []
        for s in range(0, tm, sub):
            zs = z[s:s + sub]
            part = zs * keep + _dot(adjacent, zs.astype(BF16)) * mix
            before = z[s - 1:s] if s > 0 else prev_row
            after = z[s + sub:s + sub + 1] if s + sub < tm else next_row
            blocks += [part[:HALO] + jnp.where(row == 0, before * mix, 0.0), part[HALO:sub - HALO],
                       part[sub - HALO:] + jnp.where(row == HALO - 1, after * mix, 0.0)]
        return jnp.concatenate(blocks, axis=0)

    half = c // 2
    cos, s1, s2 = cos_ref[...], s1_ref[...], s2_ref[...]

    def q_rows(p, j):
        q = _qk_rows(p, cos, s1, s2, qkg_ref[:, j * half:(j + 1) * half], bd)
        q_ref[0, :, j * half:(j + 1) * half] = (q * (HEAD_DIM ** -0.5)).astype(BF16)

    z_k0 = project(o_k, o_k + half)
    z_low = shift(z_low, 3 * c, RWKV_COLS)
    lowrank = z_low[:, :LANES]
    left = lax.broadcasted_iota(jnp.int32, (1, LANES), 1) < DECAY_RANK
    lowrank = jnp.where(left, jnp.tanh(lowrank), lowrank).astype(BF16)
    gate_in = _sigmoid(z_low[:, LANES:]).astype(BF16)

    z_k1 = project(o_k + half, o_k + c)
    wa = _dot(lowrank, wlo_ref[...])
    g_ref[0] = _dot(gate_in, g2_ref[...])
    p_q0 = project(0, half)

    lw0_ref[0] = -math.exp(-0.5) * _sigmoid(w0_ref[0:1, :] + wa[:, :c])
    p_q1 = project(half, c)
    lw1_ref[0] = -math.exp(-0.5) * _sigmoid(w0_ref[1:2, :] + wa[:, c:2 * c])
    z_r0 = project(o_r, o_r + half)

    k = shift(jnp.concatenate([z_k0, z_k1], axis=1), c, 2 * c)
    kk = k * kk_ref[...]
    kk = kk * lax.rsqrt(jnp.maximum(_head_sum(kk * kk, bd, split=False), 1e-24))
    a_ref[0] = (-kk).astype(BF16)
    z_r1 = project(o_r + half, o_r + c)

    later = [lambda: project(o_v, o_v + half), lambda: project(o_v + half, o_v + c)]
    z_v = []
    dot_rk = None
    for d, (kd_ref, bdir_ref) in enumerate(((k0_ref, b0_ref), (k1_ref, b1_ref))):
        rate = _sigmoid(a0_ref[d:d + 1, :] + wa[:, (2 + d) * c:(3 + d) * c])
        k_dir = k * (1.0 + (rate - 1.0) * ka_ref[...])
        kd_ref[0] = k_dir.astype(BF16)
        bdir_ref[0] = (kk * rate).astype(BF16)
        term = k_dir * rk_ref[d:d + 1, :]
        dot_rk = term if dot_rk is None else dot_rk + term
        z_v.append(later[d]())
        q_rows((p_q0, p_q1)[d], d)

    p_kv = project(ATTN_WIDTH, QKV_W)
    r = shift(jnp.concatenate([z_r0, z_r1], axis=1), 0, c)
    r_ref[0] = r.astype(BF16)
    r_dot = _head_sum(r * dot_rk, bd, split=False)
    k_att = _qk_rows(p_kv[:, :KV_WIDTH], cos, s1, s2, qkg_ref[:, ATTN_WIDTH:], bd)
    k_ref[0] = jnp.concatenate([k_att, pltpu.roll(k_att, HEAD_DIM, 1)], axis=1).astype(BF16)
    v = shift(jnp.concatenate(z_v, axis=1), 2 * c, 3 * c)
    vv_ref[0] = v.astype(BF16)
    bonus_ref[0] = r_dot * v
    v_att = p_kv[:, KV_WIDTH:]
    v_ref[0] = jnp.concatenate([v_att, pltpu.roll(v_att, HEAD_DIM, 1)], axis=1).astype(BF16)


def _front(x, gain, w, cos, s1, s2, qkg, bd, mu, wlo, g2, w0, a0, k_k, k_a, r_k, tm, weights):
    b, s, d = x.shape
    nt = s // tm
    w_ins, w_in_specs, w_out_specs, w_out_shapes, cols = _slab_plan(weights, b * nt, lambda bi, i: bi * nt + i)
    c = RWKV_WIDTH
    nh = tm // HALO
    const = lambda shape: pl.BlockSpec(shape, lambda bi, i: (0,) * len(shape))
    tok = lambda width: pl.BlockSpec((1, tm, width), lambda bi, i: (bi, i, 0))
    tab = pl.BlockSpec((tm, LANES), lambda bi, i: (i, 0))
    halo_p = pl.BlockSpec((1, HALO, d), lambda bi, i: (bi, jnp.maximum(i * nh - 1, 0), 0))
    halo_n = pl.BlockSpec((1, HALO, d), lambda bi, i: (bi, jnp.minimum((i + 1) * nh, s // HALO - 1), 0))
    rwkv_dtypes = (BF16,) * 3 + (F32,) * 2 + (BF16,) * 4 + (F32,) * 2
    outs = pl.pallas_call(
        functools.partial(_front_kernel, cols),
        grid=(b, nt),
        in_specs=[tok(d), halo_p, halo_n, const((1, d)), const((d, PROJ_W)), tab, tab, tab, const((1, QK_W)),
                  const((LANES, LANES)), const((1, RWKV_COLS)), const((LANES, 4 * c)), const((GATE_RANK, c)),
                  const((2, c)), const((2, c)), const((1, c)), const((1, c)), const((2, c))] + w_in_specs,
        out_specs=[tok(ATTN_WIDTH), tok(2 * KV_WIDTH), tok(2 * KV_WIDTH)] + [tok(c)] * 11 + w_out_specs,
        out_shape=[jax.ShapeDtypeStruct((b, s, ATTN_WIDTH), BF16),
                   jax.ShapeDtypeStruct((b, s, 2 * KV_WIDTH), BF16),
                   jax.ShapeDtypeStruct((b, s, 2 * KV_WIDTH), BF16)]
                  + [jax.ShapeDtypeStruct((b, s, c), dt) for dt in rwkv_dtypes] + w_out_shapes,
        compiler_params=_cparams(("parallel", "parallel")),
        name="front",
    )(x, x, x, gain, w, cos, s1, s2, qkg, bd, mu, wlo, g2, w0, a0, k_k, k_a, r_k, *w_ins)
    converted = [o.reshape(wt.shape[0], hi - lo) for o, (wt, lo, hi) in zip(outs[N_FRONT_OUT:], weights)]
    return outs[:N_FRONT_OUT], converted


def _scan_kernel(r0_ref, v0_ref, a0_ref, lw0_ref, k0_ref, b0_ref,
                 r1_ref, v1_ref, a1_ref, lw1_ref, k1_ref, b1_ref,
                 y0_ref, y1_ref, ht_ref):
    L, gw, nch = CHUNK, GROUP_W, SCAN_CHUNKS
    rep = gw // L
    ng = RWKV_WIDTH // gw
    rows = nch * L

    @pl.when(pl.program_id(1) == 0)
    def _():
        ht_ref[...] = jnp.zeros_like(ht_ref)

    tau = lax.broadcasted_iota(jnp.int32, (L, gw), 0)
    sig = lax.broadcasted_iota(jnp.int32, (L, gw), 1) % L
    bd = (lax.broadcasted_iota(jnp.int32, (gw, gw), 0) // L) == (lax.broadcasted_iota(jnp.int32, (gw, gw), 1) // L)
    half = 2
    ngrp = nch // half
    hrows = half * L
    ti = lax.broadcasted_iota(jnp.int32, (hrows, hrows), 0)
    si = lax.broadcasted_iota(jnp.int32, (hrows, hrows), 1)
    same_chunk = (ti // L) == (si // L)
    tris = ((same_chunk & (si <= ti)).astype(BF16), (same_chunk & (si >= ti)).astype(BF16))
    masks = ((sig < tau, sig <= tau), (sig > tau, sig >= tau))

    def bdtile(x):
        xb = x.astype(BF16)
        return jnp.where(bd, jnp.concatenate([xb] * rep, axis=0), jnp.zeros((), BF16))

    dirs = ((r0_ref, v0_ref, a0_ref, lw0_ref, k0_ref, b0_ref),
            (r1_ref, v1_ref, a1_ref, lw1_ref, k1_ref, b1_ref))

    def build(grp):
        out = {}
        for d, (r_ref, v_ref, a_ref, lw_ref, k_ref, b_ref) in enumerate(dirs):
            base = (grp if d == 0 else ngrp - 1 - grp) * hrows
            rsl = slice(base, base + hrows)
            strict, incl = masks[d]
            lw = lw_ref[0, rsl, :]
            cl = _split_dot_left(tris[d], lw)
            e_in = jnp.exp(cl)
            e_inv = jnp.exp(-cl)
            e_ex = jnp.exp(cl - lw)
            a_t = a_ref[0, rsl, :].astype(F32) * e_ex
            r_t = r_ref[0, rsl, :].astype(F32) * e_in
            b_t = b_ref[0, rsl, :].astype(F32) * e_inv
            k_t = k_ref[0, rsl, :].astype(F32) * e_inv
            v_all = v_ref[0, rsl, :]
            for jj in range(half):
                rs = slice(jj * L, (jj + 1) * L)
                last = (jj + 1) * L - 1 if d == 0 else jj * L
                d_end = jnp.exp(cl[last:last + 1, :])
                b_h = b_t[rs] * d_end
                k_h = k_t[rs] * d_end
                for g in range(ng):
                    sl = slice(g * gw, (g + 1) * gw)
                    out[(base // L + jj, d, g)] = dict(
                        strict=strict, incl=incl, at=a_t[rs, sl], rt=r_t[rs, sl], vv=v_all[rs, sl], bt=b_t[rs, sl],
                        kt=k_t[rs, sl], bh=b_h[:, sl], kh=k_h[:, sl], dend=d_end[:, sl])
        return out

    def s_gram(par):
        for c in par:
            lhs = jnp.concatenate([c["at"], c["rt"]], axis=0).astype(BF16)
            gram = _dot_nt(lhs, jnp.concatenate([bdtile(c["bt"]), bdtile(c["kt"])], axis=0))
            c["gb"], c["gk"] = gram[:, :gw], gram[:, gw:]

    def s_vprod(par):
        for c in par:
            c["m_rb"] = jnp.where(c["incl"], c["gb"][L:], 0.0).astype(BF16)
            c["p"] = jnp.where(c["strict"], c["gb"][:L], 0.0)
            m_k = jnp.concatenate([jnp.where(c["strict"], c["gk"][:L], 0.0), jnp.where(c["incl"], c["gk"][L:], 0.0)],
                                  axis=0)
            res = _dot(m_k.astype(BF16), bdtile(c["vv"]))
            c["makv"], c["mrkv"] = res[:L], res[L:]

    levels = int(math.log2(L))

    def s_level(lvl):
        def run(par):
            for c in par:
                ops = ([] if lvl == 0 else [bdtile(c["t"])]) + ([] if lvl == levels - 1 else [bdtile(c["p"])])
                c["res"] = _dot(c["p"].astype(BF16), ops[0] if len(ops) == 1 else jnp.concatenate(ops, axis=1))
            for c in par:
                if lvl == 0:
                    c["t"] = jnp.where(sig == tau, 1.0, c["p"])
                    c["p"] = c["res"]
                else:
                    c["t"] = c["t"] + c["res"][:, :gw]
                    if lvl < levels - 1:
                        c["p"] = c["res"][:, gw:]
        return run

    def s_apply(par):
        for c in par:
            res = _dot(c["t"].astype(BF16), jnp.concatenate([bdtile(c["at"]), bdtile(c["makv"])], axis=1))
            c["ah"], c["uloc"] = res[:, :gw], res[:, gw:]

    def s_g(par):
        for c in par:
            c["bkh"] = jnp.concatenate([c["bh"], c["kh"]], axis=0).astype(BF16)
            c["g"] = jnp.where(bd, _dot_tn(c["ah"].astype(BF16), c["bkh"][:L]), 0.0).astype(BF16)

    def s_c(par):
        for c in par:
            c["cst"] = jnp.where(bd, _dot_tn(jnp.concatenate([c["uloc"].astype(BF16), c["vv"]], axis=0), c["bkh"]),
                                 0.0)

    def s_q(par):
        for c in par:
            res = _dot(c["m_rb"], jnp.concatenate([bdtile(c["ah"]), bdtile(c["uloc"])], axis=1))
            c["qh"] = (c["rt"] + res[:, :gw]).astype(BF16)
            c["yloc"] = res[:, gw:] + c["mrkv"]

    stages = [s_gram, s_vprod] + [s_level(lvl) for lvl in range(levels)] + [s_apply, s_g, s_c, s_q]

    state = {(d, g): ht_ref[d, g] for d in range(2) for g in range(ng)}
    recs = {}

    def seq_step(step):
        cur = [(d, g, recs[(step if d == 0 else nch - 1 - step, d, g)]) for d in range(2) for g in range(ng)]
        for d, g, c in cur:
            c["htb"] = state[(d, g)].astype(BF16)
        for d, g, c in cur:
            state[(d, g)] = state[(d, g)] * c["dend"] + c["cst"] + _dot(c["htb"], c["g"])
        for d, g, c in cur:
            c["y"] = c["yloc"] + _dot_nt(c["qh"], c["htb"])

    slots = {(i + 1) * len(stages) // (half + 1): i for i in range(half)}
    groups = [build(0)]
    recs.update(groups[0])
    for k in range(ngrp):
        for s, fn in enumerate(stages):
            if s == 2 and k + 1 < ngrp:
                groups.append(build(k + 1))
                recs.update(groups[k + 1])
            fn(list(groups[k].values()))
            if k > 0 and s in slots:
                seq_step((k - 1) * half + slots[s])
    for step in range(nch - half, nch):
        seq_step(step)
    for (d, g), h in state.items():
        ht_ref[d, g] = h
    for d, y_ref in enumerate((y0_ref, y1_ref)):
        y_ref[0] = jnp.concatenate(
            [jnp.concatenate([recs[(j, d, g)]["y"] for g in range(ng)], axis=1) for j in range(nch)], axis=0)


def _split_dot_left(w_bf16, x):
    hi = x.astype(BF16)
    lo = (x - hi.astype(F32)).astype(BF16)
    return _dot(w_bf16, hi) + _dot(w_bf16, lo)


def _scan(r, v, a, lw0, lw1, k0, k1, b0, b1):
    b, s, c = r.shape
    rows = CHUNK * SCAN_CHUNKS
    nblk = s // rows
    fwd = pl.BlockSpec((1, rows, c), lambda bi, i: (bi, i, 0))
    bwd = pl.BlockSpec((1, rows, c), lambda bi, i: (bi, nblk - 1 - i, 0))
    ng = c // GROUP_W
    return pl.pallas_call(
        _scan_kernel,
        grid=(b, nblk),
        in_specs=[fwd] * 6 + [bwd] * 6,
        out_specs=[fwd, bwd],
        out_shape=[jax.ShapeDtypeStruct((b, s, c), F32)] * 2,
        scratch_shapes=[pltpu.VMEM((2, ng, GROUP_W, GROUP_W), F32)],
        compiler_params=_cparams(("parallel", "arbitrary")),
        name="scan",
    )(r, v, a, lw0, k0, b0, r, v, a, lw1, k1, b1)


MID_BLOCKS = 4


def _mid_kernel(sink_ref, q_ref, kp_ref, kc_ref, kn_ref, vp_ref, vc_ref, vn_ref, x_ref, gain_ref, wg_ref, y0_ref,
                y1_ref, bonus_ref, g_ref, lnw_ref, lnb_ref, bd_ref, wua_ref, wur_ref, wout_ref, o_ref):
    n = pl.program_id(1)
    nb = pl.num_programs(1)
    blk, nsub = ATTN_BLOCK, MID_BLOCKS
    d = x_ref.shape[2]
    k = jnp.concatenate([kp_ref[0], kc_ref[0], kn_ref[0]], axis=0)
    v = jnp.concatenate([vp_ref[0], vc_ref[0], vn_ref[0]], axis=0)
    qi = lax.broadcasted_iota(jnp.int32, (2 * blk, blk), 0) % blk
    ki = lax.broadcasted_iota(jnp.int32, (2 * blk, blk), 1)
    first = lax.broadcasted_iota(jnp.int32, (2 * blk, 1), 0) < blk
    left = lax.broadcasted_iota(jnp.int32, (1, LANES), 1) < HEAD_DIM
    zero = jnp.zeros((), BF16)
    bd = bd_ref[...]

    def scores(j):
        rows = slice(j * blk, (j + 1) * blk)
        win = slice(j * blk, (j + 3) * blk)
        mask_prev = (ki >= qi) & ((n > 0) if j == 0 else True)
        mask_next = (ki <= qi) & ((n < nb - 1) if j == nsub - 1 else True)
        recs = []
        for g in range(KV_HEADS):
            pairs = jnp.concatenate([q_ref[0, rows, p * LANES:(p + 1) * LANES] for p in (2 * g, 2 * g + 1)], axis=0)
            for side in range(2):
                off = LANES * ((g + side) % 2)
                keep = left if side == 0 else jnp.logical_not(left)
                sk = jnp.where(first, sink_ref[4 * g + side], sink_ref[4 * g + 2 + side])
                sc = _dot_nt(jnp.where(keep, pairs, zero), k[win, off:off + LANES])
                sc = jnp.concatenate([jnp.where(mask_prev, sc[:, :blk], -1e30), sc[:, blk:2 * blk],
                                      jnp.where(mask_next, sc[:, 2 * blk:], -1e30)], axis=1)
                recs.append(dict(sc=sc, sk=sk, v=jnp.where(keep, v[win, off:off + LANES], zero)))
        return recs

    def softmax(recs):
        for c in recs:
            m = jnp.maximum(jnp.max(c["sc"], axis=-1, keepdims=True), c["sk"])
            e = jnp.exp(c["sc"] - m)
            c["den"] = jnp.sum(e, axis=-1, keepdims=True) + jnp.exp(c["sk"] - m)
            c["e"] = e.astype(BF16)

    def weighted(recs):
        outs = []
        for g in range(KV_HEADS):
            both = sum(_dot(c["e"], c["v"]) / c["den"] for c in recs[2 * g:2 * g + 2])
            outs += [both[:blk], both[blk:]]
        return jnp.concatenate(outs, axis=1).astype(BF16)

    quarter = wg_ref.shape[1] // 4
    x = x_ref[0]
    blocks = [scores(0), scores(1)]
    h = _rms(x, gain_ref[...]).astype(BF16)
    gate_pre = [_dot(h, wg_ref[:, :quarter])]
    softmax(blocks[0])
    gate_pre.append(_dot(h, wg_ref[:, quarter:2 * quarter]))
    softmax(blocks[1])
    attn = [weighted(blocks[0])]
    blocks.append(scores(2))
    gates = [_sigmoid(gate_pre[0])]
    gate_pre.append(_dot(h, wg_ref[:, 2 * quarter:3 * quarter]))
    attn.append(weighted(blocks[1]))
    blocks.append(scores(3))
    softmax(blocks[2])
    gate_pre.append(_dot(h, wg_ref[:, 3 * quarter:]))
    gates.append(_sigmoid(gate_pre[1]))
    softmax(blocks[3])
    attn.append(weighted(blocks[2]))

    y = y0_ref[0] + y1_ref[0]
    yc = y - _head_sum(y, bd) * (1.0 / HEAD_DIM)
    var = _head_sum(yc * yc, bd, split=False) * (1.0 / HEAD_DIM)
    gates.append(_sigmoid(gate_pre[2]))
    attn.append(weighted(blocks[3]))
    yn = yc * lax.rsqrt(var + GN_EPS) * lnw_ref[...] + lnb_ref[...]
    rw = ((yn + bonus_ref[0]) * g_ref[0]).astype(BF16)
    up_attn = _dot(jnp.concatenate(attn, axis=0), wua_ref[...])
    gates.append(_sigmoid(gate_pre[3]))
    up_rwkv = _dot(rw, wur_ref[...])
    half = d // 2
    merged = (jnp.concatenate(gates[:2], axis=1) * up_attn + jnp.concatenate(gates[2:], axis=1) * up_rwkv).astype(BF16)
    for col in range(0, d, half):
        o_ref[0, :, col:col + half] = x[:, col:col + half] + _dot(merged, wout_ref[:, col:col + half])


def _mid(sink, q, k2, v2, x, gain, wg, y0, y1, bonus, g, lnw, lnb, bd, wua, wur, wout):
    b, s, d = x.shape
    c = RWKV_WIDTH
    tm = ATTN_BLOCK * MID_BLOCKS
    nb = s // tm
    last = s // ATTN_BLOCK - 1
    const = lambda shape: pl.BlockSpec(shape, lambda bi, n: (0,) * len(shape))
    tok = lambda width: pl.BlockSpec((1, tm, width), lambda bi, n: (bi, n, 0))
    prev = pl.BlockSpec((1, ATTN_BLOCK, 2 * KV_WIDTH), lambda bi, n: (bi, jnp.maximum(n * MID_BLOCKS - 1, 0), 0))
    nxt = pl.BlockSpec((1, ATTN_BLOCK, 2 * KV_WIDTH), lambda bi, n: (bi, jnp.minimum((n + 1) * MID_BLOCKS, last), 0))
    kv = [prev, tok(2 * KV_WIDTH), nxt]
    return pl.pallas_call(
        _mid_kernel,
        grid=(b, nb),
        in_specs=[pl.BlockSpec(memory_space=pltpu.SMEM), tok(ATTN_WIDTH)] + kv + kv
                 + [tok(d), const((1, d)), const((d, 2 * d)), tok(c), tok(c), tok(c), tok(c), const((1, c)),
                    const((1, c)), const((LANES, LANES)), const((ATTN_WIDTH, d)), const((c, d)), const((d, d))],
        out_specs=tok(d),
        out_shape=jax.ShapeDtypeStruct((b, s, d), F32),
        compiler_params=_cparams(("parallel", "parallel")),
        name="mid",
    )(sink, q, k2, k2, k2, v2, v2, v2, x, gain, wg, y0, y1, bonus, g, lnw, lnb, bd, wua, wur, wout)


FF_SLAB = 1024


def _ffn_kernel(x_ref, p_ref, gf_ref, w1_ref, w2_ref, gp_ref, wpg_ref, wple_ref, o_ref):
    x = x_ref[0]
    h = _rms(x, gf_ref[...]).astype(BF16)
    acc = x
    for j in range(0, w1_ref.shape[1], FF_SLAB):
        hid = jnp.maximum(_dot(h, w1_ref[:, j:j + FF_SLAB]), 0.0)
        acc = acc + _dot((hid * hid).astype(BF16), w2_ref[j:j + FF_SLAB, :])
    hp = _rms(acc, gp_ref[...]).astype(BF16)
    o_ref[0] = acc + _dot(p_ref[0].astype(BF16), wple_ref[...]) * _sigmoid(_dot(hp, wpg_ref[...]))


def _ffn(x, p, gf, w1, w2, gp, wpg, wple, tm):
    b, s, d = x.shape
    dff = w1.shape[1]
    pd = p.shape[-1]
    const = lambda shape: pl.BlockSpec(shape, lambda bi, i: (0,) * len(shape))
    tok = lambda width: pl.BlockSpec((1, tm, width), lambda bi, i: (bi, i, 0))
    return pl.pallas_call(
        _ffn_kernel,
        grid=(b, s // tm),
        in_specs=[tok(d), tok(pd), const((1, d)), const((d, dff)), const((dff, d)), const((1, d)),
                  const((d, d)), const((pd, d))],
        out_specs=tok(d),
        out_shape=jax.ShapeDtypeStruct((b, s, d), F32),
        compiler_params=_cparams(("parallel", "parallel")),
        name="ffn",
    )(x, p, gf, w1, w2, gp, wpg, wple)


def _rotary_tables(s):
    half = ROT_DIM // 2
    inv_freq = jnp.power(jnp.float32(ROPE_THETA), -jnp.arange(half, dtype=F32) * 2.0 / ROT_DIM)
    ang = jnp.arange(s).astype(F32)[:, None] * inv_freq[None, :]
    cos, sin = jnp.cos(ang), jnp.sin(ang)
    pad = jnp.zeros((s, HEAD_DIM - ROT_DIM), F32)
    zero = jnp.zeros((s, half), F32)
    c = jnp.concatenate([cos, cos, pad + 1.0], axis=1)
    s1 = jnp.concatenate([-sin, zero, pad], axis=1)
    s2 = jnp.concatenate([zero, sin, pad], axis=1)
    rep = LANES // HEAD_DIM
    return jnp.tile(c, (1, rep)), jnp.tile(s1, (1, rep)), jnp.tile(s2, (1, rep))


def kernel(x, p, norm_mix, w_in, shift_mu, q_norm, k_norm, sink, w0, w2, a0, a2, g2, k_k, k_a, r_k, lnx_w, lnx_b,
           w_up_attn, w_up_rwkv, w_out, norm_ffn, w_ff1, w_ff2, norm_ple, w_ple_gate, w_ple):
    bsz, s, d = x.shape
    depth = w_in.shape[0]
    c = RWKV_WIDTH
    tm = min(512, s)
    cos, s1, s2 = _rotary_tables(s)
    lane = jnp.arange(LANES)
    bd = ((lane[:, None] // HEAD_DIM) == (lane[None, :] // HEAD_DIM)).astype(BF16)
    o_g = PROJ_W
    for i in range(depth):
        wi = w_in[i]
        qkg = jnp.concatenate([jnp.tile(q_norm[i], Q_HEADS), jnp.tile(k_norm[i], KV_HEADS)])[None, :]
        zpad = jnp.zeros((DECAY_RANK, c), F32)
        wlo = jnp.concatenate([jnp.concatenate([w2[i, 0], zpad], axis=0), jnp.concatenate([w2[i, 1], zpad], axis=0),
                               jnp.concatenate([zpad, a2[i, 0]], axis=0), jnp.concatenate([zpad, a2[i, 1]], axis=0)],
                              axis=1).astype(BF16)
        full = lambda w: (w, 0, w.shape[1])
        ((q, k2, v2, r, v, a, lw0, lw1, k0, k1, b0, b1, g, bonus), (wg, wua, wur, wo, wf1, wf2, wpg, wpl)) = _front(
            x, norm_mix[i][None, :], wi, cos, s1, s2, qkg, bd, shift_mu[i][None, :], wlo, g2[i].astype(BF16),
            w0[i], a0[i], k_k[i][None, :], k_a[i][None, :], r_k[i].reshape(2, c), tm,
            [(wi, o_g, wi.shape[1]), full(w_up_attn[i]), full(w_up_rwkv[i]), full(w_out[i]), full(w_ff1[i]),
             full(w_ff2[i]), full(w_ple_gate[i]), full(w_ple[i])])
        y0, y1 = _scan(r, v, a, lw0, lw1, k0, k1, b0, b1)
        x = _mid(sink[i], q, k2, v2, x, norm_mix[i][None, :], wg, y0, y1, bonus, g, lnx_w[i][None, :],
                 lnx_b[i][None, :], bd, wua, wur, wo)
        x = _ffn(x, p[i], norm_ffn[i][None, :], wf1, wf2, norm_ple[i][None, :], wpg, wpl, tm)
    return x
```

```python
import functools
import math

import jax
import jax.numpy as jnp
from jax import lax
from jax.experimental import pallas as pl
from jax.experimental.pallas import tpu as pltpu

F32 = jnp.float32
BF16 = jnp.bfloat16

LANES = 128
HEAD_DIM = 64
Q_HEADS = 8
KV_HEADS = 2
ATTN_WIDTH = Q_HEADS * HEAD_DIM
KV_WIDTH = KV_HEADS * HEAD_DIM
WINDOW = 128
ATTN_BLOCK = 128
ROPE_THETA = 500000.0
ROT_DIM = HEAD_DIM // 4
RWKV_WIDTH = 512
DECAY_RANK = 64
ICLR_RANK = 64
GATE_RANK = 128
RWKV_COLS = 3 * RWKV_WIDTH + DECAY_RANK + ICLR_RANK + GATE_RANK
NORM_EPS = 1e-6
GN_EPS = 64e-5
CHUNK = 64
GROUP_W = 128
SCAN_CHUNKS = 8
VMEM_LIMIT = 56 * 1024 * 1024


def _cparams(sem):
    return pltpu.CompilerParams(dimension_semantics=sem, vmem_limit_bytes=VMEM_LIMIT)


def _dot(a, b):
    return jnp.dot(a, b, preferred_element_type=F32)


def _dot_nt(a, b):
    return lax.dot_general(a, b, (((1,), (1,)), ((), ())), preferred_element_type=F32)


def _dot_tn(a, b):
    return lax.dot_general(a, b, (((0,), (0,)), ((), ())), preferred_element_type=F32)


def _split_dot(x, w_bf16):
    hi = x.astype(BF16)
    lo = (x - hi.astype(F32)).astype(BF16)
    return _dot(hi, w_bf16) + _dot(lo, w_bf16)


def _head_sum(x, bd, split=True):
    one = _split_dot if split else (lambda t, w: _dot(t.astype(BF16), w))
    cols = [one(x[:, i:i + LANES], bd) for i in range(0, x.shape[1], LANES)]
    return cols[0] if len(cols) == 1 else jnp.concatenate(cols, axis=1)


def _rms(x, gain):
    return x * lax.rsqrt(jnp.mean(x * x, axis=-1, keepdims=True) + NORM_EPS) * gain


def _sigmoid(x):
    return 0.5 * jnp.tanh(0.5 * x) + 0.5


QK_W = ATTN_WIDTH + KV_WIDTH
QKV_W = QK_W + KV_WIDTH
PROJ_W = QKV_W + RWKV_COLS
HALO = 8
FRONT_SUB = 128


def _qk_rows(qk, cos, s1, s2, gains, bd):
    ms = _head_sum(qk * qk, bd, split=False) * (1.0 / HEAD_DIM)
    qk = qk * lax.rsqrt(ms + NORM_EPS) * gains
    tiles = []
    for i in range(0, qk.shape[1], LANES):
        t = qk[:, i:i + LANES]
        t = t * cos + pltpu.roll(t, LANES - ROT_DIM // 2, 1) * s1 + pltpu.roll(t, ROT_DIM // 2, 1) * s2
        tiles.append(t)
    return jnp.concatenate(tiles, axis=1)


def _slab_plan(weights, steps, step_index):
    ins, in_specs, out_specs, out_shapes, cols = [], [], [], [], []
    for w, lo, hi in weights:
        rows, width = w.shape
        assert rows % steps == 0, (rows, steps)
        n = rows // steps
        ins.append(w.reshape(steps, n, width))
        in_specs.append(pl.BlockSpec((1, n, width), lambda *g: (step_index(*g), 0, 0)))
        out_specs.append(pl.BlockSpec((1, n, hi - lo), lambda *g: (step_index(*g), 0, 0)))
        out_shapes.append(jax.ShapeDtypeStruct((steps, n, hi - lo), BF16))
        cols.append((lo, hi))
    return ins, in_specs, out_specs, out_shapes, cols


N_FRONT_IN, N_FRONT_OUT = 18, 14


def _front_kernel(cols, *refs):
    nw = len(cols)
    (x_ref, xp_ref, xn_ref, gain_ref, w_ref, cos_ref, s1_ref, s2_ref, qkg_ref, bd_ref, mu_ref, wlo_ref,
     g2_ref, w0_ref, a0_ref, kk_ref, ka_ref, rk_ref) = refs[:N_FRONT_IN]
    (q_ref, k_ref, v_ref, r_ref, vv_ref, a_ref, lw0_ref, lw1_ref, k0_ref, k1_ref, b0_ref, b1_ref, g_ref,
     bonus_ref) = refs[N_FRONT_IN + nw:N_FRONT_IN + nw + N_FRONT_OUT]
    for src_ref, dst_ref, (lo, hi) in zip(refs[N_FRONT_IN:], refs[N_FRONT_IN + nw + N_FRONT_OUT:], cols):
        dst_ref[...] = src_ref[:, :, lo:hi].astype(BF16)
    i = pl.program_id(1)
    nt = pl.num_programs(1)
    tm = x_ref.shape[1]
    sub = FRONT_SUB
    c = RWKV_WIDTH
    gain = gain_ref[...]
    bd = bd_ref[...]
    o_r, o_k, o_v, o_low = QKV_W, QKV_W + c, QKV_W + 2 * c, QKV_W + 3 * c
    halo = jnp.concatenate([xp_ref[0], xn_ref[0]], axis=0)
    w_low = w_ref[:, o_low:PROJ_W].astype(BF16)
    h_blocks = [_rms(halo, gain).astype(BF16)]
    low_blocks = [_dot(h_blocks[0], w_low)]
    for s in range(0, tm, sub):
        h_blocks.insert(-1, _rms(x_ref[0, s:s + sub, :], gain).astype(BF16))
        low_blocks.insert(-1, _dot(h_blocks[-2], w_low))
    h_ext = jnp.concatenate(h_blocks, axis=0)
    h = h_ext[:tm]
    z_low = jnp.concatenate(low_blocks, axis=0)

    row = lax.broadcasted_iota(jnp.int32, (HALO, 1), 0)
    ri = lax.broadcasted_iota(jnp.int32, (sub, sub), 0)
    ci = lax.broadcasted_iota(jnp.int32, (sub, sub), 1)
    adjacent = ((ri - ci == 1) | (ci - ri == 1)).astype(BF16)

    def project(lo, hi):
        return _dot(h_ext if lo >= QKV_W else h, w_ref[:, lo:hi].astype(BF16))

    def shift(z, lo, hi):
        mu = mu_ref[:, lo:hi]
        mix = 0.5 * mu
        keep = 1.0 - mu
        prev_row = jnp.where(i > 0, z[tm + HALO - 1:tm + HALO], 0.0)
        next_row = jnp.where(i < nt - 1, z[tm + HALO:tm + HALO + 1], 0.0)
        blocks = []
        for s in range(0, tm, sub):
            zs = z[s:s + sub]
            part = zs * keep + _dot(adjacent, zs.astype(BF16)) * mix
            before = z[s - 1:s] if s > 0 else prev_row
            after = z[s + sub:s + sub + 1] if s + sub < tm else next_row
            blocks += [part[:HALO] + jnp.where(row == 0, before * mix, 0.0), part[HALO:sub - HALO],
                       part[sub - HALO:] + jnp.where(row == HALO - 1, after * mix, 0.0)]
        return jnp.concatenate(blocks, axis=0)

    half = c // 2
    cos, s1, s2 = cos_ref[...], s1_ref[...], s2_ref[...]

    def q_rows(p, j):
        q = _qk_rows(p, cos, s1, s2, qkg_ref[:, j * half:(j + 1) * half], bd)
        q_ref[0, :, j * half:(j + 1) * half] = (q * (HEAD_DIM ** -0.5)).astype(BF16)

    z_k0 = project(o_k, o_k + half)
    z_low = shift(z_low, 3 * c, RWKV_COLS)
    lowrank = z_low[:, :LANES]
    left = lax.broadcasted_iota(jnp.int32, (1, LANES), 1) < DECAY_RANK
    lowrank = jnp.where(left, jnp.tanh(lowrank), lowrank).astype(BF16)
    gate_in = _sigmoid(z_low[:, LANES:]).astype(BF16)

    z_k1 = project(o_k + half, o_k + c)
    wa = _dot(lowrank, wlo_ref[...])
    g_ref[0] = _dot(gate_in, g2_ref[...])
    p_q0 = project(0, half)

    lw_scale = -0.5 * math.exp(-0.5)
    lw0_ref[0] = lw_scale * jnp.tanh(w0_ref[0:1, :] + wa[:, :c]) + lw_scale
    p_q1 = project(half, c)
    lw1_ref[0] = lw_scale * jnp.tanh(w0_ref[1:2, :] + wa[:, c:2 * c]) + lw_scale
    z_r0 = project(o_r, o_r + half)

    k = shift(jnp.concatenate([z_k0, z_k1], axis=1), c, 2 * c)
    kk = k * kk_ref[...]
    kk = kk * lax.rsqrt(jnp.maximum(_head_sum(kk * kk, bd, split=False), 1e-24))
    a_ref[0] = kk.astype(BF16)
    kk_half = 0.5 * kk
    z_r1 = project(o_r + half, o_r + c)

    later = [lambda: project(o_v, o_v + half), lambda: project(o_v + half, o_v + c)]
    z_v = []
    dot_rk = None
    for d, (kd_ref, bdir_ref) in enumerate(((k0_ref, b0_ref), (k1_ref, b1_ref))):
        t = jnp.tanh(a0_ref[d:d + 1, :] + wa[:, (2 + d) * c:(3 + d) * c])
        k_dir = k * (ka_ref[0:1, :] + ka_ref[1:2, :] * t)
        kd_ref[0] = k_dir.astype(BF16)
        bdir_ref[0] = (kk_half * t + kk_half).astype(BF16)
        term = k_dir * rk_ref[d:d + 1, :]
        dot_rk = term if dot_rk is None else dot_rk + term
        z_v.append(later[d]())
        q_rows((p_q0, p_q1)[d], d)

    p_kv = project(ATTN_WIDTH, QKV_W)
    r = shift(jnp.concatenate([z_r0, z_r1], axis=1), 0, c)
    r_ref[0] = r.astype(BF16)
    r_dot = _head_sum(r * dot_rk, bd, split=False)
    k_att = _qk_rows(p_kv[:, :KV_WIDTH], cos, s1, s2, qkg_ref[:, ATTN_WIDTH:], bd)
    k_ref[0] = jnp.concatenate([k_att, pltpu.roll(k_att, HEAD_DIM, 1)], axis=1).astype(BF16)
    v = shift(jnp.concatenate(z_v, axis=1), 2 * c, 3 * c)
    vv_ref[0] = v.astype(BF16)
    bonus_ref[0] = r_dot * v
    v_att = p_kv[:, KV_WIDTH:]
    v_ref[0] = jnp.concatenate([v_att, pltpu.roll(v_att, HEAD_DIM, 1)], axis=1).astype(BF16)


def _front(x, gain, w, cos, s1, s2, qkg, bd, mu, wlo, g2, w0, a0, k_k, k_a, r_k, tm, weights):
    b, s, d = x.shape
    nt = s // tm
    w_ins, w_in_specs, w_out_specs, w_out_shapes, cols = _slab_plan(weights, b * nt, lambda bi, i: bi * nt + i)
    c = RWKV_WIDTH
    nh = tm // HALO
    const = lambda shape: pl.BlockSpec(shape, lambda bi, i: (0,) * len(shape))
    tok = lambda width: pl.BlockSpec((1, tm, width), lambda bi, i: (bi, i, 0))
    tab = pl.BlockSpec((tm, LANES), lambda bi, i: (i, 0))
    halo_p = pl.BlockSpec((1, HALO, d), lambda bi, i: (bi, jnp.maximum(i * nh - 1, 0), 0))
    halo_n = pl.BlockSpec((1, HALO, d), lambda bi, i: (bi, jnp.minimum((i + 1) * nh, s // HALO - 1), 0))
    rwkv_dtypes = (BF16,) * 3 + (F32,) * 2 + (BF16,) * 4 + (F32,) * 2
    outs = pl.pallas_call(
        functools.partial(_front_kernel, cols),
        grid=(b, nt),
        in_specs=[tok(d), halo_p, halo_n, const((1, d)), const((d, PROJ_W)), tab, tab, tab, const((1, QK_W)),
                  const((LANES, LANES)), const((1, RWKV_COLS)), const((LANES, 4 * c)), const((GATE_RANK, c)),
                  const((2, c)), const((2, c)), const((1, c)), const((2, c)), const((2, c))] + w_in_specs,
        out_specs=[tok(ATTN_WIDTH), tok(2 * KV_WIDTH), tok(2 * KV_WIDTH)] + [tok(c)] * 11 + w_out_specs,
        out_shape=[jax.ShapeDtypeStruct((b, s, ATTN_WIDTH), BF16),
                   jax.ShapeDtypeStruct((b, s, 2 * KV_WIDTH), BF16),
                   jax.ShapeDtypeStruct((b, s, 2 * KV_WIDTH), BF16)]
                  + [jax.ShapeDtypeStruct((b, s, c), dt) for dt in rwkv_dtypes] + w_out_shapes,
        compiler_params=_cparams(("parallel", "parallel")),
        name="front",
    )(x, x, x, gain, w, cos, s1, s2, qkg, bd, mu, wlo, g2, w0, a0, k_k, k_a, r_k, *w_ins)
    converted = [o.reshape(wt.shape[0], hi - lo) for o, (wt, lo, hi) in zip(outs[N_FRONT_OUT:], weights)]
    return outs[:N_FRONT_OUT], converted


def _scan_kernel(r0_ref, v0_ref, a0_ref, lw0_ref, k0_ref, b0_ref,
                 r1_ref, v1_ref, a1_ref, lw1_ref, k1_ref, b1_ref,
                 y0_ref, y1_ref, ht_ref):
    L, gw, nch = CHUNK, GROUP_W, SCAN_CHUNKS
    rep = gw // L
    ng = RWKV_WIDTH // gw
    rows = nch * L

    @pl.when(pl.program_id(1) == 0)
    def _():
        ht_ref[...] = jnp.zeros_like(ht_ref)

    tau = lax.broadcasted_iota(jnp.int32, (L, gw), 0)
    sig = lax.broadcasted_iota(jnp.int32, (L, gw), 1) % L
    bd = (lax.broadcasted_iota(jnp.int32, (gw, gw), 0) // L) == (lax.broadcasted_iota(jnp.int32, (gw, gw), 1) // L)
    half = 2
    ngrp = nch // half
    hrows = half * L
    ti = lax.broadcasted_iota(jnp.int32, (hrows, hrows), 0)
    si = lax.broadcasted_iota(jnp.int32, (hrows, hrows), 1)
    same_chunk = (ti // L) == (si // L)
    tris = ((same_chunk & (si <= ti)).astype(BF16), (same_chunk & (si >= ti)).astype(BF16))
    masks = ((sig < tau, sig <= tau), (sig > tau, sig >= tau))

    def bdtile(x):
        xb = x.astype(BF16)
        return jnp.where(bd, jnp.concatenate([xb] * rep, axis=0), jnp.zeros((), BF16))

    dirs = ((r0_ref, v0_ref, a0_ref, lw0_ref, k0_ref, b0_ref),
            (r1_ref, v1_ref, a1_ref, lw1_ref, k1_ref, b1_ref))

    def build(grp):
        out = {}
        for d, (r_ref, v_ref, a_ref, lw_ref, k_ref, b_ref) in enumerate(dirs):
            base = (grp if d == 0 else ngrp - 1 - grp) * hrows
            rsl = slice(base, base + hrows)
            strict, incl = masks[d]
            lw = lw_ref[0, rsl, :]
            cl = _split_dot_left(tris[d], lw)
            e_in = jnp.exp(cl)
            e_inv = jnp.exp(-cl)
            e_ex = jnp.exp(cl - lw)
            a_t = -(a_ref[0, rsl, :].astype(F32) * e_ex)
            r_t = r_ref[0, rsl, :].astype(F32) * e_in
            b_t = b_ref[0, rsl, :].astype(F32) * e_inv
            k_t = k_ref[0, rsl, :].astype(F32) * e_inv
            v_all = v_ref[0, rsl, :]
            for jj in range(half):
                rs = slice(jj * L, (jj + 1) * L)
                last = (jj + 1) * L - 1 if d == 0 else jj * L
                d_end = jnp.exp(cl[last:last + 1, :])
                b_h = b_t[rs] * d_end
                k_h = k_t[rs] * d_end
                for g in range(ng):
                    sl = slice(g * gw, (g + 1) * gw)
                    out[(base // L + jj, d, g)] = dict(
                        strict=strict, incl=incl, at=a_t[rs, sl], rt=r_t[rs, sl], vv=v_all[rs, sl], bt=b_t[rs, sl],
                        kt=k_t[rs, sl], bh=b_h[:, sl], kh=k_h[:, sl], dend=d_end[:, sl])
        return out

    def s_gram(par):
        for c in par:
            lhs = jnp.concatenate([c["at"], c["rt"]], axis=0).astype(BF16)
            gram = _dot_nt(lhs, jnp.concatenate([bdtile(c["bt"]), bdtile(c["kt"])], axis=0))
            c["gb"], c["gk"] = gram[:, :gw], gram[:, gw:]

    def s_vprod(par):
        for c in par:
            c["m_rb"] = jnp.where(c["incl"], c["gb"][L:], 0.0).astype(BF16)
            c["p"] = jnp.where(c["strict"], c["gb"][:L], 0.0)
            m_k = jnp.concatenate([jnp.where(c["strict"], c["gk"][:L], 0.0), jnp.where(c["incl"], c["gk"][L:], 0.0)],
                                  axis=0)
            res = _dot(m_k.astype(BF16), bdtile(c["vv"]))
            c["makv"], c["mrkv"] = res[:L], res[L:]

    levels = int(math.log2(L))

    def s_level(lvl):
        def run(par):
            for c in par:
                ops = ([] if lvl == 0 else [bdtile(c["t"])]) + ([] if lvl == levels - 1 else [bdtile(c["p"])])
                c["res"] = _dot(c["p"].astype(BF16), ops[0] if len(ops) == 1 else jnp.concatenate(ops, axis=1))
            for c in par:
                if lvl == 0:
                    c["t"] = jnp.where(sig == tau, 1.0, c["p"])
                    c["p"] = c["res"]
                else:
                    c["t"] = c["t"] + c["res"][:, :gw]
                    if lvl < levels - 1:
                        c["p"] = c["res"][:, gw:]
        return run

    def s_apply(par):
        for c in par:
            res = _dot(c["t"].astype(BF16), jnp.concatenate([bdtile(c["at"]), bdtile(c["makv"])], axis=1))
            c["ah"], c["uloc"] = res[:, :gw], res[:, gw:]

    def s_g(par):
        for c in par:
            c["bkh"] = jnp.concatenate([c["bh"], c["kh"]], axis=0).astype(BF16)
            c["g"] = jnp.where(bd, _dot_tn(c["ah"].astype(BF16), c["bkh"][:L]), 0.0).astype(BF16)

    def s_c(par):
        for c in par:
            c["cst"] = jnp.where(bd, _dot_tn(jnp.concatenate([c["uloc"].astype(BF16), c["vv"]], axis=0), c["bkh"]),
                                 0.0)

    def s_q(par):
        for c in par:
            res = _dot(c["m_rb"], jnp.concatenate([bdtile(c["ah"]), bdtile(c["uloc"])], axis=1))
            c["qh"] = (c["rt"] + res[:, :gw]).astype(BF16)
            c["yloc"] = res[:, gw:] + c["mrkv"]

    stages = [s_gram, s_vprod] + [s_level(lvl) for lvl in range(levels)] + [s_apply, s_g, s_c, s_q]

    state = {(d, g): ht_ref[d, g] for d in range(2) for g in range(ng)}
    recs = {}

    def seq_step(step):
        cur = [(d, g, recs[(step if d == 0 else nch - 1 - step, d, g)]) for d in range(2) for g in range(ng)]
        for d, g, c in cur:
            c["htb"] = state[(d, g)].astype(BF16)
        for d, g, c in cur:
            state[(d, g)] = state[(d, g)] * c["dend"] + c["cst"] + _dot(c["htb"], c["g"])
        for d, g, c in cur:
            c["y"] = c["yloc"] + _dot_nt(c["qh"], c["htb"])

    slots = {(i + 1) * len(stages) // (half + 1): i for i in range(half)}
    groups = [build(0)]
    recs.update(groups[0])
    for k in range(ngrp):
        for s, fn in enumerate(stages):
            if s == 2 and k + 1 < ngrp:
                groups.append(build(k + 1))
                recs.update(groups[k + 1])
            fn(list(groups[k].values()))
            if k > 0 and s in slots:
                seq_step((k - 1) * half + slots[s])
    for step in range(nch - half, nch):
        seq_step(step)
    for (d, g), h in state.items():
        ht_ref[d, g] = h
    for d, y_ref in enumerate((y0_ref, y1_ref)):
        y_ref[0] = jnp.concatenate(
            [jnp.concatenate([recs[(j, d, g)]["y"] for g in range(ng)], axis=1) for j in range(nch)], axis=0)


def _split_dot_left(w_bf16, x):
    hi = x.astype(BF16)
    lo = (x - hi.astype(F32)).astype(BF16)
    return _dot(w_bf16, hi) + _dot(w_bf16, lo)


def _scan(r, v, a, lw0, lw1, k0, k1, b0, b1):
    b, s, c = r.shape
    rows = CHUNK * SCAN_CHUNKS
    nblk = s // rows
    fwd = pl.BlockSpec((1, rows, c), lambda bi, i: (bi, i, 0))
    bwd = pl.BlockSpec((1, rows, c), lambda bi, i: (bi, nblk - 1 - i, 0))
    ng = c // GROUP_W
    return pl.pallas_call(
        _scan_kernel,
        grid=(b, nblk),
        in_specs=[fwd] * 6 + [bwd] * 6,
        out_specs=[fwd, bwd],
        out_shape=[jax.ShapeDtypeStruct((b, s, c), F32)] * 2,
        scratch_shapes=[pltpu.VMEM((2, ng, GROUP_W, GROUP_W), F32)],
        compiler_params=_cparams(("parallel", "arbitrary")),
        name="scan",
    )(r, v, a, lw0, k0, b0, r, v, a, lw1, k1, b1)


MID_BLOCKS = 4


def _mid_kernel(sink_ref, q_ref, kp_ref, kc_ref, kn_ref, vp_ref, vc_ref, vn_ref, x_ref, gain_ref, wg_ref, y0_ref,
                y1_ref, bonus_ref, g_ref, lnw_ref, lnb_ref, bd_ref, wua_ref, wur_ref, wout_ref, o_ref):
    n = pl.program_id(1)
    nb = pl.num_programs(1)
    blk, nsub = ATTN_BLOCK, MID_BLOCKS
    d = x_ref.shape[2]
    k = jnp.concatenate([kp_ref[0], kc_ref[0], kn_ref[0]], axis=0)
    v = jnp.concatenate([vp_ref[0], vc_ref[0], vn_ref[0]], axis=0)
    qi = lax.broadcasted_iota(jnp.int32, (2 * blk, blk), 0) % blk
    ki = lax.broadcasted_iota(jnp.int32, (2 * blk, blk), 1)
    first = lax.broadcasted_iota(jnp.int32, (2 * blk, 1), 0) < blk
    left = lax.broadcasted_iota(jnp.int32, (1, LANES), 1) < HEAD_DIM
    zero = jnp.zeros((), BF16)
    bd = bd_ref[...]

    def scores(j):
        rows = slice(j * blk, (j + 1) * blk)
        win = slice(j * blk, (j + 3) * blk)
        mask_prev = (ki >= qi) & ((n > 0) if j == 0 else True)
        mask_next = (ki <= qi) & ((n < nb - 1) if j == nsub - 1 else True)
        recs = []
        for g in range(KV_HEADS):
            pairs = jnp.concatenate([q_ref[0, rows, p * LANES:(p + 1) * LANES] for p in (2 * g, 2 * g + 1)], axis=0)
            for side in range(2):
                off = LANES * ((g + side) % 2)
                keep = left if side == 0 else jnp.logical_not(left)
                sk = jnp.where(first, sink_ref[4 * g + side], sink_ref[4 * g + 2 + side])
                sc = _dot_nt(jnp.where(keep, pairs, zero), k[win, off:off + LANES])
                sc = jnp.concatenate([jnp.where(mask_prev, sc[:, :blk], -1e30), sc[:, blk:2 * blk],
                                      jnp.where(mask_next, sc[:, 2 * blk:], -1e30)], axis=1)
                recs.append(dict(sc=sc, sk=sk, v=jnp.where(keep, v[win, off:off + LANES], zero)))
        return recs

    def softmax(recs):
        for c in recs:
            m = jnp.maximum(jnp.max(c["sc"], axis=-1, keepdims=True), c["sk"])
            e = jnp.exp(c["sc"] - m)
            c["den"] = jnp.sum(e, axis=-1, keepdims=True) + jnp.exp(c["sk"] - m)
            c["e"] = e.astype(BF16)

    def weighted(recs):
        outs = []
        for g in range(KV_HEADS):
            both = sum(_dot(c["e"], c["v"]) / c["den"] for c in recs[2 * g:2 * g + 2])
            outs += [both[:blk], both[blk:]]
        return jnp.concatenate(outs, axis=1).astype(BF16)

    quarter = wg_ref.shape[1] // 4
    x = x_ref[0]
    blocks = [scores(0), scores(1)]
    h = _rms(x, gain_ref[...]).astype(BF16)
    gate_pre = [_dot(h, wg_ref[:, :quarter])]
    softmax(blocks[0])
    gate_pre.append(_dot(h, wg_ref[:, quarter:2 * quarter]))
    softmax(blocks[1])
    attn = [weighted(blocks[0])]
    blocks.append(scores(2))
    gates = [_sigmoid(gate_pre[0])]
    gate_pre.append(_dot(h, wg_ref[:, 2 * quarter:3 * quarter]))
    attn.append(weighted(blocks[1]))
    blocks.append(scores(3))
    softmax(blocks[2])
    gate_pre.append(_dot(h, wg_ref[:, 3 * quarter:]))
    gates.append(_sigmoid(gate_pre[1]))
    softmax(blocks[3])
    attn.append(weighted(blocks[2]))

    y = y0_ref[0] + y1_ref[0]
    yc = y - _head_sum(y, bd) * (1.0 / HEAD_DIM)
    var = _head_sum(yc * yc, bd, split=False) * (1.0 / HEAD_DIM)
    gates.append(_sigmoid(gate_pre[2]))
    attn.append(weighted(blocks[3]))
    yn = yc * lax.rsqrt(var + GN_EPS) * lnw_ref[...] + lnb_ref[...]
    rw = ((yn + bonus_ref[0]) * g_ref[0]).astype(BF16)
    up_attn = _dot(jnp.concatenate(attn, axis=0), wua_ref[...])
    gates.append(_sigmoid(gate_pre[3]))
    up_rwkv = _dot(rw, wur_ref[...])
    half = d // 2
    merged = (jnp.concatenate(gates[:2], axis=1) * up_attn + jnp.concatenate(gates[2:], axis=1) * up_rwkv).astype(BF16)
    for col in range(0, d, half):
        o_ref[0, :, col:col + half] = x[:, col:col + half] + _dot(merged, wout_ref[:, col:col + half])


def _mid(sink, q, k2, v2, x, gain, wg, y0, y1, bonus, g, lnw, lnb, bd, wua, wur, wout):
    b, s, d = x.shape
    c = RWKV_WIDTH
    tm = ATTN_BLOCK * MID_BLOCKS
    nb = s // tm
    last = s // ATTN_BLOCK - 1
    const = lambda shape: pl.BlockSpec(shape, lambda bi, n: (0,) * len(shape))
    tok = lambda width: pl.BlockSpec((1, tm, width), lambda bi, n: (bi, n, 0))
    prev = pl.BlockSpec((1, ATTN_BLOCK, 2 * KV_WIDTH), lambda bi, n: (bi, jnp.maximum(n * MID_BLOCKS - 1, 0), 0))
    nxt = pl.BlockSpec((1, ATTN_BLOCK, 2 * KV_WIDTH), lambda bi, n: (bi, jnp.minimum((n + 1) * MID_BLOCKS, last), 0))
    kv = [prev, tok(2 * KV_WIDTH), nxt]
    return pl.pallas_call(
        _mid_kernel,
        grid=(b, nb),
        in_specs=[pl.BlockSpec(memory_space=pltpu.SMEM), tok(ATTN_WIDTH)] + kv + kv
                 + [tok(d), const((1, d)), const((d, 2 * d)), tok(c), tok(c), tok(c), tok(c), const((1, c)),
                    const((1, c)), const((LANES, LANES)), const((ATTN_WIDTH, d)), const((c, d)), const((d, d))],
        out_specs=tok(d),
        out_shape=jax.ShapeDtypeStruct((b, s, d), F32),
        compiler_params=_cparams(("parallel", "parallel")),
        name="mid",
    )(sink, q, k2, k2, k2, v2, v2, v2, x, gain, wg, y0, y1, bonus, g, lnw, lnb, bd, wua, wur, wout)


FF_SLAB = 1024


def _ffn_kernel(x_ref, p_ref, gf_ref, w1_ref, w2_ref, gp_ref, wpg_ref, wple_ref, o_ref):
    x = x_ref[0]
    h = _rms(x, gf_ref[...]).astype(BF16)
    acc = x
    for j in range(0, w1_ref.shape[1], FF_SLAB):
        hid = jnp.maximum(_dot(h, w1_ref[:, j:j + FF_SLAB]), 0.0)
        acc = acc + _dot((hid * hid).astype(BF16), w2_ref[j:j + FF_SLAB, :])
    hp = _rms(acc, gp_ref[...]).astype(BF16)
    o_ref[0] = acc + _dot(p_ref[0].astype(BF16), wple_ref[...]) * _sigmoid(_dot(hp, wpg_ref[...]))


def _ffn(x, p, gf, w1, w2, gp, wpg, wple, tm):
    b, s, d = x.shape
    dff = w1.shape[1]
    pd = p.shape[-1]
    const = lambda shape: pl.BlockSpec(shape, lambda bi, i: (0,) * len(shape))
    tok = lambda width: pl.BlockSpec((1, tm, width), lambda bi, i: (bi, i, 0))
    return pl.pallas_call(
        _ffn_kernel,
        grid=(b, s // tm),
        in_specs=[tok(d), tok(pd), const((1, d)), const((d, dff)), const((dff, d)), const((1, d)),
                  const((d, d)), const((pd, d))],
        out_specs=tok(d),
        out_shape=jax.ShapeDtypeStruct((b, s, d), F32),
        compiler_params=_cparams(("parallel", "parallel")),
        name="ffn",
    )(x, p, gf, w1, w2, gp, wpg, wple)


def _rotary_tables(s):
    half = ROT_DIM // 2
    inv_freq = jnp.power(jnp.float32(ROPE_THETA), -jnp.arange(half, dtype=F32) * 2.0 / ROT_DIM)
    ang = jnp.arange(s).astype(F32)[:, None] * inv_freq[None, :]
    cos, sin = jnp.cos(ang), jnp.sin(ang)
    pad = jnp.zeros((s, HEAD_DIM - ROT_DIM), F32)
    zero = jnp.zeros((s, half), F32)
    c = jnp.concatenate([cos, cos, pad + 1.0], axis=1)
    s1 = jnp.concatenate([-sin, zero, pad], axis=1)
    s2 = jnp.concatenate([zero, sin, pad], axis=1)
    rep = LANES // HEAD_DIM
    return jnp.tile(c, (1, rep)), jnp.tile(s1, (1, rep)), jnp.tile(s2, (1, rep))


def kernel(x, p, norm_mix, w_in, shift_mu, q_norm, k_norm, sink, w0, w2, a0, a2, g2, k_k, k_a, r_k, lnx_w, lnx_b,
           w_up_attn, w_up_rwkv, w_out, norm_ffn, w_ff1, w_ff2, norm_ple, w_ple_gate, w_ple):
    bsz, s, d = x.shape
    depth = w_in.shape[0]
    c = RWKV_WIDTH
    tm = min(512, s)
    cos, s1, s2 = _rotary_tables(s)
    lane = jnp.arange(LANES)
    bd = ((lane[:, None] // HEAD_DIM) == (lane[None, :] // HEAD_DIM)).astype(BF16)
    o_g = PROJ_W
    for i in range(depth):
        wi = w_in[i]
        qkg = jnp.concatenate([jnp.tile(q_norm[i], Q_HEADS), jnp.tile(k_norm[i], KV_HEADS)])[None, :]
        zpad = jnp.zeros((DECAY_RANK, c), F32)
        wlo = jnp.concatenate([jnp.concatenate([w2[i, 0], zpad], axis=0), jnp.concatenate([w2[i, 1], zpad], axis=0),
                               jnp.concatenate([zpad, a2[i, 0]], axis=0), jnp.concatenate([zpad, a2[i, 1]], axis=0)],
                              axis=1)
        wlo = (0.5 * wlo).astype(BF16)
        k_mix = jnp.stack([1.0 - 0.5 * k_a[i], 0.5 * k_a[i]])
        full = lambda w: (w, 0, w.shape[1])
        ((q, k2, v2, r, v, a, lw0, lw1, k0, k1, b0, b1, g, bonus), (wg, wua, wur, wo, wf1, wf2, wpg, wpl)) = _front(
            x, norm_mix[i][None, :], wi, cos, s1, s2, qkg, bd, shift_mu[i][None, :], wlo, g2[i].astype(BF16),
            0.5 * w0[i], 0.5 * a0[i], k_k[i][None, :], k_mix, r_k[i].reshape(2, c), tm,
            [(wi, o_g, wi.shape[1]), full(w_up_attn[i]), full(w_up_rwkv[i]), full(w_out[i]), full(w_ff1[i]),
             full(w_ff2[i]), full(w_ple_gate[i]), full(w_ple[i])])
        y0, y1 = _scan(r, v, a, lw0, lw1, k0, k1, b0, b1)
        x = _mid(sink[i], q, k2, v2, x, norm_mix[i][None, :], wg, y0, y1, bonus, g, lnx_w[i][None, :],
                 lnx_b[i][None, :], bd, wua, wur, wo)
        x = _ffn(x, p[i], norm_ffn[i][None, :], wf1, wf2, norm_ple[i][None, :], wpg, wpl, tm)
    return x
```

```python
import functools
import math

import jax
import jax.numpy as jnp
from jax import lax
from jax.experimental import pallas as pl
from jax.experimental.pallas import tpu as pltpu

F32 = jnp.float32
BF16 = jnp.bfloat16

LANES = 128
HEAD_DIM = 64
Q_HEADS = 8
KV_HEADS = 2
ATTN_WIDTH = Q_HEADS * HEAD_DIM
KV_WIDTH = KV_HEADS * HEAD_DIM
WINDOW = 128
ATTN_BLOCK = 128
ROPE_THETA = 500000.0
ROT_DIM = HEAD_DIM // 4
RWKV_WIDTH = 512
DECAY_RANK = 64
ICLR_RANK = 64
GATE_RANK = 128
RWKV_COLS = 3 * RWKV_WIDTH + DECAY_RANK + ICLR_RANK + GATE_RANK
NORM_EPS = 1e-6
GN_EPS = 64e-5
CHUNK = 64
GROUP_W = 128
SCAN_CHUNKS = 8
VMEM_LIMIT = 56 * 1024 * 1024


def _cparams(sem):
    return pltpu.CompilerParams(dimension_semantics=sem, vmem_limit_bytes=VMEM_LIMIT)


def _dot(a, b):
    return jnp.dot(a, b, preferred_element_type=F32)


def _dot_nt(a, b):
    return lax.dot_general(a, b, (((1,), (1,)), ((), ())), preferred_element_type=F32)


def _dot_tn(a, b):
    return lax.dot_general(a, b, (((0,), (0,)), ((), ())), preferred_element_type=F32)


def _split_dot(x, w_bf16):
    hi = x.astype(BF16)
    lo = (x - hi.astype(F32)).astype(BF16)
    return _dot(hi, w_bf16) + _dot(lo, w_bf16)


def _head_sum(x, bd, split=True):
    one = _split_dot if split else (lambda t, w: _dot(t.astype(BF16), w))
    cols = [one(x[:, i:i + LANES], bd) for i in range(0, x.shape[1], LANES)]
    return cols[0] if len(cols) == 1 else jnp.concatenate(cols, axis=1)


def _rms(x, gain):
    return x * lax.rsqrt(jnp.mean(x * x, axis=-1, keepdims=True) + NORM_EPS) * gain


def _sigmoid(x):
    return 0.5 * jnp.tanh(0.5 * x) + 0.5


QK_W = ATTN_WIDTH + KV_WIDTH
QKV_W = QK_W + KV_WIDTH
PROJ_W = QKV_W + RWKV_COLS
HALO = 8
FRONT_SUB = 128


def _qk_rows(qk, cos, s1, s2, gains, bd):
    ms = _head_sum(qk * qk, bd, split=False) * (1.0 / HEAD_DIM)
    qk = qk * lax.rsqrt(ms + NORM_EPS) * gains
    tiles = []
    for i in range(0, qk.shape[1], LANES):
        t = qk[:, i:i + LANES]
        t = t * cos + pltpu.roll(t, LANES - ROT_DIM // 2, 1) * s1 + pltpu.roll(t, ROT_DIM // 2, 1) * s2
        tiles.append(t)
    return jnp.concatenate(tiles, axis=1)


def _slab_plan(weights, steps, step_index):
    ins, in_specs, out_specs, out_shapes, cols = [], [], [], [], []
    for w, lo, hi in weights:
        rows, width = w.shape
        assert rows % steps == 0, (rows, steps)
        n = rows // steps
        ins.append(w.reshape(steps, n, width))
        in_specs.append(pl.BlockSpec((1, n, width), lambda *g: (step_index(*g), 0, 0)))
        out_specs.append(pl.BlockSpec((1, n, hi - lo), lambda *g: (step_index(*g), 0, 0)))
        out_shapes.append(jax.ShapeDtypeStruct((steps, n, hi - lo), BF16))
        cols.append((lo, hi))
    return ins, in_specs, out_specs, out_shapes, cols


N_FRONT_IN, N_FRONT_OUT = 18, 14


def _front_kernel(cols, *refs):
    nw = len(cols)
    (x_ref, xp_ref, xn_ref, gain_ref, w_ref, cos_ref, s1_ref, s2_ref, qkg_ref, bd_ref, mu_ref, wlo_ref,
     g2_ref, w0_ref, a0_ref, kk_ref, ka_ref, rk_ref) = refs[:N_FRONT_IN]
    (q_ref, k_ref, v_ref, r_ref, vv_ref, a_ref, lw0_ref, lw1_ref, k0_ref, k1_ref, b0_ref, b1_ref, g_ref,
     bonus_ref) = refs[N_FRONT_IN + nw:N_FRONT_IN + nw + N_FRONT_OUT]
    for src_ref, dst_ref, (lo, hi) in zip(refs[N_FRONT_IN:], refs[N_FRONT_IN + nw + N_FRONT_OUT:], cols):
        dst_ref[...] = src_ref[:, :, lo:hi].astype(BF16)
    i = pl.program_id(1)
    nt = pl.num_programs(1)
    tm = x_ref.shape[1]
    sub = FRONT_SUB
    c = RWKV_WIDTH
    gain = gain_ref[...]
    bd = bd_ref[...]
    o_r, o_k, o_v, o_low = QKV_W, QKV_W + c, QKV_W + 2 * c, QKV_W + 3 * c
    halo = jnp.concatenate([xp_ref[0], xn_ref[0]], axis=0)
    w_low = w_ref[:, o_low:PROJ_W].astype(BF16)
    h_blocks = [_rms(halo, gain).astype(BF16)]
    low_blocks = [_dot(h_blocks[0], w_low)]
    for s in range(0, tm, sub):
        h_blocks.insert(-1, _rms(x_ref[0, s:s + sub, :], gain).astype(BF16))
        low_blocks.insert(-1, _dot(h_blocks[-2], w_low))
    h_ext = jnp.concatenate(h_blocks, axis=0)
    h = h_ext[:tm]
    z_low = jnp.concatenate(low_blocks, axis=0)

    row = lax.broadcasted_iota(jnp.int32, (HALO, 1), 0)
    ri = lax.broadcasted_iota(jnp.int32, (sub, sub), 0)
    ci = lax.broadcasted_iota(jnp.int32, (sub, sub), 1)
    adjacent = ((ri - ci == 1) | (ci - ri == 1)).astype(BF16)

    def project(lo, hi):
        return _dot(h_ext if lo >= QKV_W else h, w_ref[:, lo:hi].astype(BF16))

    def shift(z, lo, hi):
        mu = mu_ref[:, lo:hi]
        mix = 0.5 * mu
        keep = 1.0 - mu
        prev_row = jnp.where(i > 0, z[tm + HALO - 1:tm + HALO], 0.0)
        next_row = jnp.where(i < nt - 1, z[tm + HALO:tm + HALO + 1], 0.0)
        blocks = []
        for s in range(0, tm, sub):
            zs = z[s:s + sub]
            part = zs * keep + _dot(adjacent, zs.astype(BF16)) * mix
            before = z[s - 1:s] if s > 0 else prev_row
            after = z[s + sub:s + sub + 1] if s + sub < tm else next_row
            blocks += [part[:HALO] + jnp.where(row == 0, before * mix, 0.0), part[HALO:sub - HALO],
                       part[sub - HALO:] + jnp.where(row == HALO - 1, after * mix, 0.0)]
        return jnp.concatenate(blocks, axis=0)

    half = c // 2
    cos, s1, s2 = cos_ref[...], s1_ref[...], s2_ref[...]

    def q_rows(p, j):
        q = _qk_rows(p, cos, s1, s2, qkg_ref[:, j * half:(j + 1) * half], bd)
        q_ref[0, :, j * half:(j + 1) * half] = (q * (HEAD_DIM ** -0.5)).astype(BF16)

    z_k0 = project(o_k, o_k + half)
    z_low = shift(z_low, 3 * c, RWKV_COLS)
    lowrank = z_low[:, :LANES]
    left = lax.broadcasted_iota(jnp.int32, (1, LANES), 1) < DECAY_RANK
    lowrank = jnp.where(left, jnp.tanh(lowrank), lowrank).astype(BF16)
    gate_in = _sigmoid(z_low[:, LANES:]).astype(BF16)

    z_k1 = project(o_k + half, o_k + c)
    wa = _dot(lowrank, wlo_ref[...])
    g_ref[0] = _dot(gate_in, g2_ref[...])
    p_q0 = project(0, half)

    lw_scale = -0.5 * math.exp(-0.5)
    lw0_ref[0] = lw_scale * jnp.tanh(w0_ref[0:1, :] + wa[:, :c]) + lw_scale
    p_q1 = project(half, c)
    lw1_ref[0] = lw_scale * jnp.tanh(w0_ref[1:2, :] + wa[:, c:2 * c]) + lw_scale
    z_r0 = project(o_r, o_r + half)

    k = shift(jnp.concatenate([z_k0, z_k1], axis=1), c, 2 * c)
    kk = k * kk_ref[...]
    kk = kk * lax.rsqrt(jnp.maximum(_head_sum(kk * kk, bd, split=False), 1e-24))
    a_ref[0] = kk.astype(BF16)
    kk_half = 0.5 * kk
    z_r1 = project(o_r + half, o_r + c)

    later = [lambda: project(o_v, o_v + half), lambda: project(o_v + half, o_v + c)]
    z_v = []
    dot_rk = None
    for d, (kd_ref, bdir_ref) in enumerate(((k0_ref, b0_ref), (k1_ref, b1_ref))):
        t = jnp.tanh(a0_ref[d:d + 1, :] + wa[:, (2 + d) * c:(3 + d) * c])
        k_dir = k * (ka_ref[0:1, :] + ka_ref[1:2, :] * t)
        kd_ref[0] = k_dir.astype(BF16)
        bdir_ref[0] = (kk_half * t + kk_half).astype(BF16)
        term = k_dir * rk_ref[d:d + 1, :]
        dot_rk = term if dot_rk is None else dot_rk + term
        z_v.append(later[d]())
        q_rows((p_q0, p_q1)[d], d)

    p_kv = project(ATTN_WIDTH, QKV_W)
    r = shift(jnp.concatenate([z_r0, z_r1], axis=1), 0, c)
    r_ref[0] = r.astype(BF16)
    r_dot = _head_sum(r * dot_rk, bd, split=False)
    k_att = _qk_rows(p_kv[:, :KV_WIDTH], cos, s1, s2, qkg_ref[:, ATTN_WIDTH:], bd)
    k_ref[0] = jnp.concatenate([k_att, pltpu.roll(k_att, HEAD_DIM, 1)], axis=1).astype(BF16)
    v = shift(jnp.concatenate(z_v, axis=1), 2 * c, 3 * c)
    vv_ref[0] = v.astype(BF16)
    bonus_ref[0] = r_dot * v
    v_att = p_kv[:, KV_WIDTH:]
    v_ref[0] = jnp.concatenate([v_att, pltpu.roll(v_att, HEAD_DIM, 1)], axis=1).astype(BF16)


def _front(x, gain, w, cos, s1, s2, qkg, bd, mu, wlo, g2, w0, a0, k_k, k_a, r_k, tm, weights):
    b, s, d = x.shape
    nt = s // tm
    w_ins, w_in_specs, w_out_specs, w_out_shapes, cols = _slab_plan(weights, b * nt, lambda bi, i: bi * nt + i)
    c = RWKV_WIDTH
    nh = tm // HALO
    const = lambda shape: pl.BlockSpec(shape, lambda bi, i: (0,) * len(shape))
    tok = lambda width: pl.BlockSpec((1, tm, width), lambda bi, i: (bi, i, 0))
    tab = pl.BlockSpec((tm, LANES), lambda bi, i: (i, 0))
    halo_p = pl.BlockSpec((1, HALO, d), lambda bi, i: (bi, jnp.maximum(i * nh - 1, 0), 0))
    halo_n = pl.BlockSpec((1, HALO, d), lambda bi, i: (bi, jnp.minimum((i + 1) * nh, s // HALO - 1), 0))
    rwkv_dtypes = (BF16,) * 3 + (F32,) * 2 + (BF16,) * 4 + (F32,) * 2
    outs = pl.pallas_call(
        functools.partial(_front_kernel, cols),
        grid=(b, nt),
        in_specs=[tok(d), halo_p, halo_n, const((1, d)), const((d, PROJ_W)), tab, tab, tab, const((1, QK_W)),
                  const((LANES, LANES)), const((1, RWKV_COLS)), const((LANES, 4 * c)), const((GATE_RANK, c)),
                  const((2, c)), const((2, c)), const((1, c)), const((2, c)), const((2, c))] + w_in_specs,
        out_specs=[tok(ATTN_WIDTH), tok(2 * KV_WIDTH), tok(2 * KV_WIDTH)] + [tok(c)] * 11 + w_out_specs,
        out_shape=[jax.ShapeDtypeStruct((b, s, ATTN_WIDTH), BF16),
                   jax.ShapeDtypeStruct((b, s, 2 * KV_WIDTH), BF16),
                   jax.ShapeDtypeStruct((b, s, 2 * KV_WIDTH), BF16)]
                  + [jax.ShapeDtypeStruct((b, s, c), dt) for dt in rwkv_dtypes] + w_out_shapes,
        compiler_params=_cparams(("parallel", "parallel")),
        name="front",
    )(x, x, x, gain, w, cos, s1, s2, qkg, bd, mu, wlo, g2, w0, a0, k_k, k_a, r_k, *w_ins)
    converted = [o.reshape(wt.shape[0], hi - lo) for o, (wt, lo, hi) in zip(outs[N_FRONT_OUT:], weights)]
    return outs[:N_FRONT_OUT], converted


def _scan_kernel(r0_ref, v0_ref, a0_ref, lw0_ref, k0_ref, b0_ref,
                 r1_ref, v1_ref, a1_ref, lw1_ref, k1_ref, b1_ref,
                 y0_ref, y1_ref, ht_ref):
    L, gw, nch = CHUNK, GROUP_W, SCAN_CHUNKS
    rep = gw // L
    ng = RWKV_WIDTH // gw
    rows = nch * L

    @pl.when(pl.program_id(1) == 0)
    def _():
        ht_ref[...] = jnp.zeros_like(ht_ref)

    tau = lax.broadcasted_iota(jnp.int32, (L, gw), 0)
    sig = lax.broadcasted_iota(jnp.int32, (L, gw), 1) % L
    bd = (lax.broadcasted_iota(jnp.int32, (gw, gw), 0) // L) == (lax.broadcasted_iota(jnp.int32, (gw, gw), 1) // L)
    half = 2
    ngrp = nch // half
    hrows = half * L
    ti = lax.broadcasted_iota(jnp.int32, (hrows, hrows), 0)
    si = lax.broadcasted_iota(jnp.int32, (hrows, hrows), 1)
    same_chunk = (ti // L) == (si // L)
    tris = ((same_chunk & (si <= ti)).astype(BF16), (same_chunk & (si >= ti)).astype(BF16))
    masks = ((sig < tau, sig <= tau), (sig > tau, sig >= tau))

    def bdtile(x):
        xb = x.astype(BF16)
        return jnp.where(bd, jnp.concatenate([xb] * rep, axis=0), jnp.zeros((), BF16))

    dirs = ((r0_ref, v0_ref, a0_ref, lw0_ref, k0_ref, b0_ref),
            (r1_ref, v1_ref, a1_ref, lw1_ref, k1_ref, b1_ref))

    def build(grp):
        out = {}
        for d, (r_ref, v_ref, a_ref, lw_ref, k_ref, b_ref) in enumerate(dirs):
            base = (grp if d == 0 else ngrp - 1 - grp) * hrows
            rsl = slice(base, base + hrows)
            strict, incl = masks[d]
            lw = lw_ref[0, rsl, :]
            cl = _split_dot_left(tris[d], lw)
            e_in = jnp.exp(cl)
            e_inv = jnp.exp(-cl)
            e_ex = jnp.exp(cl - lw)
            a_t = -(a_ref[0, rsl, :].astype(F32) * e_ex)
            r_t = r_ref[0, rsl, :].astype(F32) * e_in
            b_t = b_ref[0, rsl, :].astype(F32) * e_inv
            k_t = k_ref[0, rsl, :].astype(F32) * e_inv
            v_all = v_ref[0, rsl, :]
            for jj in range(half):
                rs = slice(jj * L, (jj + 1) * L)
                last = (jj + 1) * L - 1 if d == 0 else jj * L
                d_end = jnp.exp(cl[last:last + 1, :])
                b_h = b_t[rs] * d_end
                k_h = k_t[rs] * d_end
                for g in range(ng):
                    sl = slice(g * gw, (g + 1) * gw)
                    out[(base // L + jj, d, g)] = dict(
                        strict=strict, incl=incl, at=a_t[rs, sl], rt=r_t[rs, sl], vv=v_all[rs, sl], bt=b_t[rs, sl],
                        kt=k_t[rs, sl], bh=b_h[:, sl], kh=k_h[:, sl], dend=d_end[:, sl])
        return out

    def s_gram(par):
        for c in par:
            lhs = jnp.concatenate([c["at"], c["rt"]], axis=0).astype(BF16)
            gram = _dot_nt(lhs, jnp.concatenate([bdtile(c["bt"]), bdtile(c["kt"])], axis=0))
            c["gb"], c["gk"] = gram[:, :gw], gram[:, gw:]

    def s_vprod(par):
        for c in par:
            c["m_rb"] = jnp.where(c["incl"], c["gb"][L:], 0.0).astype(BF16)
            c["p"] = jnp.where(c["strict"], c["gb"][:L], 0.0)
            m_k = jnp.concatenate([jnp.where(c["strict"], c["gk"][:L], 0.0), jnp.where(c["incl"], c["gk"][L:], 0.0)],
                                  axis=0)
            res = _dot(m_k.astype(BF16), bdtile(c["vv"]))
            c["makv"], c["mrkv"] = res[:L], res[L:]

    levels = int(math.log2(L))

    def s_level(lvl):
        def run(par):
            for c in par:
                ops = ([] if lvl == 0 else [bdtile(c["t"])]) + ([] if lvl == levels - 1 else [bdtile(c["p"])])
                c["res"] = _dot(c["p"].astype(BF16), ops[0] if len(ops) == 1 else jnp.concatenate(ops, axis=1))
            for c in par:
                if lvl == 0:
                    c["t"] = jnp.where(sig == tau, 1.0, c["p"])
                    c["p"] = c["res"]
                else:
                    c["t"] = c["t"] + c["res"][:, :gw]
                    if lvl < levels - 1:
                        c["p"] = c["res"][:, gw:]
        return run

    def s_apply(par):
        for c in par:
            res = _dot(c["t"].astype(BF16), jnp.concatenate([bdtile(c["at"]), bdtile(c["makv"])], axis=1))
            c["ah"], c["uloc"] = res[:, :gw], res[:, gw:]

    def s_g(par):
        for c in par:
            c["bkh"] = jnp.concatenate([c["bh"], c["kh"]], axis=0).astype(BF16)
            c["g"] = jnp.where(bd, _dot_tn(c["ah"].astype(BF16), c["bkh"][:L]), 0.0).astype(BF16)

    def s_c(par):
        for c in par:
            c["cst"] = jnp.where(bd, _dot_tn(jnp.concatenate([c["uloc"].astype(BF16), c["vv"]], axis=0), c["bkh"]),
                                 0.0)

    def s_q(par):
        for c in par:
            res = _dot(c["m_rb"], jnp.concatenate([bdtile(c["ah"]), bdtile(c["uloc"])], axis=1))
            c["qh"] = (c["rt"] + res[:, :gw]).astype(BF16)
            c["yloc"] = res[:, gw:] + c["mrkv"]

    stages = [s_gram, s_vprod] + [s_level(lvl) for lvl in range(levels)] + [s_apply, s_g, s_c, s_q]

    state = {(d, g): ht_ref[d, g] for d in range(2) for g in range(ng)}
    recs = {}

    def seq_step(step):
        cur = [(d, g, recs[(step if d == 0 else nch - 1 - step, d, g)]) for d in range(2) for g in range(ng)]
        for d, g, c in cur:
            c["htb"] = state[(d, g)].astype(BF16)
        for d, g, c in cur:
            state[(d, g)] = state[(d, g)] * c["dend"] + c["cst"] + _dot(c["htb"], c["g"])
        for d, g, c in cur:
            c["y"] = c["yloc"] + _dot_nt(c["qh"], c["htb"])

    slots = {(i + 1) * len(stages) // (half + 1): i for i in range(half)}
    groups = [build(0)]
    recs.update(groups[0])
    for k in range(ngrp):
        for s, fn in enumerate(stages):
            if s == 2 and k + 1 < ngrp:
                groups.append(build(k + 1))
                recs.update(groups[k + 1])
            fn(list(groups[k].values()))
            if k > 0 and s in slots:
                seq_step((k - 1) * half + slots[s])
    for step in range(nch - half, nch):
        seq_step(step)
    for (d, g), h in state.items():
        ht_ref[d, g] = h
    for d, y_ref in enumerate((y0_ref, y1_ref)):
        y_ref[0] = jnp.concatenate(
            [jnp.concatenate([recs[(j, d, g)]["y"] for g in range(ng)], axis=1) for j in range(nch)], axis=0)


def _split_dot_left(w_bf16, x):
    hi = x.astype(BF16)
    lo = (x - hi.astype(F32)).astype(BF16)
    return _dot(w_bf16, hi) + _dot(w_bf16, lo)


def _scan(r, v, a, lw0, lw1, k0, k1, b0, b1):
    b, s, c = r.shape
    rows = CHUNK * SCAN_CHUNKS
    nblk = s // rows
    fwd = pl.BlockSpec((1, rows, c), lambda bi, i: (bi, i, 0))
    bwd = pl.BlockSpec((1, rows, c), lambda bi, i: (bi, nblk - 1 - i, 0))
    ng = c // GROUP_W
    return pl.pallas_call(
        _scan_kernel,
        grid=(b, nblk),
        in_specs=[fwd] * 6 + [bwd] * 6,
        out_specs=[fwd, bwd],
        out_shape=[jax.ShapeDtypeStruct((b, s, c), F32)] * 2,
        scratch_shapes=[pltpu.VMEM((2, ng, GROUP_W, GROUP_W), F32)],
        compiler_params=_cparams(("parallel", "arbitrary")),
        name="scan",
    )(r, v, a, lw0, k0, b0, r, v, a, lw1, k1, b1)


MID_BLOCKS = 4


def _mid_kernel(sink_ref, q_ref, kp_ref, kc_ref, kn_ref, vp_ref, vc_ref, vn_ref, x_ref, gain_ref, wg_ref, y0_ref,
                y1_ref, bonus_ref, g_ref, lnw_ref, lnb_ref, bd_ref, wua_ref, wur_ref, wout_ref, o_ref):
    n = pl.program_id(1)
    nb = pl.num_programs(1)
    blk, nsub = ATTN_BLOCK, MID_BLOCKS
    d = x_ref.shape[2]
    k = jnp.concatenate([kp_ref[0], kc_ref[0], kn_ref[0]], axis=0)
    v = jnp.concatenate([vp_ref[0], vc_ref[0], vn_ref[0]], axis=0)
    qi = lax.broadcasted_iota(jnp.int32, (2 * blk, blk), 0) % blk
    ki = lax.broadcasted_iota(jnp.int32, (2 * blk, blk), 1)
    first = lax.broadcasted_iota(jnp.int32, (2 * blk, 1), 0) < blk
    left = lax.broadcasted_iota(jnp.int32, (1, LANES), 1) < HEAD_DIM
    zero = jnp.zeros((), BF16)
    bd = bd_ref[...]

    def scores(j):
        rows = slice(j * blk, (j + 1) * blk)
        win = slice(j * blk, (j + 3) * blk)
        mask_prev = (ki >= qi) & ((n > 0) if j == 0 else True)
        mask_next = (ki <= qi) & ((n < nb - 1) if j == nsub - 1 else True)
        recs = []
        for g in range(KV_HEADS):
            pairs = jnp.concatenate([q_ref[0, rows, p * LANES:(p + 1) * LANES] for p in (2 * g, 2 * g + 1)], axis=0)
            for side in range(2):
                off = LANES * ((g + side) % 2)
                keep = left if side == 0 else jnp.logical_not(left)
                sk = jnp.where(first, sink_ref[4 * g + side], sink_ref[4 * g + 2 + side])
                sc = _dot_nt(jnp.where(keep, pairs, zero), k[win, off:off + LANES])
                sc = jnp.concatenate([jnp.where(mask_prev, sc[:, :blk], -1e30), sc[:, blk:2 * blk],
                                      jnp.where(mask_next, sc[:, 2 * blk:], -1e30)], axis=1)
                recs.append(dict(sc=sc, sk=sk, v=jnp.where(keep, v[win, off:off + LANES], zero)))
        return recs

    def softmax(recs):
        for c in recs:
            m = jnp.maximum(jnp.max(c["sc"], axis=-1, keepdims=True), c["sk"])
            e = jnp.exp(c["sc"] - m)
            c["den"] = jnp.sum(e, axis=-1, keepdims=True) + jnp.exp(c["sk"] - m)
            c["e"] = e.astype(BF16)

    def weighted(recs):
        outs = []
        for g in range(KV_HEADS):
            both = sum(_dot(c["e"], c["v"]) / c["den"] for c in recs[2 * g:2 * g + 2])
            outs += [both[:blk], both[blk:]]
        return jnp.concatenate(outs, axis=1).astype(BF16)

    quarter = wg_ref.shape[1] // 4
    x = x_ref[0]
    blocks = [scores(0), scores(1)]
    h = _rms(x, gain_ref[...]).astype(BF16)
    gate_pre = [_dot(h, wg_ref[:, :quarter])]
    softmax(blocks[0])
    gate_pre.append(_dot(h, wg_ref[:, quarter:2 * quarter]))
    softmax(blocks[1])
    attn = [weighted(blocks[0])]
    blocks.append(scores(2))
    gates = [_sigmoid(gate_pre[0])]
    gate_pre.append(_dot(h, wg_ref[:, 2 * quarter:3 * quarter]))
    attn.append(weighted(blocks[1]))
    blocks.append(scores(3))
    softmax(blocks[2])
    gate_pre.append(_dot(h, wg_ref[:, 3 * quarter:]))
    gates.append(_sigmoid(gate_pre[1]))
    softmax(blocks[3])
    attn.append(weighted(blocks[2]))

    y = y0_ref[0] + y1_ref[0]
    yc = y - _head_sum(y, bd) * (1.0 / HEAD_DIM)
    var = _head_sum(yc * yc, bd, split=False) * (1.0 / HEAD_DIM)
    gates.append(_sigmoid(gate_pre[2]))
    attn.append(weighted(blocks[3]))
    yn = yc * lax.rsqrt(var + GN_EPS) * lnw_ref[...] + lnb_ref[...]
    rw = ((yn + bonus_ref[0]) * g_ref[0]).astype(BF16)
    up_attn = _dot(jnp.concatenate(attn, axis=0), wua_ref[...])
    gates.append(_sigmoid(gate_pre[3]))
    up_rwkv = _dot(rw, wur_ref[...])
    half = d // 2
    merged = (jnp.concatenate(gates[:2], axis=1) * up_attn + jnp.concatenate(gates[2:], axis=1) * up_rwkv).astype(BF16)
    for col in range(0, d, half):
        o_ref[0, :, col:col + half] = x[:, col:col + half] + _dot(merged, wout_ref[:, col:col + half])


def _mid(sink, q, k2, v2, x, gain, wg, y0, y1, bonus, g, lnw, lnb, bd, wua, wur, wout):
    b, s, d = x.shape
    c = RWKV_WIDTH
    tm = ATTN_BLOCK * MID_BLOCKS
    nb = s // tm
    last = s // ATTN_BLOCK - 1
    const = lambda shape: pl.BlockSpec(shape, lambda bi, n: (0,) * len(shape))
    tok = lambda width: pl.BlockSpec((1, tm, width), lambda bi, n: (bi, n, 0))
    prev = pl.BlockSpec((1, ATTN_BLOCK, 2 * KV_WIDTH), lambda bi, n: (bi, jnp.maximum(n * MID_BLOCKS - 1, 0), 0))
    nxt = pl.BlockSpec((1, ATTN_BLOCK, 2 * KV_WIDTH), lambda bi, n: (bi, jnp.minimum((n + 1) * MID_BLOCKS, last), 0))
    kv = [prev, tok(2 * KV_WIDTH), nxt]
    return pl.pallas_call(
        _mid_kernel,
        grid=(b, nb),
        in_specs=[pl.BlockSpec(memory_space=pltpu.SMEM), tok(ATTN_WIDTH)] + kv + kv
                 + [tok(d), const((1, d)), const((d, 2 * d)), tok(c), tok(c), tok(c), tok(c), const((1, c)),
                    const((1, c)), const((LANES, LANES)), const((ATTN_WIDTH, d)), const((c, d)), const((d, d))],
        out_specs=tok(d),
        out_shape=jax.ShapeDtypeStruct((b, s, d), F32),
        compiler_params=_cparams(("parallel", "parallel")),
        name="mid",
    )(sink, q, k2, k2, k2, v2, v2, v2, x, gain, wg, y0, y1, bonus, g, lnw, lnb, bd, wua, wur, wout)


FF_SLAB = 1024


def _ffn_kernel(x_ref, p_ref, gf_ref, w1_ref, w2_ref, gp_ref, wpg_ref, wple_ref, o_ref):
    x = x_ref[0]
    h = _rms(x, gf_ref[...]).astype(BF16)
    acc = x
    for j in range(0, w1_ref.shape[1], FF_SLAB):
        hid = jnp.maximum(_dot(h, w1_ref[:, j:j + FF_SLAB]), 0.0)
        acc = acc + _dot((hid * hid).astype(BF16), w2_ref[j:j + FF_SLAB, :])
    hp = _rms(acc, gp_ref[...]).astype(BF16)
    o_ref[0] = acc + _dot(p_ref[0].astype(BF16), wple_ref[...]) * _sigmoid(_dot(hp, wpg_ref[...]))


def _ffn(x, p, gf, w1, w2, gp, wpg, wple, tm):
    b, s, d = x.shape
    dff = w1.shape[1]
    pd = p.shape[-1]
    const = lambda shape: pl.BlockSpec(shape, lambda bi, i: (0,) * len(shape), pipeline_mode=pl.Buffered(1))
    tok = lambda width: pl.BlockSpec((1, tm, width), lambda bi, i: (bi, i, 0))
    return pl.pallas_call(
        _ffn_kernel,
        grid=(b, s // tm),
        in_specs=[tok(d), tok(pd), const((1, d)), const((d, dff)), const((dff, d)), const((1, d)),
                  const((d, d)), const((pd, d))],
        out_specs=tok(d),
        out_shape=jax.ShapeDtypeStruct((b, s, d), F32),
        compiler_params=_cparams(("parallel", "parallel")),
        name="ffn",
    )(x, p, gf, w1, w2, gp, wpg, wple)


def _rotary_tables(s):
    half = ROT_DIM // 2
    inv_freq = jnp.power(jnp.float32(ROPE_THETA), -jnp.arange(half, dtype=F32) * 2.0 / ROT_DIM)
    ang = jnp.arange(s).astype(F32)[:, None] * inv_freq[None, :]
    cos, sin = jnp.cos(ang), jnp.sin(ang)
    pad = jnp.zeros((s, HEAD_DIM - ROT_DIM), F32)
    zero = jnp.zeros((s, half), F32)
    c = jnp.concatenate([cos, cos, pad + 1.0], axis=1)
    s1 = jnp.concatenate([-sin, zero, pad], axis=1)
    s2 = jnp.concatenate([zero, sin, pad], axis=1)
    rep = LANES // HEAD_DIM
    return jnp.tile(c, (1, rep)), jnp.tile(s1, (1, rep)), jnp.tile(s2, (1, rep))


def kernel(x, p, norm_mix, w_in, shift_mu, q_norm, k_norm, sink, w0, w2, a0, a2, g2, k_k, k_a, r_k, lnx_w, lnx_b,
           w_up_attn, w_up_rwkv, w_out, norm_ffn, w_ff1, w_ff2, norm_ple, w_ple_gate, w_ple):
    bsz, s, d = x.shape
    depth = w_in.shape[0]
    c = RWKV_WIDTH
    tm = min(512, s)
    cos, s1, s2 = _rotary_tables(s)
    lane = jnp.arange(LANES)
    bd = ((lane[:, None] // HEAD_DIM) == (lane[None, :] // HEAD_DIM)).astype(BF16)
    o_g = PROJ_W
    for i in range(depth):
        wi = w_in[i]
        qkg = jnp.concatenate([jnp.tile(q_norm[i], Q_HEADS), jnp.tile(k_norm[i], KV_HEADS)])[None, :]
        zpad = jnp.zeros((DECAY_RANK, c), F32)
        wlo = jnp.concatenate([jnp.concatenate([w2[i, 0], zpad], axis=0), jnp.concatenate([w2[i, 1], zpad], axis=0),
                               jnp.concatenate([zpad, a2[i, 0]], axis=0), jnp.concatenate([zpad, a2[i, 1]], axis=0)],
                              axis=1)
        wlo = (0.5 * wlo).astype(BF16)
        k_mix = jnp.stack([1.0 - 0.5 * k_a[i], 0.5 * k_a[i]])
        full = lambda w: (w, 0, w.shape[1])
        ((q, k2, v2, r, v, a, lw0, lw1, k0, k1, b0, b1, g, bonus), (wg, wua, wur, wo, wf1, wf2, wpg, wpl)) = _front(
            x, norm_mix[i][None, :], wi, cos, s1, s2, qkg, bd, shift_mu[i][None, :], wlo, g2[i].astype(BF16),
            0.5 * w0[i], 0.5 * a0[i], k_k[i][None, :], k_mix, r_k[i].reshape(2, c), tm,
            [(wi, o_g, wi.shape[1]), full(w_up_attn[i]), full(w_up_rwkv[i]), full(w_out[i]), full(w_ff1[i]),
             full(w_ff2[i]), full(w_ple_gate[i]), full(w_ple[i])])
        y0, y1 = _scan(r, v, a, lw0, lw1, k0, k1, b0, b1)
        x = _mid(sink[i], q, k2, v2, x, norm_mix[i][None, :], wg, y0, y1, bonus, g, lnx_w[i][None, :],
                 lnx_b[i][None, :], bd, wua, wur, wo)
        x = _ffn(x, p[i], norm_ffn[i][None, :], wf1, wf2, norm_ple[i][None, :], wpg, wpl, min(1024, s))
    return x
```

```python
import functools
import math

import jax
import jax.numpy as jnp
from jax import lax
from jax.experimental import pallas as pl
from jax.experimental.pallas import tpu as pltpu

F32 = jnp.float32
BF16 = jnp.bfloat16

LANES = 128
HEAD_DIM = 64
Q_HEADS = 8
KV_HEADS = 2
ATTN_WIDTH = Q_HEADS * HEAD_DIM
KV_WIDTH = KV_HEADS * HEAD_DIM
WINDOW = 128
ATTN_BLOCK = 128
ROPE_THETA = 500000.0
ROT_DIM = HEAD_DIM // 4
RWKV_WIDTH = 512
DECAY_RANK = 64
ICLR_RANK = 64
GATE_RANK = 128
RWKV_COLS = 3 * RWKV_WIDTH + DECAY_RANK + ICLR_RANK + GATE_RANK
NORM_EPS = 1e-6
GN_EPS = 64e-5
CHUNK = 64
GROUP_W = 128
SCAN_CHUNKS = 8
VMEM_LIMIT = 56 * 1024 * 1024


def _cparams(sem):
    return pltpu.CompilerParams(dimension_semantics=sem, vmem_limit_bytes=VMEM_LIMIT)


def _dot(a, b):
    return jnp.dot(a, b, preferred_element_type=F32)


def _dot_nt(a, b):
    return lax.dot_general(a, b, (((1,), (1,)), ((), ())), preferred_element_type=F32)


def _dot_tn(a, b):
    return lax.dot_general(a, b, (((0,), (0,)), ((), ())), preferred_element_type=F32)


def _split_dot(x, w_bf16):
    hi = x.astype(BF16)
    lo = (x - hi.astype(F32)).astype(BF16)
    return _dot(hi, w_bf16) + _dot(lo, w_bf16)


def _head_sum(x, bd, split=True):
    one = _split_dot if split else (lambda t, w: _dot(t.astype(BF16), w))
    cols = [one(x[:, i:i + LANES], bd) for i in range(0, x.shape[1], LANES)]
    return cols[0] if len(cols) == 1 else jnp.concatenate(cols, axis=1)


def _rms(x, gain):
    return x * lax.rsqrt(jnp.mean(x * x, axis=-1, keepdims=True) + NORM_EPS) * gain


def _sigmoid(x):
    return 0.5 * jnp.tanh(0.5 * x) + 0.5


QK_W = ATTN_WIDTH + KV_WIDTH
QKV_W = QK_W + KV_WIDTH
PROJ_W = QKV_W + RWKV_COLS
HALO = 8
FRONT_SUB = 128


def _qk_rows(qk, cos, s1, s2, gains, bd):
    ms = _head_sum(qk * qk, bd, split=False) * (1.0 / HEAD_DIM)
    qk = qk * lax.rsqrt(ms + NORM_EPS) * gains
    tiles = []
    for i in range(0, qk.shape[1], LANES):
        t = qk[:, i:i + LANES]
        t = t * cos + pltpu.roll(t, LANES - ROT_DIM // 2, 1) * s1 + pltpu.roll(t, ROT_DIM // 2, 1) * s2
        tiles.append(t)
    return jnp.concatenate(tiles, axis=1)


def _slab_plan(weights, steps, step_index):
    ins, in_specs, out_specs, out_shapes, cols = [], [], [], [], []
    for w, lo, hi in weights:
        rows, width = w.shape
        assert rows % steps == 0, (rows, steps)
        n = rows // steps
        ins.append(w.reshape(steps, n, width))
        in_specs.append(pl.BlockSpec((1, n, width), lambda *g: (step_index(*g), 0, 0)))
        out_specs.append(pl.BlockSpec((1, n, hi - lo), lambda *g: (step_index(*g), 0, 0)))
        out_shapes.append(jax.ShapeDtypeStruct((steps, n, hi - lo), BF16))
        cols.append((lo, hi))
    return ins, in_specs, out_specs, out_shapes, cols


N_FRONT_IN, N_FRONT_OUT = 18, 14


def _front_kernel(cols, *refs):
    nw = len(cols)
    (x_ref, xp_ref, xn_ref, gain_ref, w_ref, cos_ref, s1_ref, s2_ref, qkg_ref, bd_ref, mu_ref, wlo_ref,
     g2_ref, w0_ref, a0_ref, kk_ref, ka_ref, rk_ref) = refs[:N_FRONT_IN]
    (q_ref, k_ref, v_ref, r_ref, vv_ref, a_ref, lw0_ref, lw1_ref, k0_ref, k1_ref, b0_ref, b1_ref, g_ref,
     bonus_ref) = refs[N_FRONT_IN + nw:N_FRONT_IN + nw + N_FRONT_OUT]
    for src_ref, dst_ref, (lo, hi) in zip(refs[N_FRONT_IN:], refs[N_FRONT_IN + nw + N_FRONT_OUT:], cols):
        dst_ref[...] = src_ref[:, :, lo:hi].astype(BF16)
    i = pl.program_id(1)
    nt = pl.num_programs(1)
    tm = x_ref.shape[1]
    sub = FRONT_SUB
    c = RWKV_WIDTH
    gain = gain_ref[...]
    bd = bd_ref[...]
    o_r, o_k, o_v, o_low = QKV_W, QKV_W + c, QKV_W + 2 * c, QKV_W + 3 * c
    halo = jnp.concatenate([xp_ref[0], xn_ref[0]], axis=0)
    w_low = w_ref[:, o_low:PROJ_W].astype(BF16)
    h_blocks = [_rms(halo, gain).astype(BF16)]
    low_blocks = [_dot(h_blocks[0], w_low)]
    for s in range(0, tm, sub):
        h_blocks.insert(-1, _rms(x_ref[0, s:s + sub, :], gain).astype(BF16))
        low_blocks.insert(-1, _dot(h_blocks[-2], w_low))
    h_ext = jnp.concatenate(h_blocks, axis=0)
    h = h_ext[:tm]
    z_low = jnp.concatenate(low_blocks, axis=0)

    row = lax.broadcasted_iota(jnp.int32, (HALO, 1), 0)
    ri = lax.broadcasted_iota(jnp.int32, (sub, sub), 0)
    ci = lax.broadcasted_iota(jnp.int32, (sub, sub), 1)
    adjacent = ((ri - ci == 1) | (ci - ri == 1)).astype(BF16)

    def project(lo, hi):
        return _dot(h_ext if lo >= QKV_W else h, w_ref[:, lo:hi].astype(BF16))

    def shift(z, lo, hi):
        mu = mu_ref[:, lo:hi]
        mix = 0.5 * mu
        keep = 1.0 - mu
        prev_row = jnp.where(i > 0, z[tm + HALO - 1:tm + HALO], 0.0)
        next_row = jnp.where(i < nt - 1, z[tm + HALO:tm + HALO + 1], 0.0)
        blocks = []
        for s in range(0, tm, sub):
            zs = z[s:s + sub]
            part = zs * keep + _dot(adjacent, zs.astype(BF16)) * mix
            before = z[s - 1:s] if s > 0 else prev_row
            after = z[s + sub:s + sub + 1] if s + sub < tm else next_row
            blocks += [part[:HALO] + jnp.where(row == 0, before * mix, 0.0), part[HALO:sub - HALO],
                       part[sub - HALO:] + jnp.where(row == HALO - 1, after * mix, 0.0)]
        return jnp.concatenate(blocks, axis=0)

    half = c // 2
    cos, s1, s2 = cos_ref[...], s1_ref[...], s2_ref[...]

    def q_rows(p, j):
        q = _qk_rows(p, cos, s1, s2, qkg_ref[:, j * half:(j + 1) * half], bd)
        q_ref[0, :, j * half:(j + 1) * half] = (q * (HEAD_DIM ** -0.5)).astype(BF16)

    z_k0 = project(o_k, o_k + half)
    z_low = shift(z_low, 3 * c, RWKV_COLS)
    lowrank = z_low[:, :LANES]
    left = lax.broadcasted_iota(jnp.int32, (1, LANES), 1) < DECAY_RANK
    lowrank = jnp.where(left, jnp.tanh(lowrank), lowrank).astype(BF16)
    gate_in = _sigmoid(z_low[:, LANES:]).astype(BF16)

    z_k1 = project(o_k + half, o_k + c)
    wa = _dot(lowrank, wlo_ref[...])
    g_ref[0] = _dot(gate_in, g2_ref[...])
    p_q0 = project(0, half)

    lw_scale = -0.5 * math.exp(-0.5)
    lw0_ref[0] = lw_scale * jnp.tanh(w0_ref[0:1, :] + wa[:, :c]) + lw_scale
    p_q1 = project(half, c)
    lw1_ref[0] = lw_scale * jnp.tanh(w0_ref[1:2, :] + wa[:, c:2 * c]) + lw_scale
    z_r0 = project(o_r, o_r + half)

    k = shift(jnp.concatenate([z_k0, z_k1], axis=1), c, 2 * c)
    kk = k * kk_ref[...]
    kk = kk * lax.rsqrt(jnp.maximum(_head_sum(kk * kk, bd, split=False), 1e-24))
    a_ref[0] = kk.astype(BF16)
    kk_half = 0.5 * kk
    z_r1 = project(o_r + half, o_r + c)

    later = [lambda: project(o_v, o_v + half), lambda: project(o_v + half, o_v + c)]
    z_v = []
    dot_rk = None
    for d, (kd_ref, bdir_ref) in enumerate(((k0_ref, b0_ref), (k1_ref, b1_ref))):
        t = jnp.tanh(a0_ref[d:d + 1, :] + wa[:, (2 + d) * c:(3 + d) * c])
        k_dir = k * (ka_ref[0:1, :] + ka_ref[1:2, :] * t)
        kd_ref[0] = k_dir.astype(BF16)
        bdir_ref[0] = (kk_half * t + kk_half).astype(BF16)
        term = k_dir * rk_ref[d:d + 1, :]
        dot_rk = term if dot_rk is None else dot_rk + term
        z_v.append(later[d]())
        q_rows((p_q0, p_q1)[d], d)

    p_kv = project(ATTN_WIDTH, QKV_W)
    r = shift(jnp.concatenate([z_r0, z_r1], axis=1), 0, c)
    r_ref[0] = r.astype(BF16)
    r_dot = _head_sum(r * dot_rk, bd, split=False)
    k_att = _qk_rows(p_kv[:, :KV_WIDTH], cos, s1, s2, qkg_ref[:, ATTN_WIDTH:], bd)
    k_ref[0] = jnp.concatenate([k_att, pltpu.roll(k_att, HEAD_DIM, 1)], axis=1).astype(BF16)
    v = shift(jnp.concatenate(z_v, axis=1), 2 * c, 3 * c)
    vv_ref[0] = v.astype(BF16)
    bonus_ref[0] = r_dot * v
    v_att = p_kv[:, KV_WIDTH:]
    v_ref[0] = jnp.concatenate([v_att, pltpu.roll(v_att, HEAD_DIM, 1)], axis=1).astype(BF16)


def _front(x, gain, w, cos, s1, s2, qkg, bd, mu, wlo, g2, w0, a0, k_k, k_a, r_k, tm, weights):
    b, s, d = x.shape
    nt = s // tm
    w_ins, w_in_specs, w_out_specs, w_out_shapes, cols = _slab_plan(weights, b * nt, lambda bi, i: bi * nt + i)
    c = RWKV_WIDTH
    nh = tm // HALO
    const = lambda shape: pl.BlockSpec(shape, lambda bi, i: (0,) * len(shape))
    tok = lambda width: pl.BlockSpec((1, tm, width), lambda bi, i: (bi, i, 0))
    tab = pl.BlockSpec((tm, LANES), lambda bi, i: (i, 0))
    halo_p = pl.BlockSpec((1, HALO, d), lambda bi, i: (bi, jnp.maximum(i * nh - 1, 0), 0))
    halo_n = pl.BlockSpec((1, HALO, d), lambda bi, i: (bi, jnp.minimum((i + 1) * nh, s // HALO - 1), 0))
    rwkv_dtypes = (BF16,) * 3 + (F32,) * 2 + (BF16,) * 4 + (F32,) * 2
    outs = pl.pallas_call(
        functools.partial(_front_kernel, cols),
        grid=(b, nt),
        in_specs=[tok(d), halo_p, halo_n, const((1, d)), const((d, PROJ_W)), tab, tab, tab, const((1, QK_W)),
                  const((LANES, LANES)), const((1, RWKV_COLS)), const((LANES, 4 * c)), const((GATE_RANK, c)),
                  const((2, c)), const((2, c)), const((1, c)), const((2, c)), const((2, c))] + w_in_specs,
        out_specs=[tok(ATTN_WIDTH), tok(2 * KV_WIDTH), tok(2 * KV_WIDTH)] + [tok(c)] * 11 + w_out_specs,
        out_shape=[jax.ShapeDtypeStruct((b, s, ATTN_WIDTH), BF16),
                   jax.ShapeDtypeStruct((b, s, 2 * KV_WIDTH), BF16),
                   jax.ShapeDtypeStruct((b, s, 2 * KV_WIDTH), BF16)]
                  + [jax.ShapeDtypeStruct((b, s, c), dt) for dt in rwkv_dtypes] + w_out_shapes,
        compiler_params=_cparams(("parallel", "parallel")),
        name="front",
    )(x, x, x, gain, w, cos, s1, s2, qkg, bd, mu, wlo, g2, w0, a0, k_k, k_a, r_k, *w_ins)
    converted = [o.reshape(wt.shape[0], hi - lo) for o, (wt, lo, hi) in zip(outs[N_FRONT_OUT:], weights)]
    return outs[:N_FRONT_OUT], converted


def _scan_kernel(r0_ref, v0_ref, a0_ref, lw0_ref, k0_ref, b0_ref,
                 r1_ref, v1_ref, a1_ref, lw1_ref, k1_ref, b1_ref,
                 y0_ref, y1_ref, ht_ref):
    L, gw, nch = CHUNK, GROUP_W, SCAN_CHUNKS
    rep = gw // L
    ng = RWKV_WIDTH // gw
    rows = nch * L

    @pl.when(pl.program_id(1) == 0)
    def _():
        ht_ref[...] = jnp.zeros_like(ht_ref)

    tau = lax.broadcasted_iota(jnp.int32, (L, gw), 0)
    sig = lax.broadcasted_iota(jnp.int32, (L, gw), 1) % L
    bd = (lax.broadcasted_iota(jnp.int32, (gw, gw), 0) // L) == (lax.broadcasted_iota(jnp.int32, (gw, gw), 1) // L)
    half = 2
    ngrp = nch // half
    hrows = half * L
    ti = lax.broadcasted_iota(jnp.int32, (hrows, hrows), 0)
    si = lax.broadcasted_iota(jnp.int32, (hrows, hrows), 1)
    same_chunk = (ti // L) == (si // L)
    tris = ((same_chunk & (si <= ti)).astype(BF16), (same_chunk & (si >= ti)).astype(BF16))
    masks = ((sig < tau, sig <= tau), (sig > tau, sig >= tau))

    def bdtile(x):
        xb = x.astype(BF16)
        return jnp.where(bd, jnp.concatenate([xb] * rep, axis=0), jnp.zeros((), BF16))

    dirs = ((r0_ref, v0_ref, a0_ref, lw0_ref, k0_ref, b0_ref),
            (r1_ref, v1_ref, a1_ref, lw1_ref, k1_ref, b1_ref))

    def build(grp):
        out = {}
        for d, (r_ref, v_ref, a_ref, lw_ref, k_ref, b_ref) in enumerate(dirs):
            base = (grp if d == 0 else ngrp - 1 - grp) * hrows
            rsl = slice(base, base + hrows)
            strict, incl = masks[d]
            lw = lw_ref[0, rsl, :]
            cl = _split_dot_left(tris[d], lw)
            e_in = jnp.exp(cl)
            e_inv = jnp.exp(-cl)
            e_ex = jnp.exp(cl - lw)
            a_t = -(a_ref[0, rsl, :].astype(F32) * e_ex)
            r_t = r_ref[0, rsl, :].astype(F32) * e_in
            b_t = b_ref[0, rsl, :].astype(F32) * e_inv
            k_t = k_ref[0, rsl, :].astype(F32) * e_inv
            v_all = v_ref[0, rsl, :]
            for jj in range(half):
                rs = slice(jj * L, (jj + 1) * L)
                last = (jj + 1) * L - 1 if d == 0 else jj * L
                d_end = jnp.exp(cl[last:last + 1, :])
                b_h = b_t[rs] * d_end
                k_h = k_t[rs] * d_end
                for g in range(ng):
                    sl = slice(g * gw, (g + 1) * gw)
                    out[(base // L + jj, d, g)] = dict(
                        strict=strict, incl=incl, at=a_t[rs, sl], rt=r_t[rs, sl], vv=v_all[rs, sl], bt=b_t[rs, sl],
                        kt=k_t[rs, sl], bh=b_h[:, sl], kh=k_h[:, sl], dend=d_end[:, sl])
        return out

    def s_gram(par):
        for c in par:
            lhs = jnp.concatenate([c["at"], c["rt"]], axis=0).astype(BF16)
            gram = _dot_nt(lhs, jnp.concatenate([bdtile(c["bt"]), bdtile(c["kt"])], axis=0))
            c["gb"], c["gk"] = gram[:, :gw], gram[:, gw:]

    def s_vprod(par):
        for c in par:
            c["m_rb"] = jnp.where(c["incl"], c["gb"][L:], 0.0).astype(BF16)
            c["p"] = jnp.where(c["strict"], c["gb"][:L], 0.0)
            m_k = jnp.concatenate([jnp.where(c["strict"], c["gk"][:L], 0.0), jnp.where(c["incl"], c["gk"][L:], 0.0)],
                                  axis=0)
            res = _dot(m_k.astype(BF16), bdtile(c["vv"]))
            c["makv"], c["mrkv"] = res[:L], res[L:]

    levels = int(math.log2(L))

    def s_level(lvl):
        def run(par):
            for c in par:
                ops = ([] if lvl == 0 else [bdtile(c["t"])]) + ([] if lvl == levels - 1 else [bdtile(c["p"])])
                c["res"] = _dot(c["p"].astype(BF16), ops[0] if len(ops) == 1 else jnp.concatenate(ops, axis=1))
            for c in par:
                if lvl == 0:
                    c["t"] = jnp.where(sig == tau, 1.0, c["p"])
                    c["p"] = c["res"]
                else:
                    c["t"] = c["t"] + c["res"][:, :gw]
                    if lvl < levels - 1:
                        c["p"] = c["res"][:, gw:]
        return run

    def s_apply(par):
        for c in par:
            res = _dot(c["t"].astype(BF16), jnp.concatenate([bdtile(c["at"]), bdtile(c["makv"])], axis=1))
            c["ah"], c["uloc"] = res[:, :gw], res[:, gw:]

    def s_g(par):
        for c in par:
            c["bkh"] = jnp.concatenate([c["bh"], c["kh"]], axis=0).astype(BF16)
            c["g"] = jnp.where(bd, _dot_tn(c["ah"].astype(BF16), c["bkh"][:L]), 0.0).astype(BF16)

    def s_c(par):
        for c in par:
            c["cst"] = jnp.where(bd, _dot_tn(jnp.concatenate([c["uloc"].astype(BF16), c["vv"]], axis=0), c["bkh"]),
                                 0.0)

    def s_q(par):
        for c in par:
            res = _dot(c["m_rb"], jnp.concatenate([bdtile(c["ah"]), bdtile(c["uloc"])], axis=1))
            c["qh"] = (c["rt"] + res[:, :gw]).astype(BF16)
            c["yloc"] = res[:, gw:] + c["mrkv"]

    stages = [s_gram, s_vprod] + [s_level(lvl) for lvl in range(levels)] + [s_apply, s_g, s_c, s_q]

    state = {(d, g): ht_ref[d, g] for d in range(2) for g in range(ng)}
    recs = {}

    def seq_step(step):
        cur = [(d, g, recs[(step if d == 0 else nch - 1 - step, d, g)]) for d in range(2) for g in range(ng)]
        for d, g, c in cur:
            c["htb"] = state[(d, g)].astype(BF16)
        for d, g, c in cur:
            state[(d, g)] = state[(d, g)] * c["dend"] + c["cst"] + _dot(c["htb"], c["g"])
        for d, g, c in cur:
            c["y"] = c["yloc"] + _dot_nt(c["qh"], c["htb"])

    slots = {(i + 1) * len(stages) // (half + 1): i for i in range(half)}
    groups = [build(0)]
    recs.update(groups[0])
    for k in range(ngrp):
        for s, fn in enumerate(stages):
            if s == 2 and k + 1 < ngrp:
                groups.append(build(k + 1))
                recs.update(groups[k + 1])
            fn(list(groups[k].values()))
            if k > 0 and s in slots:
                seq_step((k - 1) * half + slots[s])
    for step in range(nch - half, nch):
        seq_step(step)
    for (d, g), h in state.items():
        ht_ref[d, g] = h
    for d, y_ref in enumerate((y0_ref, y1_ref)):
        y_ref[0] = jnp.concatenate(
            [jnp.concatenate([recs[(j, d, g)]["y"] for g in range(ng)], axis=1) for j in range(nch)], axis=0)


def _split_dot_left(w_bf16, x):
    hi = x.astype(BF16)
    lo = (x - hi.astype(F32)).astype(BF16)
    return _dot(w_bf16, hi) + _dot(w_bf16, lo)


def _scan(r, v, a, lw0, lw1, k0, k1, b0, b1):
    b, s, c = r.shape
    rows = CHUNK * SCAN_CHUNKS
    nblk = s // rows
    fwd = pl.BlockSpec((1, rows, c), lambda bi, i: (bi, i, 0))
    bwd = pl.BlockSpec((1, rows, c), lambda bi, i: (bi, nblk - 1 - i, 0))
    ng = c // GROUP_W
    return pl.pallas_call(
        _scan_kernel,
        grid=(b, nblk),
        in_specs=[fwd] * 6 + [bwd] * 6,
        out_specs=[fwd, bwd],
        out_shape=[jax.ShapeDtypeStruct((b, s, c), F32)] * 2,
        scratch_shapes=[pltpu.VMEM((2, ng, GROUP_W, GROUP_W), F32)],
        compiler_params=_cparams(("parallel", "arbitrary")),
        name="scan",
    )(r, v, a, lw0, k0, b0, r, v, a, lw1, k1, b1)


MID_BLOCKS = 8


def _mid_kernel(sink_ref, q_ref, kp_ref, kc_ref, kn_ref, vp_ref, vc_ref, vn_ref, x_ref, gain_ref, wg_ref, y0_ref,
                y1_ref, bonus_ref, g_ref, lnw_ref, lnb_ref, bd_ref, wua_ref, wur_ref, wout_ref, o_ref):
    n = pl.program_id(1)
    nb = pl.num_programs(1)
    blk, nsub = ATTN_BLOCK, MID_BLOCKS
    d = x_ref.shape[2]
    k = jnp.concatenate([kp_ref[0], kc_ref[0], kn_ref[0]], axis=0)
    v = jnp.concatenate([vp_ref[0], vc_ref[0], vn_ref[0]], axis=0)
    qi = lax.broadcasted_iota(jnp.int32, (2 * blk, blk), 0) % blk
    ki = lax.broadcasted_iota(jnp.int32, (2 * blk, blk), 1)
    first = lax.broadcasted_iota(jnp.int32, (2 * blk, 1), 0) < blk
    left = lax.broadcasted_iota(jnp.int32, (1, LANES), 1) < HEAD_DIM
    zero = jnp.zeros((), BF16)
    bd = bd_ref[...]

    def scores(j):
        rows = slice(j * blk, (j + 1) * blk)
        win = slice(j * blk, (j + 3) * blk)
        mask_prev = (ki >= qi) & ((n > 0) if j == 0 else True)
        mask_next = (ki <= qi) & ((n < nb - 1) if j == nsub - 1 else True)
        recs = []
        for g in range(KV_HEADS):
            pairs = jnp.concatenate([q_ref[0, rows, p * LANES:(p + 1) * LANES] for p in (2 * g, 2 * g + 1)], axis=0)
            for side in range(2):
                off = LANES * ((g + side) % 2)
                keep = left if side == 0 else jnp.logical_not(left)
                sk = jnp.where(first, sink_ref[4 * g + side], sink_ref[4 * g + 2 + side])
                sc = _dot_nt(jnp.where(keep, pairs, zero), k[win, off:off + LANES])
                sc = jnp.concatenate([jnp.where(mask_prev, sc[:, :blk], -1e30), sc[:, blk:2 * blk],
                                      jnp.where(mask_next, sc[:, 2 * blk:], -1e30)], axis=1)
                recs.append(dict(sc=sc, sk=sk, v=jnp.where(keep, v[win, off:off + LANES], zero)))
        return recs

    def softmax(recs):
        for c in recs:
            m = jnp.maximum(jnp.max(c["sc"], axis=-1, keepdims=True), c["sk"])
            e = jnp.exp(c["sc"] - m)
            c["den"] = jnp.sum(e, axis=-1, keepdims=True) + jnp.exp(c["sk"] - m)
            c["e"] = e.astype(BF16)

    def weighted(recs):
        outs = []
        for g in range(KV_HEADS):
            both = sum(_dot(c["e"], c["v"]) / c["den"] for c in recs[2 * g:2 * g + 2])
            outs += [both[:blk], both[blk:]]
        return jnp.concatenate(outs, axis=1).astype(BF16)

    quarter = wg_ref.shape[1] // 4
    half = d // 2
    group = 4
    for j0 in range(0, nsub, group):
        rows = slice(j0 * blk, (j0 + group) * blk)
        x = x_ref[0, rows, :]
        blocks = [scores(j0), scores(j0 + 1)]
        h = _rms(x, gain_ref[...]).astype(BF16)
        gate_pre = [_dot(h, wg_ref[:, :quarter])]
        softmax(blocks[0])
        gate_pre.append(_dot(h, wg_ref[:, quarter:2 * quarter]))
        softmax(blocks[1])
        attn = [weighted(blocks[0])]
        blocks.append(scores(j0 + 2))
        gates = [_sigmoid(gate_pre[0])]
        gate_pre.append(_dot(h, wg_ref[:, 2 * quarter:3 * quarter]))
        attn.append(weighted(blocks[1]))
        blocks.append(scores(j0 + 3))
        softmax(blocks[2])
        gate_pre.append(_dot(h, wg_ref[:, 3 * quarter:]))
        gates.append(_sigmoid(gate_pre[1]))
        softmax(blocks[3])
        attn.append(weighted(blocks[2]))

        y = y0_ref[0, rows, :] + y1_ref[0, rows, :]
        yc = y - _head_sum(y, bd) * (1.0 / HEAD_DIM)
        var = _head_sum(yc * yc, bd, split=False) * (1.0 / HEAD_DIM)
        gates.append(_sigmoid(gate_pre[2]))
        attn.append(weighted(blocks[3]))
        yn = yc * lax.rsqrt(var + GN_EPS) * lnw_ref[...] + lnb_ref[...]
        rw = ((yn + bonus_ref[0, rows, :]) * g_ref[0, rows, :]).astype(BF16)
        up_attn = _dot(jnp.concatenate(attn, axis=0), wua_ref[...])
        gates.append(_sigmoid(gate_pre[3]))
        up_rwkv = _dot(rw, wur_ref[...])
        merged = (jnp.concatenate(gates[:2], axis=1) * up_attn
                  + jnp.concatenate(gates[2:], axis=1) * up_rwkv).astype(BF16)
        for col in range(0, d, half):
            o_ref[0, rows, col:col + half] = x[:, col:col + half] + _dot(merged, wout_ref[:, col:col + half])


def _mid(sink, q, k2, v2, x, gain, wg, y0, y1, bonus, g, lnw, lnb, bd, wua, wur, wout):
    b, s, d = x.shape
    c = RWKV_WIDTH
    tm = ATTN_BLOCK * MID_BLOCKS
    nb = s // tm
    last = s // ATTN_BLOCK - 1
    const = lambda shape: pl.BlockSpec(shape, lambda bi, n: (0,) * len(shape), pipeline_mode=pl.Buffered(1))
    tok = lambda width: pl.BlockSpec((1, tm, width), lambda bi, n: (bi, n, 0))
    prev = pl.BlockSpec((1, ATTN_BLOCK, 2 * KV_WIDTH), lambda bi, n: (bi, jnp.maximum(n * MID_BLOCKS - 1, 0), 0))
    nxt = pl.BlockSpec((1, ATTN_BLOCK, 2 * KV_WIDTH), lambda bi, n: (bi, jnp.minimum((n + 1) * MID_BLOCKS, last), 0))
    kv = [prev, tok(2 * KV_WIDTH), nxt]
    return pl.pallas_call(
        _mid_kernel,
        grid=(b, nb),
        in_specs=[pl.BlockSpec(memory_space=pltpu.SMEM), tok(ATTN_WIDTH)] + kv + kv
                 + [tok(d), const((1, d)), const((d, 2 * d)), tok(c), tok(c), tok(c), tok(c), const((1, c)),
                    const((1, c)), const((LANES, LANES)), const((ATTN_WIDTH, d)), const((c, d)), const((d, d))],
        out_specs=tok(d),
        out_shape=jax.ShapeDtypeStruct((b, s, d), F32),
        compiler_params=_cparams(("parallel", "parallel")),
        name="mid",
    )(sink, q, k2, k2, k2, v2, v2, v2, x, gain, wg, y0, y1, bonus, g, lnw, lnb, bd, wua, wur, wout)


FF_SLAB = 1024


def _ffn_kernel(x_ref, p_ref, gf_ref, w1_ref, w2_ref, gp_ref, wpg_ref, wple_ref, o_ref):
    x = x_ref[0]
    h = _rms(x, gf_ref[...]).astype(BF16)
    acc = x
    for j in range(0, w1_ref.shape[1], FF_SLAB):
        hid = jnp.maximum(_dot(h, w1_ref[:, j:j + FF_SLAB]), 0.0)
        acc = acc + _dot((hid * hid).astype(BF16), w2_ref[j:j + FF_SLAB, :])
    hp = _rms(acc, gp_ref[...]).astype(BF16)
    o_ref[0] = acc + _dot(p_ref[0].astype(BF16), wple_ref[...]) * _sigmoid(_dot(hp, wpg_ref[...]))


def _ffn(x, p, gf, w1, w2, gp, wpg, wple, tm):
    b, s, d = x.shape
    dff = w1.shape[1]
    pd = p.shape[-1]
    const = lambda shape: pl.BlockSpec(shape, lambda bi, i: (0,) * len(shape), pipeline_mode=pl.Buffered(1))
    tok = lambda width: pl.BlockSpec((1, tm, width), lambda bi, i: (bi, i, 0))
    return pl.pallas_call(
        _ffn_kernel,
        grid=(b, s // tm),
        in_specs=[tok(d), tok(pd), const((1, d)), const((d, dff)), const((dff, d)), const((1, d)),
                  const((d, d)), const((pd, d))],
        out_specs=tok(d),
        out_shape=jax.ShapeDtypeStruct((b, s, d), F32),
        compiler_params=_cparams(("parallel", "parallel")),
        name="ffn",
    )(x, p, gf, w1, w2, gp, wpg, wple)


def _rotary_tables(s):
    half = ROT_DIM // 2
    inv_freq = jnp.power(jnp.float32(ROPE_THETA), -jnp.arange(half, dtype=F32) * 2.0 / ROT_DIM)
    ang = jnp.arange(s).astype(F32)[:, None] * inv_freq[None, :]
    cos, sin = jnp.cos(ang), jnp.sin(ang)
    pad = jnp.zeros((s, HEAD_DIM - ROT_DIM), F32)
    zero = jnp.zeros((s, half), F32)
    c = jnp.concatenate([cos, cos, pad + 1.0], axis=1)
    s1 = jnp.concatenate([-sin, zero, pad], axis=1)
    s2 = jnp.concatenate([zero, sin, pad], axis=1)
    rep = LANES // HEAD_DIM
    return jnp.tile(c, (1, rep)), jnp.tile(s1, (1, rep)), jnp.tile(s2, (1, rep))


def kernel(x, p, norm_mix, w_in, shift_mu, q_norm, k_norm, sink, w0, w2, a0, a2, g2, k_k, k_a, r_k, lnx_w, lnx_b,
           w_up_attn, w_up_rwkv, w_out, norm_ffn, w_ff1, w_ff2, norm_ple, w_ple_gate, w_ple):
    bsz, s, d = x.shape
    depth = w_in.shape[0]
    c = RWKV_WIDTH
    tm = min(512, s)
    cos, s1, s2 = _rotary_tables(s)
    lane = jnp.arange(LANES)
    bd = ((lane[:, None] // HEAD_DIM) == (lane[None, :] // HEAD_DIM)).astype(BF16)
    o_g = PROJ_W
    for i in range(depth):
        wi = w_in[i]
        qkg = jnp.concatenate([jnp.tile(q_norm[i], Q_HEADS), jnp.tile(k_norm[i], KV_HEADS)])[None, :]
        zpad = jnp.zeros((DECAY_RANK, c), F32)
        wlo = jnp.concatenate([jnp.concatenate([w2[i, 0], zpad], axis=0), jnp.concatenate([w2[i, 1], zpad], axis=0),
                               jnp.concatenate([zpad, a2[i, 0]], axis=0), jnp.concatenate([zpad, a2[i, 1]], axis=0)],
                              axis=1)
        wlo = (0.5 * wlo).astype(BF16)
        k_mix = jnp.stack([1.0 - 0.5 * k_a[i], 0.5 * k_a[i]])
        full = lambda w: (w, 0, w.shape[1])
        ((q, k2, v2, r, v, a, lw0, lw1, k0, k1, b0, b1, g, bonus), (wg, wua, wur, wo, wf1, wf2, wpg, wpl)) = _front(
            x, norm_mix[i][None, :], wi, cos, s1, s2, qkg, bd, shift_mu[i][None, :], wlo, g2[i].astype(BF16),
            0.5 * w0[i], 0.5 * a0[i], k_k[i][None, :], k_mix, r_k[i].reshape(2, c), tm,
            [(wi, o_g, wi.shape[1]), full(w_up_attn[i]), full(w_up_rwkv[i]), full(w_out[i]), full(w_ff1[i]),
             full(w_ff2[i]), full(w_ple_gate[i]), full(w_ple[i])])
        y0, y1 = _scan(r, v, a, lw0, lw1, k0, k1, b0, b1)
        x = _mid(sink[i], q, k2, v2, x, norm_mix[i][None, :], wg, y0, y1, bonus, g, lnx_w[i][None, :],
                 lnx_b[i][None, :], bd, wua, wur, wo)
        x = _ffn(x, p[i], norm_ffn[i][None, :], wf1, wf2, norm_ple[i][None, :], wpg, wpl, min(1024, s))
    return x
```

```python
import functools
import math

import jax
import jax.numpy as jnp
from jax import lax
from jax.experimental import pallas as pl
from jax.experimental.pallas import tpu as pltpu

F32 = jnp.float32
BF16 = jnp.bfloat16

LANES = 128
HEAD_DIM = 64
Q_HEADS = 8
KV_HEADS = 2
ATTN_WIDTH = Q_HEADS * HEAD_DIM
KV_WIDTH = KV_HEADS * HEAD_DIM
WINDOW = 128
ATTN_BLOCK = 128
ROPE_THETA = 500000.0
ROT_DIM = HEAD_DIM // 4
RWKV_WIDTH = 512
DECAY_RANK = 64
ICLR_RANK = 64
GATE_RANK = 128
RWKV_COLS = 3 * RWKV_WIDTH + DECAY_RANK + ICLR_RANK + GATE_RANK
NORM_EPS = 1e-6
GN_EPS = 64e-5
CHUNK = 64
GROUP_W = 128
SCAN_CHUNKS = 8
VMEM_LIMIT = 56 * 1024 * 1024


def _cparams(sem):
    return pltpu.CompilerParams(dimension_semantics=sem, vmem_limit_bytes=VMEM_LIMIT)


def _dot(a, b):
    return jnp.dot(a, b, preferred_element_type=F32)


def _dot_nt(a, b):
    return lax.dot_general(a, b, (((1,), (1,)), ((), ())), preferred_element_type=F32)


def _dot_tn(a, b):
    return lax.dot_general(a, b, (((0,), (0,)), ((), ())), preferred_element_type=F32)


def _split_dot(x, w_bf16):
    hi = x.astype(BF16)
    lo = (x - hi.astype(F32)).astype(BF16)
    return _dot(hi, w_bf16) + _dot(lo, w_bf16)


def _head_sum(x, bd, split=True):
    one = _split_dot if split else (lambda t, w: _dot(t.astype(BF16), w))
    cols = [one(x[:, i:i + LANES], bd) for i in range(0, x.shape[1], LANES)]
    return cols[0] if len(cols) == 1 else jnp.concatenate(cols, axis=1)


def _rms(x, gain):
    return x * lax.rsqrt(jnp.mean(x * x, axis=-1, keepdims=True) + NORM_EPS) * gain


def _sigmoid(x):
    return 0.5 * jnp.tanh(0.5 * x) + 0.5


QK_W = ATTN_WIDTH + KV_WIDTH
QKV_W = QK_W + KV_WIDTH
PROJ_W = QKV_W + RWKV_COLS
HALO = 8
FRONT_SUB = 128


def _qk_rows(qk, cos, s1, s2, gains, bd):
    ms = _head_sum(qk * qk, bd, split=False) * (1.0 / HEAD_DIM)
    qk = qk * lax.rsqrt(ms + NORM_EPS) * gains
    tiles = []
    for i in range(0, qk.shape[1], LANES):
        t = qk[:, i:i + LANES]
        t = t * cos + pltpu.roll(t, LANES - ROT_DIM // 2, 1) * s1 + pltpu.roll(t, ROT_DIM // 2, 1) * s2
        tiles.append(t)
    return jnp.concatenate(tiles, axis=1)


def _slab_plan(weights, steps, step_index):
    ins, in_specs, out_specs, out_shapes, cols = [], [], [], [], []
    for w, lo, hi in weights:
        rows, width = w.shape
        assert rows % steps == 0, (rows, steps)
        n = rows // steps
        ins.append(w.reshape(steps, n, width))
        in_specs.append(pl.BlockSpec((1, n, width), lambda *g: (step_index(*g), 0, 0)))
        out_specs.append(pl.BlockSpec((1, n, hi - lo), lambda *g: (step_index(*g), 0, 0)))
        out_shapes.append(jax.ShapeDtypeStruct((steps, n, hi - lo), BF16))
        cols.append((lo, hi))
    return ins, in_specs, out_specs, out_shapes, cols


N_FRONT_IN, N_FRONT_OUT = 18, 14


def _front_kernel(cols, *refs):
    nw = len(cols)
    (x_ref, xp_ref, xn_ref, gain_ref, w_ref, cos_ref, s1_ref, s2_ref, qkg_ref, bd_ref, mu_ref, wlo_ref,
     g2_ref, w0_ref, a0_ref, kk_ref, ka_ref, rk_ref) = refs[:N_FRONT_IN]
    (q_ref, k_ref, v_ref, r_ref, vv_ref, a_ref, lw0_ref, lw1_ref, k0_ref, k1_ref, b0_ref, b1_ref, g_ref,
     bonus_ref) = refs[N_FRONT_IN + nw:N_FRONT_IN + nw + N_FRONT_OUT]
    for src_ref, dst_ref, (lo, hi) in zip(refs[N_FRONT_IN:], refs[N_FRONT_IN + nw + N_FRONT_OUT:], cols):
        dst_ref[...] = src_ref[:, :, lo:hi].astype(BF16)
    i = pl.program_id(1)
    nt = pl.num_programs(1)
    tm = x_ref.shape[1]
    sub = FRONT_SUB
    c = RWKV_WIDTH
    gain = gain_ref[...]
    bd = bd_ref[...]
    o_r, o_k, o_v, o_low = QKV_W, QKV_W + c, QKV_W + 2 * c, QKV_W + 3 * c
    halo = jnp.concatenate([xp_ref[0], xn_ref[0]], axis=0)
    w_low = w_ref[:, o_low:PROJ_W].astype(BF16)
    h_blocks = [_rms(halo, gain).astype(BF16)]
    low_blocks = [_dot(h_blocks[0], w_low)]
    for s in range(0, tm, sub):
        h_blocks.insert(-1, _rms(x_ref[0, s:s + sub, :], gain).astype(BF16))
        low_blocks.insert(-1, _dot(h_blocks[-2], w_low))
    h_ext = jnp.concatenate(h_blocks, axis=0)
    h = h_ext[:tm]
    z_low = jnp.concatenate(low_blocks, axis=0)

    row = lax.broadcasted_iota(jnp.int32, (HALO, 1), 0)
    ri = lax.broadcasted_iota(jnp.int32, (sub, sub), 0)
    ci = lax.broadcasted_iota(jnp.int32, (sub, sub), 1)
    adjacent = ((ri - ci == 1) | (ci - ri == 1)).astype(BF16)

    def project(lo, hi):
        return _dot(h_ext if lo >= QKV_W else h, w_ref[:, lo:hi].astype(BF16))

    def shift(z, lo, hi):
        mu = mu_ref[:, lo:hi]
        mix = 0.5 * mu
        keep = 1.0 - mu
        prev_row = jnp.where(i > 0, z[tm + HALO - 1:tm + HALO], 0.0)
        next_row = jnp.where(i < nt - 1, z[tm + HALO:tm + HALO + 1], 0.0)
        blocks = []
        for s in range(0, tm, sub):
            zs = z[s:s + sub]
            part = zs * keep + _dot(adjacent, zs.astype(BF16)) * mix
            before = z[s - 1:s] if s > 0 else prev_row
            after = z[s + sub:s + sub + 1] if s + sub < tm else next_row
            blocks += [part[:HALO] + jnp.where(row == 0, before * mix, 0.0), part[HALO:sub - HALO],
                       part[sub - HALO:] + jnp.where(row == HALO - 1, after * mix, 0.0)]
        return jnp.concatenate(blocks, axis=0)

    half = c // 2
    cos, s1, s2 = cos_ref[...], s1_ref[...], s2_ref[...]

    def q_rows(p, j):
        q = _qk_rows(p, cos, s1, s2, qkg_ref[:, j * half:(j + 1) * half], bd)
        q_ref[0, :, j * half:(j + 1) * half] = (q * (HEAD_DIM ** -0.5)).astype(BF16)

    z_k0 = project(o_k, o_k + half)
    z_low = shift(z_low, 3 * c, RWKV_COLS)
    lowrank = z_low[:, :LANES]
    left = lax.broadcasted_iota(jnp.int32, (1, LANES), 1) < DECAY_RANK
    lowrank = jnp.where(left, jnp.tanh(lowrank), lowrank).astype(BF16)
    gate_in = _sigmoid(z_low[:, LANES:]).astype(BF16)

    z_k1 = project(o_k + half, o_k + c)
    wa = _dot(lowrank, wlo_ref[...])
    g_ref[0] = _dot(gate_in, g2_ref[...])
    p_q0 = project(0, half)

    lw_scale = -0.5 * math.exp(-0.5)
    lw0_ref[0] = lw_scale * jnp.tanh(w0_ref[0:1, :] + wa[:, :c]) + lw_scale
    p_q1 = project(half, c)
    lw1_ref[0] = lw_scale * jnp.tanh(w0_ref[1:2, :] + wa[:, c:2 * c]) + lw_scale
    z_r0 = project(o_r, o_r + half)

    k = shift(jnp.concatenate([z_k0, z_k1], axis=1), c, 2 * c)
    kk = k * kk_ref[...]
    kk = kk * lax.rsqrt(jnp.maximum(_head_sum(kk * kk, bd, split=False), 1e-24))
    a_ref[0] = kk.astype(BF16)
    kk_half = 0.5 * kk
    z_r1 = project(o_r + half, o_r + c)

    later = [lambda: project(o_v, o_v + half), lambda: project(o_v + half, o_v + c)]
    z_v = []
    dot_rk = None
    for d, (kd_ref, bdir_ref) in enumerate(((k0_ref, b0_ref), (k1_ref, b1_ref))):
        t = jnp.tanh(a0_ref[d:d + 1, :] + wa[:, (2 + d) * c:(3 + d) * c])
        k_dir = k * (ka_ref[0:1, :] + ka_ref[1:2, :] * t)
        kd_ref[0] = k_dir.astype(BF16)
        bdir_ref[0] = (kk_half * t + kk_half).astype(BF16)
        term = k_dir * rk_ref[d:d + 1, :]
        dot_rk = term if dot_rk is None else dot_rk + term
        z_v.append(later[d]())
        q_rows((p_q0, p_q1)[d], d)

    p_kv = project(ATTN_WIDTH, QKV_W)
    r = shift(jnp.concatenate([z_r0, z_r1], axis=1), 0, c)
    r_ref[0] = r.astype(BF16)
    r_dot = _head_sum(r * dot_rk, bd, split=False)
    k_att = _qk_rows(p_kv[:, :KV_WIDTH], cos, s1, s2, qkg_ref[:, ATTN_WIDTH:], bd)
    k_ref[0] = jnp.concatenate([k_att, pltpu.roll(k_att, HEAD_DIM, 1)], axis=1).astype(BF16)
    v = shift(jnp.concatenate(z_v, axis=1), 2 * c, 3 * c)
    vv_ref[0] = v.astype(BF16)
    bonus_ref[0] = r_dot * v
    v_att = p_kv[:, KV_WIDTH:]
    v_ref[0] = jnp.concatenate([v_att, pltpu.roll(v_att, HEAD_DIM, 1)], axis=1).astype(BF16)


def _front(x, gain, w, cos, s1, s2, qkg, bd, mu, wlo, g2, w0, a0, k_k, k_a, r_k, tm, weights):
    b, s, d = x.shape
    nt = s // tm
    w_ins, w_in_specs, w_out_specs, w_out_shapes, cols = _slab_plan(weights, b * nt, lambda bi, i: bi * nt + i)
    c = RWKV_WIDTH
    nh = tm // HALO
    const = lambda shape: pl.BlockSpec(shape, lambda bi, i: (0,) * len(shape))
    tok = lambda width: pl.BlockSpec((1, tm, width), lambda bi, i: (bi, i, 0))
    tab = pl.BlockSpec((tm, LANES), lambda bi, i: (i, 0))
    halo_p = pl.BlockSpec((1, HALO, d), lambda bi, i: (bi, jnp.maximum(i * nh - 1, 0), 0))
    halo_n = pl.BlockSpec((1, HALO, d), lambda bi, i: (bi, jnp.minimum((i + 1) * nh, s // HALO - 1), 0))
    rwkv_dtypes = (BF16,) * 3 + (F32,) * 2 + (BF16,) * 4 + (F32,) * 2
    outs = pl.pallas_call(
        functools.partial(_front_kernel, cols),
        grid=(b, nt),
        in_specs=[tok(d), halo_p, halo_n, const((1, d)), const((d, PROJ_W)), tab, tab, tab, const((1, QK_W)),
                  const((LANES, LANES)), const((1, RWKV_COLS)), const((LANES, 4 * c)), const((GATE_RANK, c)),
                  const((2, c)), const((2, c)), const((1, c)), const((2, c)), const((2, c))] + w_in_specs,
        out_specs=[tok(ATTN_WIDTH), tok(2 * KV_WIDTH), tok(2 * KV_WIDTH)] + [tok(c)] * 11 + w_out_specs,
        out_shape=[jax.ShapeDtypeStruct((b, s, ATTN_WIDTH), BF16),
                   jax.ShapeDtypeStruct((b, s, 2 * KV_WIDTH), BF16),
                   jax.ShapeDtypeStruct((b, s, 2 * KV_WIDTH), BF16)]
                  + [jax.ShapeDtypeStruct((b, s, c), dt) for dt in rwkv_dtypes] + w_out_shapes,
        compiler_params=_cparams(("parallel", "parallel")),
        name="front",
    )(x, x, x, gain, w, cos, s1, s2, qkg, bd, mu, wlo, g2, w0, a0, k_k, k_a, r_k, *w_ins)
    converted = [o.reshape(wt.shape[0], hi - lo) for o, (wt, lo, hi) in zip(outs[N_FRONT_OUT:], weights)]
    return outs[:N_FRONT_OUT], converted


def _scan_kernel(r0_ref, v0_ref, a0_ref, lw0_ref, k0_ref, b0_ref,
                 r1_ref, v1_ref, a1_ref, lw1_ref, k1_ref, b1_ref,
                 y0_ref, y1_ref, ht_ref):
    L, gw, nch = CHUNK, GROUP_W, SCAN_CHUNKS
    rep = gw // L
    ng = RWKV_WIDTH // gw

    @pl.when(pl.program_id(1) == 0)
    def _():
        ht_ref[...] = jnp.zeros_like(ht_ref)

    tau = lax.broadcasted_iota(jnp.int32, (L, gw), 0)
    sig = lax.broadcasted_iota(jnp.int32, (L, gw), 1) % L
    bd = (lax.broadcasted_iota(jnp.int32, (gw, gw), 0) // L) == (lax.broadcasted_iota(jnp.int32, (gw, gw), 1) // L)
    half = 2
    ngrp = nch // half
    hrows = half * L
    ti = lax.broadcasted_iota(jnp.int32, (hrows, hrows), 0)
    si = lax.broadcasted_iota(jnp.int32, (hrows, hrows), 1)
    same_chunk = (ti // L) == (si // L)
    tris = ((same_chunk & (si <= ti)).astype(BF16), (same_chunk & (si >= ti)).astype(BF16))
    masks = ((sig < tau, sig <= tau), (sig > tau, sig >= tau))

    def bdtile(x):
        xb = x.astype(BF16)
        return jnp.where(bd, jnp.concatenate([xb] * rep, axis=0), jnp.zeros((), BF16))

    dirs = ((r0_ref, v0_ref, a0_ref, lw0_ref, k0_ref, b0_ref),
            (r1_ref, v1_ref, a1_ref, lw1_ref, k1_ref, b1_ref))

    def build(grp):
        out = {}
        for d, (r_ref, v_ref, a_ref, lw_ref, k_ref, b_ref) in enumerate(dirs):
            base = (grp if d == 0 else ngrp - 1 - grp) * hrows
            rsl = slice(base, base + hrows)
            strict, incl = masks[d]
            lw = lw_ref[0, rsl, :]
            cl = _split_dot_left(tris[d], lw)
            e_in = jnp.exp(cl)
            e_inv = jnp.exp(-cl)
            e_ex = jnp.exp(cl - lw)
            a_t = -(a_ref[0, rsl, :].astype(F32) * e_ex)
            r_t = r_ref[0, rsl, :].astype(F32) * e_in
            b_t = b_ref[0, rsl, :].astype(F32) * e_inv
            k_t = k_ref[0, rsl, :].astype(F32) * e_inv
            v_all = v_ref[0, rsl, :]
            for jj in range(half):
                rs = slice(jj * L, (jj + 1) * L)
                last = (jj + 1) * L - 1 if d == 0 else jj * L
                d_end = jnp.exp(cl[last:last + 1, :])
                b_h = b_t[rs] * d_end
                k_h = k_t[rs] * d_end
                for g in range(ng):
                    sl = slice(g * gw, (g + 1) * gw)
                    out[(base // L + jj, d, g)] = dict(
                        strict=strict, incl=incl, at=a_t[rs, sl], rt=r_t[rs, sl], vv=v_all[rs, sl], bt=b_t[rs, sl],
                        kt=k_t[rs, sl], bh=b_h[:, sl], kh=k_h[:, sl], dend=d_end[:, sl])
        return out

    def s_gram(par):
        for c in par:
            lhs = jnp.concatenate([c["at"], c["rt"]], axis=0).astype(BF16)
            gram = _dot_nt(lhs, jnp.concatenate([bdtile(c["bt"]), bdtile(c["kt"])], axis=0))
            c["gb"], c["gk"] = gram[:, :gw], gram[:, gw:]

    def s_vprod(par):
        for c in par:
            c["m_rb"] = jnp.where(c["incl"], c["gb"][L:], 0.0).astype(BF16)
            c["p"] = jnp.where(c["strict"], c["gb"][:L], 0.0)
            m_k = jnp.concatenate([jnp.where(c["strict"], c["gk"][:L], 0.0), jnp.where(c["incl"], c["gk"][L:], 0.0)],
                                  axis=0)
            res = _dot(m_k.astype(BF16), bdtile(c["vv"]))
            c["makv"], c["mrkv"] = res[:L], res[L:]

    levels = int(math.log2(L))

    def s_level(lvl):
        def run(par):
            for c in par:
                ops = ([] if lvl == 0 else [bdtile(c["t"])]) + ([] if lvl == levels - 1 else [bdtile(c["p"])])
                c["res"] = _dot(c["p"].astype(BF16), ops[0] if len(ops) == 1 else jnp.concatenate(ops, axis=1))
            for c in par:
                if lvl == 0:
                    c["t"] = jnp.where(sig == tau, 1.0, c["p"])
                    c["p"] = c["res"]
                else:
                    c["t"] = c["t"] + c["res"][:, :gw]
                    if lvl < levels - 1:
                        c["p"] = c["res"][:, gw:]
        return run

    def s_apply(par):
        for c in par:
            res = _dot(c["t"].astype(BF16), jnp.concatenate([bdtile(c["at"]), bdtile(c["makv"])], axis=1))
            c["ah"], c["uloc"] = res[:, :gw], res[:, gw:]

    def s_g(par):
        for c in par:
            c["bkh"] = jnp.concatenate([c["bh"], c["kh"]], axis=0).astype(BF16)
            c["g"] = jnp.where(bd, _dot_tn(c["ah"].astype(BF16), c["bkh"][:L]), 0.0).astype(BF16)

    def s_c(par):
        for c in par:
            c["cst"] = jnp.where(bd, _dot_tn(jnp.concatenate([c["uloc"].astype(BF16), c["vv"]], axis=0), c["bkh"]),
                                 0.0)

    def s_q(par):
        for c in par:
            res = _dot(c["m_rb"], jnp.concatenate([bdtile(c["ah"]), bdtile(c["uloc"])], axis=1))
            c["qh"] = (c["rt"] + res[:, :gw]).astype(BF16)
            c["yloc"] = res[:, gw:] + c["mrkv"]

    stages = [s_gram, s_vprod] + [s_level(lvl) for lvl in range(levels)] + [s_apply, s_g, s_c, s_q]

    state = {(d, g): ht_ref[d, g] for d in range(2) for g in range(ng)}
    recs = {}

    def seq_step(step):
        cur = [(d, g, recs[(step if d == 0 else nch - 1 - step, d, g)]) for d in range(2) for g in range(ng)]
        for d, g, c in cur:
            c["htb"] = state[(d, g)].astype(BF16)
        for d, g, c in cur:
            state[(d, g)] = state[(d, g)] * c["dend"] + c["cst"] + _dot(c["htb"], c["g"])
        for d, g, c in cur:
            c["y"] = c["yloc"] + _dot_nt(c["qh"], c["htb"])

    slots = {(i + 1) * len(stages) // (half + 1): i for i in range(half)}
    groups = [build(0)]
    recs.update(groups[0])
    for k in range(ngrp):
        for s, fn in enumerate(stages):
            if s == 2 and k + 1 < ngrp:
                groups.append(build(k + 1))
                recs.update(groups[k + 1])
            fn(list(groups[k].values()))
            if k > 0 and s in slots:
                seq_step((k - 1) * half + slots[s])
    for step in range(nch - half, nch):
        seq_step(step)
    for (d, g), h in state.items():
        ht_ref[d, g] = h
    for d, y_ref in enumerate((y0_ref, y1_ref)):
        y_ref[0] = jnp.concatenate(
            [jnp.concatenate([recs[(j, d, g)]["y"] for g in range(ng)], axis=1) for j in range(nch)], axis=0)


def _split_dot_left(w_bf16, x):
    hi = x.astype(BF16)
    lo = (x - hi.astype(F32)).astype(BF16)
    return _dot(w_bf16, hi) + _dot(w_bf16, lo)


def _scan(r, v, a, lw0, lw1, k0, k1, b0, b1):
    b, s, c = r.shape
    rows = CHUNK * SCAN_CHUNKS
    nblk = s // rows
    fwd = pl.BlockSpec((1, rows, c), lambda bi, i: (bi, i, 0))
    bwd = pl.BlockSpec((1, rows, c), lambda bi, i: (bi, nblk - 1 - i, 0))
    ng = c // GROUP_W
    return pl.pallas_call(
        _scan_kernel,
        grid=(b, nblk),
        in_specs=[fwd] * 6 + [bwd] * 6,
        out_specs=[fwd, bwd],
        out_shape=[jax.ShapeDtypeStruct((b, s, c), F32)] * 2,
        scratch_shapes=[pltpu.VMEM((2, ng, GROUP_W, GROUP_W), F32)],
        compiler_params=_cparams(("parallel", "arbitrary")),
        name="scan",
    )(r, v, a, lw0, k0, b0, r, v, a, lw1, k1, b1)


MID_BLOCKS = 4


def _mid_kernel(sink_ref, q_ref, kp_ref, kc_ref, kn_ref, vp_ref, vc_ref, vn_ref, x_ref, gain_ref, wg_ref, y0_ref,
                y1_ref, bonus_ref, g_ref, lnw_ref, lnb_ref, bd_ref, wua_ref, wur_ref, wout_ref, o_ref):
    n = pl.program_id(1)
    nb = pl.num_programs(1)
    blk, nsub = ATTN_BLOCK, MID_BLOCKS
    d = x_ref.shape[2]
    k = jnp.concatenate([kp_ref[0], kc_ref[0], kn_ref[0]], axis=0)
    v = jnp.concatenate([vp_ref[0], vc_ref[0], vn_ref[0]], axis=0)
    qi = lax.broadcasted_iota(jnp.int32, (2 * blk, blk), 0) % blk
    ki = lax.broadcasted_iota(jnp.int32, (2 * blk, blk), 1)
    first = lax.broadcasted_iota(jnp.int32, (2 * blk, 1), 0) < blk
    left = lax.broadcasted_iota(jnp.int32, (1, LANES), 1) < HEAD_DIM
    zero = jnp.zeros((), BF16)
    bd = bd_ref[...]

    def scores(j):
        rows = slice(j * blk, (j + 1) * blk)
        win = slice(j * blk, (j + 3) * blk)
        mask_prev = (ki >= qi) & ((n > 0) if j == 0 else True)
        mask_next = (ki <= qi) & ((n < nb - 1) if j == nsub - 1 else True)
        recs = []
        for g in range(KV_HEADS):
            pairs = jnp.concatenate([q_ref[0, rows, p * LANES:(p + 1) * LANES] for p in (2 * g, 2 * g + 1)], axis=0)
            for side in range(2):
                off = LANES * ((g + side) % 2)
                keep = left if side == 0 else jnp.logical_not(left)
                sk = jnp.where(first, sink_ref[4 * g + side], sink_ref[4 * g + 2 + side])
                sc = _dot_nt(jnp.where(keep, pairs, zero), k[win, off:off + LANES])
                sc = jnp.concatenate([jnp.where(mask_prev, sc[:, :blk], -1e30), sc[:, blk:2 * blk],
                                      jnp.where(mask_next, sc[:, 2 * blk:], -1e30)], axis=1)
                recs.append(dict(sc=sc, sk=sk, v=jnp.where(keep, v[win, off:off + LANES], zero)))
        return recs

    def softmax(recs):
        for c in recs:
            m = jnp.maximum(jnp.max(c["sc"], axis=-1, keepdims=True), c["sk"])
            e = jnp.exp(c["sc"] - m)
            c["den"] = jnp.sum(e, axis=-1, keepdims=True) + jnp.exp(c["sk"] - m)
            c["e"] = e.astype(BF16)

    def weighted(recs):
        outs = []
        for g in range(KV_HEADS):
            both = sum(_dot(c["e"], c["v"]) / c["den"] for c in recs[2 * g:2 * g + 2])
            outs += [both[:blk], both[blk:]]
        return jnp.concatenate(outs, axis=1).astype(BF16)

    quarter = wg_ref.shape[1] // 4
    x = x_ref[0]
    blocks = [scores(0), scores(1)]
    h = _rms(x, gain_ref[...]).astype(BF16)
    gate_pre = [_dot(h, wg_ref[:, :quarter])]
    softmax(blocks[0])
    gate_pre.append(_dot(h, wg_ref[:, quarter:2 * quarter]))
    softmax(blocks[1])
    attn = [weighted(blocks[0])]
    blocks.append(scores(2))
    gates = [_sigmoid(gate_pre[0])]
    gate_pre.append(_dot(h, wg_ref[:, 2 * quarter:3 * quarter]))
    attn.append(weighted(blocks[1]))
    blocks.append(scores(3))
    softmax(blocks[2])
    gate_pre.append(_dot(h, wg_ref[:, 3 * quarter:]))
    gates.append(_sigmoid(gate_pre[1]))
    softmax(blocks[3])
    attn.append(weighted(blocks[2]))

    y = y0_ref[0] + y1_ref[0]
    yc = y - _head_sum(y, bd) * (1.0 / HEAD_DIM)
    var = _head_sum(yc * yc, bd, split=False) * (1.0 / HEAD_DIM)
    gates.append(_sigmoid(gate_pre[2]))
    attn.append(weighted(blocks[3]))
    yn = yc * lax.rsqrt(var + GN_EPS) * lnw_ref[...] + lnb_ref[...]
    rw = ((yn + bonus_ref[0]) * g_ref[0]).astype(BF16)
    up_attn = _dot(jnp.concatenate(attn, axis=0), wua_ref[...])
    gates.append(_sigmoid(gate_pre[3]))
    up_rwkv = _dot(rw, wur_ref[...])
    half = d // 2
    merged = (jnp.concatenate(gates[:2], axis=1) * up_attn + jnp.concatenate(gates[2:], axis=1) * up_rwkv).astype(BF16)
    for col in range(0, d, half):
        o_ref[0, :, col:col + half] = x[:, col:col + half] + _dot(merged, wout_ref[:, col:col + half])


def _mid(sink, q, k2, v2, x, gain, wg, y0, y1, bonus, g, lnw, lnb, bd, wua, wur, wout):
    b, s, d = x.shape
    c = RWKV_WIDTH
    tm = ATTN_BLOCK * MID_BLOCKS
    nb = s // tm
    last = s // ATTN_BLOCK - 1
    const = lambda shape: pl.BlockSpec(shape, lambda bi, n: (0,) * len(shape))
    tok = lambda width: pl.BlockSpec((1, tm, width), lambda bi, n: (bi, n, 0))
    prev = pl.BlockSpec((1, ATTN_BLOCK, 2 * KV_WIDTH), lambda bi, n: (bi, jnp.maximum(n * MID_BLOCKS - 1, 0), 0))
    nxt = pl.BlockSpec((1, ATTN_BLOCK, 2 * KV_WIDTH), lambda bi, n: (bi, jnp.minimum((n + 1) * MID_BLOCKS, last), 0))
    kv = [prev, tok(2 * KV_WIDTH), nxt]
    return pl.pallas_call(
        _mid_kernel,
        grid=(b, nb),
        in_specs=[pl.BlockSpec(memory_space=pltpu.SMEM), tok(ATTN_WIDTH)] + kv + kv
                 + [tok(d), const((1, d)), const((d, 2 * d)), tok(c), tok(c), tok(c), tok(c), const((1, c)),
                    const((1, c)), const((LANES, LANES)), const((ATTN_WIDTH, d)), const((c, d)), const((d, d))],
        out_specs=tok(d),
        out_shape=jax.ShapeDtypeStruct((b, s, d), F32),
        compiler_params=_cparams(("parallel", "parallel")),
        name="mid",
    )(sink, q, k2, k2, k2, v2, v2, v2, x, gain, wg, y0, y1, bonus, g, lnw, lnb, bd, wua, wur, wout)


FF_SLAB = 1024


def _ffn_kernel(x_ref, p_ref, gf_ref, w1_ref, w2_ref, gp_ref, wpg_ref, wple_ref, o_ref):
    x = x_ref[0]
    h = _rms(x, gf_ref[...]).astype(BF16)
    acc = x
    for j in range(0, w1_ref.shape[1], FF_SLAB):
        hid = jnp.maximum(_dot(h, w1_ref[:, j:j + FF_SLAB]), 0.0)
        acc = acc + _dot((hid * hid).astype(BF16), w2_ref[j:j + FF_SLAB, :])
    hp = _rms(acc, gp_ref[...]).astype(BF16)
    o_ref[0] = acc + _dot(p_ref[0].astype(BF16), wple_ref[...]) * _sigmoid(_dot(hp, wpg_ref[...]))


def _ffn(x, p, gf, w1, w2, gp, wpg, wple, tm):
    b, s, d = x.shape
    dff = w1.shape[1]
    pd = p.shape[-1]
    const = lambda shape: pl.BlockSpec(shape, lambda bi, i: (0,) * len(shape), pipeline_mode=pl.Buffered(1))
    tok = lambda width: pl.BlockSpec((1, tm, width), lambda bi, i: (bi, i, 0))
    return pl.pallas_call(
        _ffn_kernel,
        grid=(b, s // tm),
        in_specs=[tok(d), tok(pd), const((1, d)), const((d, dff)), const((dff, d)), const((1, d)),
                  const((d, d)), const((pd, d))],
        out_specs=tok(d),
        out_shape=jax.ShapeDtypeStruct((b, s, d), F32),
        compiler_params=_cparams(("parallel", "parallel")),
        name="ffn",
    )(x, p, gf, w1, w2, gp, wpg, wple)


def _rotary_tables(s):
    half = ROT_DIM // 2
    inv_freq = jnp.power(jnp.float32(ROPE_THETA), -jnp.arange(half, dtype=F32) * 2.0 / ROT_DIM)
    ang = jnp.arange(s).astype(F32)[:, None] * inv_freq[None, :]
    cos, sin = jnp.cos(ang), jnp.sin(ang)
    pad = jnp.zeros((s, HEAD_DIM - ROT_DIM), F32)
    zero = jnp.zeros((s, half), F32)
    c = jnp.concatenate([cos, cos, pad + 1.0], axis=1)
    s1 = jnp.concatenate([-sin, zero, pad], axis=1)
    s2 = jnp.concatenate([zero, sin, pad], axis=1)
    rep = LANES // HEAD_DIM
    return jnp.tile(c, (1, rep)), jnp.tile(s1, (1, rep)), jnp.tile(s2, (1, rep))


def kernel(x, p, norm_mix, w_in, shift_mu, q_norm, k_norm, sink, w0, w2, a0, a2, g2, k_k, k_a, r_k, lnx_w, lnx_b,
           w_up_attn, w_up_rwkv, w_out, norm_ffn, w_ff1, w_ff2, norm_ple, w_ple_gate, w_ple):
    bsz, s, d = x.shape
    depth = w_in.shape[0]
    c = RWKV_WIDTH
    tm = min(512, s)
    cos, s1, s2 = _rotary_tables(s)
    lane = jnp.arange(LANES)
    bd = ((lane[:, None] // HEAD_DIM) == (lane[None, :] // HEAD_DIM)).astype(BF16)
    o_g = PROJ_W
    for i in range(depth):
        wi = w_in[i]
        qkg = jnp.concatenate([jnp.tile(q_norm[i], Q_HEADS), jnp.tile(k_norm[i], KV_HEADS)])[None, :]
        zpad = jnp.zeros((DECAY_RANK, c), F32)
        wlo = jnp.concatenate([jnp.concatenate([w2[i, 0], zpad], axis=0), jnp.concatenate([w2[i, 1], zpad], axis=0),
                               jnp.concatenate([zpad, a2[i, 0]], axis=0), jnp.concatenate([zpad, a2[i, 1]], axis=0)],
                              axis=1)
        wlo = (0.5 * wlo).astype(BF16)
        k_mix = jnp.stack([1.0 - 0.5 * k_a[i], 0.5 * k_a[i]])
        full = lambda w: (w, 0, w.shape[1])
        ((q, k2, v2, r, v, a, lw0, lw1, k0, k1, b0, b1, g, bonus), (wg, wua, wur, wo, wf1, wf2, wpg, wpl)) = _front(
            x, norm_mix[i][None, :], wi, cos, s1, s2, qkg, bd, shift_mu[i][None, :], wlo, g2[i].astype(BF16),
            0.5 * w0[i], 0.5 * a0[i], k_k[i][None, :], k_mix, r_k[i].reshape(2, c), tm,
            [(wi, o_g, wi.shape[1]), full(w_up_attn[i]), full(w_up_rwkv[i]), full(w_out[i]), full(w_ff1[i]),
             full(w_ff2[i]), full(w_ple_gate[i]), full(w_ple[i])])
        y0, y1 = _scan(r, v, a, lw0, lw1, k0, k1, b0, b1)
        x = _mid(sink[i], q, k2, v2, x, norm_mix[i][None, :], wg, y0, y1, bonus, g, lnx_w[i][None, :],
                 lnx_b[i][None, :], bd, wua, wur, wo)
        x = _ffn(x, p[i], norm_ffn[i][None, :], wf1, wf2, norm_ple[i][None, :], wpg, wpl, min(1024, s))
    return x
```

```python
import functools
import math

import jax
import jax.numpy as jnp
from jax import lax
from jax.experimental import pallas as pl
from jax.experimental.pallas import tpu as pltpu

F32 = jnp.float32
BF16 = jnp.bfloat16

LANES = 128
HEAD_DIM = 64
Q_HEADS = 8
KV_HEADS = 2
ATTN_WIDTH = Q_HEADS * HEAD_DIM
KV_WIDTH = KV_HEADS * HEAD_DIM
WINDOW = 128
ATTN_BLOCK = 128
ROPE_THETA = 500000.0
ROT_DIM = HEAD_DIM // 4
RWKV_WIDTH = 512
DECAY_RANK = 64
ICLR_RANK = 64
GATE_RANK = 128
RWKV_COLS = 3 * RWKV_WIDTH + DECAY_RANK + ICLR_RANK + GATE_RANK
NORM_EPS = 1e-6
GN_EPS = 64e-5
CHUNK = 64
GROUP_W = 128
SCAN_CHUNKS = 16
VMEM_LIMIT = 56 * 1024 * 1024


def _cparams(sem):
    return pltpu.CompilerParams(dimension_semantics=sem, vmem_limit_bytes=VMEM_LIMIT)


def _dot(a, b):
    return jnp.dot(a, b, preferred_element_type=F32)


def _dot_nt(a, b):
    return lax.dot_general(a, b, (((1,), (1,)), ((), ())), preferred_element_type=F32)


def _dot_tn(a, b):
    return lax.dot_general(a, b, (((0,), (0,)), ((), ())), preferred_element_type=F32)


def _split_dot(x, w_bf16):
    hi = x.astype(BF16)
    lo = (x - hi.astype(F32)).astype(BF16)
    return _dot(hi, w_bf16) + _dot(lo, w_bf16)


def _head_sum(x, bd, split=True):
    one = _split_dot if split else (lambda t, w: _dot(t.astype(BF16), w))
    cols = [one(x[:, i:i + LANES], bd) for i in range(0, x.shape[1], LANES)]
    return cols[0] if len(cols) == 1 else jnp.concatenate(cols, axis=1)


def _rms(x, gain):
    return x * lax.rsqrt(jnp.mean(x * x, axis=-1, keepdims=True) + NORM_EPS) * gain


def _sigmoid(x):
    return 0.5 * jnp.tanh(0.5 * x) + 0.5


QK_W = ATTN_WIDTH + KV_WIDTH
QKV_W = QK_W + KV_WIDTH
PROJ_W = QKV_W + RWKV_COLS
HALO = 8
FRONT_SUB = 128


def _qk_rows(qk, cos, s1, s2, gains, bd):
    ms = _head_sum(qk * qk, bd, split=False) * (1.0 / HEAD_DIM)
    qk = qk * lax.rsqrt(ms + NORM_EPS) * gains
    tiles = []
    for i in range(0, qk.shape[1], LANES):
        t = qk[:, i:i + LANES]
        t = t * cos + pltpu.roll(t, LANES - ROT_DIM // 2, 1) * s1 + pltpu.roll(t, ROT_DIM // 2, 1) * s2
        tiles.append(t)
    return jnp.concatenate(tiles, axis=1)


def _slab_plan(weights, steps, step_index):
    ins, in_specs, out_specs, out_shapes, cols = [], [], [], [], []
    for w, lo, hi in weights:
        rows, width = w.shape
        assert rows % steps == 0, (rows, steps)
        n = rows // steps
        ins.append(w.reshape(steps, n, width))
        in_specs.append(pl.BlockSpec((1, n, width), lambda *g: (step_index(*g), 0, 0)))
        out_specs.append(pl.BlockSpec((1, n, hi - lo), lambda *g: (step_index(*g), 0, 0)))
        out_shapes.append(jax.ShapeDtypeStruct((steps, n, hi - lo), BF16))
        cols.append((lo, hi))
    return ins, in_specs, out_specs, out_shapes, cols


N_FRONT_IN, N_FRONT_OUT = 18, 14


def _front_kernel(cols, *refs):
    nw = len(cols)
    (x_ref, xp_ref, xn_ref, gain_ref, w_ref, cos_ref, s1_ref, s2_ref, qkg_ref, bd_ref, mu_ref, wlo_ref,
     g2_ref, w0_ref, a0_ref, kk_ref, ka_ref, rk_ref) = refs[:N_FRONT_IN]
    (q_ref, k_ref, v_ref, r_ref, vv_ref, a_ref, lw0_ref, lw1_ref, k0_ref, k1_ref, b0_ref, b1_ref, g_ref,
     bonus_ref) = refs[N_FRONT_IN + nw:N_FRONT_IN + nw + N_FRONT_OUT]
    for src_ref, dst_ref, (lo, hi) in zip(refs[N_FRONT_IN:], refs[N_FRONT_IN + nw + N_FRONT_OUT:], cols):
        dst_ref[...] = src_ref[:, :, lo:hi].astype(BF16)
    i = pl.program_id(1)
    nt = pl.num_programs(1)
    tm = x_ref.shape[1]
    sub = FRONT_SUB
    c = RWKV_WIDTH
    gain = gain_ref[...]
    bd = bd_ref[...]
    o_r, o_k, o_v, o_low = QKV_W, QKV_W + c, QKV_W + 2 * c, QKV_W + 3 * c
    halo = jnp.concatenate([xp_ref[0], xn_ref[0]], axis=0)
    w_low = w_ref[:, o_low:PROJ_W].astype(BF16)
    h_blocks = [_rms(halo, gain).astype(BF16)]
    low_blocks = [_dot(h_blocks[0], w_low)]
    for s in range(0, tm, sub):
        h_blocks.insert(-1, _rms(x_ref[0, s:s + sub, :], gain).astype(BF16))
        low_blocks.insert(-1, _dot(h_blocks[-2], w_low))
    h_ext = jnp.concatenate(h_blocks, axis=0)
    h = h_ext[:tm]
    z_low = jnp.concatenate(low_blocks, axis=0)

    row = lax.broadcasted_iota(jnp.int32, (HALO, 1), 0)
    ri = lax.broadcasted_iota(jnp.int32, (sub, sub), 0)
    ci = lax.broadcasted_iota(jnp.int32, (sub, sub), 1)
    adjacent = ((ri - ci == 1) | (ci - ri == 1)).astype(BF16)

    def project(lo, hi):
        return _dot(h_ext if lo >= QKV_W else h, w_ref[:, lo:hi].astype(BF16))

    def shift(z, lo, hi):
        mu = mu_ref[:, lo:hi]
        mix = 0.5 * mu
        keep = 1.0 - mu
        prev_row = jnp.where(i > 0, z[tm + HALO - 1:tm + HALO], 0.0)
        next_row = jnp.where(i < nt - 1, z[tm + HALO:tm + HALO + 1], 0.0)
        blocks = []
        for s in range(0, tm, sub):
            zs = z[s:s + sub]
            part = zs * keep + _dot(adjacent, zs.astype(BF16)) * mix
            before = z[s - 1:s] if s > 0 else prev_row
            after = z[s + sub:s + sub + 1] if s + sub < tm else next_row
            blocks += [part[:HALO] + jnp.where(row == 0, before * mix, 0.0), part[HALO:sub - HALO],
                       part[sub - HALO:] + jnp.where(row == HALO - 1, after * mix, 0.0)]
        return jnp.concatenate(blocks, axis=0)

    half = c // 2
    cos, s1, s2 = cos_ref[...], s1_ref[...], s2_ref[...]

    def q_rows(p, j):
        q = _qk_rows(p, cos, s1, s2, qkg_ref[:, j * half:(j + 1) * half], bd)
        q_ref[0, :, j * half:(j + 1) * half] = (q * (HEAD_DIM ** -0.5)).astype(BF16)

    z_k0 = project(o_k, o_k + half)
    z_low = shift(z_low, 3 * c, RWKV_COLS)
    lowrank = z_low[:, :LANES]
    left = lax.broadcasted_iota(jnp.int32, (1, LANES), 1) < DECAY_RANK
    lowrank = jnp.where(left, jnp.tanh(lowrank), lowrank).astype(BF16)
    gate_in = _sigmoid(z_low[:, LANES:]).astype(BF16)

    z_k1 = project(o_k + half, o_k + c)
    wa = _dot(lowrank, wlo_ref[...])
    g_ref[0] = _dot(gate_in, g2_ref[...])
    p_q0 = project(0, half)

    lw_scale = -0.5 * math.exp(-0.5)
    lw0_ref[0] = lw_scale * jnp.tanh(w0_ref[0:1, :] + wa[:, :c]) + lw_scale
    p_q1 = project(half, c)
    lw1_ref[0] = lw_scale * jnp.tanh(w0_ref[1:2, :] + wa[:, c:2 * c]) + lw_scale
    z_r0 = project(o_r, o_r + half)

    k = shift(jnp.concatenate([z_k0, z_k1], axis=1), c, 2 * c)
    kk = k * kk_ref[...]
    kk = kk * lax.rsqrt(jnp.maximum(_head_sum(kk * kk, bd, split=False), 1e-24))
    a_ref[0] = kk.astype(BF16)
    kk_half = 0.5 * kk
    z_r1 = project(o_r + half, o_r + c)

    later = [lambda: project(o_v, o_v + half), lambda: project(o_v + half, o_v + c)]
    z_v = []
    dot_rk = None
    for d, (kd_ref, bdir_ref) in enumerate(((k0_ref, b0_ref), (k1_ref, b1_ref))):
        t = jnp.tanh(a0_ref[d:d + 1, :] + wa[:, (2 + d) * c:(3 + d) * c])
        k_dir = k * (ka_ref[0:1, :] + ka_ref[1:2, :] * t)
        kd_ref[0] = k_dir.astype(BF16)
        bdir_ref[0] = (kk_half * t + kk_half).astype(BF16)
        term = k_dir * rk_ref[d:d + 1, :]
        dot_rk = term if dot_rk is None else dot_rk + term
        z_v.append(later[d]())
        q_rows((p_q0, p_q1)[d], d)

    p_kv = project(ATTN_WIDTH, QKV_W)
    r = shift(jnp.concatenate([z_r0, z_r1], axis=1), 0, c)
    r_ref[0] = r.astype(BF16)
    r_dot = _head_sum(r * dot_rk, bd, split=False)
    k_att = _qk_rows(p_kv[:, :KV_WIDTH], cos, s1, s2, qkg_ref[:, ATTN_WIDTH:], bd)
    k_ref[0] = jnp.concatenate([k_att, pltpu.roll(k_att, HEAD_DIM, 1)], axis=1).astype(BF16)
    v = shift(jnp.concatenate(z_v, axis=1), 2 * c, 3 * c)
    vv_ref[0] = v.astype(BF16)
    bonus_ref[0] = r_dot * v
    v_att = p_kv[:, KV_WIDTH:]
    v_ref[0] = jnp.concatenate([v_att, pltpu.roll(v_att, HEAD_DIM, 1)], axis=1).astype(BF16)


def _front(x, gain, w, cos, s1, s2, qkg, bd, mu, wlo, g2, w0, a0, k_k, k_a, r_k, tm, weights):
    b, s, d = x.shape
    nt = s // tm
    w_ins, w_in_specs, w_out_specs, w_out_shapes, cols = _slab_plan(weights, b * nt, lambda bi, i: bi * nt + i)
    c = RWKV_WIDTH
    nh = tm // HALO
    const = lambda shape: pl.BlockSpec(shape, lambda bi, i: (0,) * len(shape))
    tok = lambda width: pl.BlockSpec((1, tm, width), lambda bi, i: (bi, i, 0))
    tab = pl.BlockSpec((tm, LANES), lambda bi, i: (i, 0))
    halo_p = pl.BlockSpec((1, HALO, d), lambda bi, i: (bi, jnp.maximum(i * nh - 1, 0), 0))
    halo_n = pl.BlockSpec((1, HALO, d), lambda bi, i: (bi, jnp.minimum((i + 1) * nh, s // HALO - 1), 0))
    rwkv_dtypes = (BF16,) * 3 + (F32,) * 2 + (BF16,) * 4 + (F32,) * 2
    outs = pl.pallas_call(
        functools.partial(_front_kernel, cols),
        grid=(b, nt),
        in_specs=[tok(d), halo_p, halo_n, const((1, d)), const((d, PROJ_W)), tab, tab, tab, const((1, QK_W)),
                  const((LANES, LANES)), const((1, RWKV_COLS)), const((LANES, 4 * c)), const((GATE_RANK, c)),
                  const((2, c)), const((2, c)), const((1, c)), const((2, c)), const((2, c))] + w_in_specs,
        out_specs=[tok(ATTN_WIDTH), tok(2 * KV_WIDTH), tok(2 * KV_WIDTH)] + [tok(c)] * 11 + w_out_specs,
        out_shape=[jax.ShapeDtypeStruct((b, s, ATTN_WIDTH), BF16),
                   jax.ShapeDtypeStruct((b, s, 2 * KV_WIDTH), BF16),
                   jax.ShapeDtypeStruct((b, s, 2 * KV_WIDTH), BF16)]
                  + [jax.ShapeDtypeStruct((b, s, c), dt) for dt in rwkv_dtypes] + w_out_shapes,
        compiler_params=_cparams(("parallel", "parallel")),
        name="front",
    )(x, x, x, gain, w, cos, s1, s2, qkg, bd, mu, wlo, g2, w0, a0, k_k, k_a, r_k, *w_ins)
    converted = [o.reshape(wt.shape[0], hi - lo) for o, (wt, lo, hi) in zip(outs[N_FRONT_OUT:], weights)]
    return outs[:N_FRONT_OUT], converted


def _scan_kernel(r0_ref, v0_ref, a0_ref, lw0_ref, k0_ref, b0_ref,
                 r1_ref, v1_ref, a1_ref, lw1_ref, k1_ref, b1_ref,
                 y0_ref, y1_ref, ht_ref):
    L, gw, nch = CHUNK, GROUP_W, SCAN_CHUNKS
    rep = gw // L
    ng = RWKV_WIDTH // gw

    @pl.when(pl.program_id(1) == 0)
    def _():
        ht_ref[...] = jnp.zeros_like(ht_ref)

    tau = lax.broadcasted_iota(jnp.int32, (L, gw), 0)
    sig = lax.broadcasted_iota(jnp.int32, (L, gw), 1) % L
    bd = (lax.broadcasted_iota(jnp.int32, (gw, gw), 0) // L) == (lax.broadcasted_iota(jnp.int32, (gw, gw), 1) // L)
    half = 2
    ngrp = nch // half
    hrows = half * L
    ti = lax.broadcasted_iota(jnp.int32, (hrows, hrows), 0)
    si = lax.broadcasted_iota(jnp.int32, (hrows, hrows), 1)
    same_chunk = (ti // L) == (si // L)
    tris = ((same_chunk & (si <= ti)).astype(BF16), (same_chunk & (si >= ti)).astype(BF16))
    masks = ((sig < tau, sig <= tau), (sig > tau, sig >= tau))

    def bdtile(x):
        xb = x.astype(BF16)
        return jnp.where(bd, jnp.concatenate([xb] * rep, axis=0), jnp.zeros((), BF16))

    dirs = ((r0_ref, v0_ref, a0_ref, lw0_ref, k0_ref, b0_ref),
            (r1_ref, v1_ref, a1_ref, lw1_ref, k1_ref, b1_ref))

    def build(grp):
        out = {}
        for d, (r_ref, v_ref, a_ref, lw_ref, k_ref, b_ref) in enumerate(dirs):
            base = (grp if d == 0 else ngrp - 1 - grp) * hrows
            rsl = slice(base, base + hrows)
            strict, incl = masks[d]
            lw = lw_ref[0, rsl, :]
            cl = _split_dot_left(tris[d], lw)
            e_in = jnp.exp(cl)
            e_inv = jnp.exp(-cl)
            e_ex = jnp.exp(cl - lw)
            a_t = -(a_ref[0, rsl, :].astype(F32) * e_ex)
            r_t = r_ref[0, rsl, :].astype(F32) * e_in
            b_t = b_ref[0, rsl, :].astype(F32) * e_inv
            k_t = k_ref[0, rsl, :].astype(F32) * e_inv
            v_all = v_ref[0, rsl, :]
            for jj in range(half):
                rs = slice(jj * L, (jj + 1) * L)
                last = (jj + 1) * L - 1 if d == 0 else jj * L
                d_end = jnp.exp(cl[last:last + 1, :])
                b_h = b_t[rs] * d_end
                k_h = k_t[rs] * d_end
                for g in range(ng):
                    sl = slice(g * gw, (g + 1) * gw)
                    out[(base // L + jj, d, g)] = dict(
                        strict=strict, incl=incl, at=a_t[rs, sl], rt=r_t[rs, sl], vv=v_all[rs, sl], bt=b_t[rs, sl],
                        kt=k_t[rs, sl], bh=b_h[:, sl], kh=k_h[:, sl], dend=d_end[:, sl])
        return out

    def s_gram(par):
        for c in par:
            lhs = jnp.concatenate([c["at"], c["rt"]], axis=0).astype(BF16)
            gram = _dot_nt(lhs, jnp.concatenate([bdtile(c["bt"]), bdtile(c["kt"])], axis=0))
            c["gb"], c["gk"] = gram[:, :gw], gram[:, gw:]

    def s_vprod(par):
        for c in par:
            c["m_rb"] = jnp.where(c["incl"], c["gb"][L:], 0.0).astype(BF16)
            c["p"] = jnp.where(c["strict"], c["gb"][:L], 0.0)
            m_k = jnp.concatenate([jnp.where(c["strict"], c["gk"][:L], 0.0), jnp.where(c["incl"], c["gk"][L:], 0.0)],
                                  axis=0)
            res = _dot(m_k.astype(BF16), bdtile(c["vv"]))
            c["makv"], c["mrkv"] = res[:L], res[L:]

    levels = int(math.log2(L))

    def s_level(lvl):
        def run(par):
            for c in par:
                ops = ([] if lvl == 0 else [bdtile(c["t"])]) + ([] if lvl == levels - 1 else [bdtile(c["p"])])
                c["res"] = _dot(c["p"].astype(BF16), ops[0] if len(ops) == 1 else jnp.concatenate(ops, axis=1))
            for c in par:
                if lvl == 0:
                    c["t"] = jnp.where(sig == tau, 1.0, c["p"])
                    c["p"] = c["res"]
                else:
                    c["t"] = c["t"] + c["res"][:, :gw]
                    if lvl < levels - 1:
                        c["p"] = c["res"][:, gw:]
        return run

    def s_apply(par):
        for c in par:
            res = _dot(c["t"].astype(BF16), jnp.concatenate([bdtile(c["at"]), bdtile(c["makv"])], axis=1))
            c["ah"], c["uloc"] = res[:, :gw], res[:, gw:]

    def s_g(par):
        for c in par:
            c["bkh"] = jnp.concatenate([c["bh"], c["kh"]], axis=0).astype(BF16)
            c["g"] = jnp.where(bd, _dot_tn(c["ah"].astype(BF16), c["bkh"][:L]), 0.0).astype(BF16)

    def s_c(par):
        for c in par:
            c["cst"] = jnp.where(bd, _dot_tn(jnp.concatenate([c["uloc"].astype(BF16), c["vv"]], axis=0), c["bkh"]),
                                 0.0)

    def s_q(par):
        for c in par:
            res = _dot(c["m_rb"], jnp.concatenate([bdtile(c["ah"]), bdtile(c["uloc"])], axis=1))
            c["qh"] = (c["rt"] + res[:, :gw]).astype(BF16)
            c["yloc"] = res[:, gw:] + c["mrkv"]

    stages = [s_gram, s_vprod] + [s_level(lvl) for lvl in range(levels)] + [s_apply, s_g, s_c, s_q]

    state = {(d, g): ht_ref[d, g] for d in range(2) for g in range(ng)}
    recs = {}

    def seq_step(step):
        cur = [(d, g, recs[(step if d == 0 else nch - 1 - step, d, g)]) for d in range(2) for g in range(ng)]
        for d, g, c in cur:
            c["htb"] = state[(d, g)].astype(BF16)
        for d, g, c in cur:
            state[(d, g)] = state[(d, g)] * c["dend"] + c["cst"] + _dot(c["htb"], c["g"])
        for d, g, c in cur:
            c["y"] = c["yloc"] + _dot_nt(c["qh"], c["htb"])

    slots = {(i + 1) * len(stages) // (half + 1): i for i in range(half)}
    groups = [build(0)]
    recs.update(groups[0])
    for k in range(ngrp):
        for s, fn in enumerate(stages):
            if s == 2 and k + 1 < ngrp:
                groups.append(build(k + 1))
                recs.update(groups[k + 1])
            fn(list(groups[k].values()))
            if k > 0 and s in slots:
                seq_step((k - 1) * half + slots[s])
    for step in range(nch - half, nch):
        seq_step(step)
    for (d, g), h in state.items():
        ht_ref[d, g] = h
    for d, y_ref in enumerate((y0_ref, y1_ref)):
        y_ref[0] = jnp.concatenate(
            [jnp.concatenate([recs[(j, d, g)]["y"] for g in range(ng)], axis=1) for j in range(nch)], axis=0)


def _split_dot_left(w_bf16, x):
    hi = x.astype(BF16)
    lo = (x - hi.astype(F32)).astype(BF16)
    return _dot(w_bf16, hi) + _dot(w_bf16, lo)


def _scan(r, v, a, lw0, lw1, k0, k1, b0, b1):
    b, s, c = r.shape
    rows = CHUNK * SCAN_CHUNKS
    nblk = s // rows
    fwd = pl.BlockSpec((1, rows, c), lambda bi, i: (bi, i, 0))
    bwd = pl.BlockSpec((1, rows, c), lambda bi, i: (bi, nblk - 1 - i, 0))
    ng = c // GROUP_W
    return pl.pallas_call(
        _scan_kernel,
        grid=(b, nblk),
        in_specs=[fwd] * 6 + [bwd] * 6,
        out_specs=[fwd, bwd],
        out_shape=[jax.ShapeDtypeStruct((b, s, c), F32)] * 2,
        scratch_shapes=[pltpu.VMEM((2, ng, GROUP_W, GROUP_W), F32)],
        compiler_params=_cparams(("parallel", "arbitrary")),
        name="scan",
    )(r, v, a, lw0, k0, b0, r, v, a, lw1, k1, b1)


MID_BLOCKS = 4


def _mid_kernel(sink_ref, q_ref, kp_ref, kc_ref, kn_ref, vp_ref, vc_ref, vn_ref, x_ref, gain_ref, wg_ref, y0_ref,
                y1_ref, bonus_ref, g_ref, lnw_ref, lnb_ref, bd_ref, wua_ref, wur_ref, wout_ref, o_ref):
    n = pl.program_id(1)
    nb = pl.num_programs(1)
    blk, nsub = ATTN_BLOCK, MID_BLOCKS
    d = x_ref.shape[2]
    k = jnp.concatenate([kp_ref[0], kc_ref[0], kn_ref[0]], axis=0)
    v = jnp.concatenate([vp_ref[0], vc_ref[0], vn_ref[0]], axis=0)
    qi = lax.broadcasted_iota(jnp.int32, (2 * blk, blk), 0) % blk
    ki = lax.broadcasted_iota(jnp.int32, (2 * blk, blk), 1)
    first = lax.broadcasted_iota(jnp.int32, (2 * blk, 1), 0) < blk
    left = lax.broadcasted_iota(jnp.int32, (1, LANES), 1) < HEAD_DIM
    zero = jnp.zeros((), BF16)
    bd = bd_ref[...]

    def scores(j):
        rows = slice(j * blk, (j + 1) * blk)
        win = slice(j * blk, (j + 3) * blk)
        mask_prev = (ki >= qi) & ((n > 0) if j == 0 else True)
        mask_next = (ki <= qi) & ((n < nb - 1) if j == nsub - 1 else True)
        recs = []
        for g in range(KV_HEADS):
            pairs = jnp.concatenate([q_ref[0, rows, p * LANES:(p + 1) * LANES] for p in (2 * g, 2 * g + 1)], axis=0)
            for side in range(2):
                off = LANES * ((g + side) % 2)
                keep = left if side == 0 else jnp.logical_not(left)
                sk = jnp.where(first, sink_ref[4 * g + side], sink_ref[4 * g + 2 + side])
                sc = _dot_nt(jnp.where(keep, pairs, zero), k[win, off:off + LANES])
                sc = jnp.concatenate([jnp.where(mask_prev, sc[:, :blk], -1e30), sc[:, blk:2 * blk],
                                      jnp.where(mask_next, sc[:, 2 * blk:], -1e30)], axis=1)
                recs.append(dict(sc=sc, sk=sk, v=jnp.where(keep, v[win, off:off + LANES], zero)))
        return recs

    def softmax(recs):
        for c in recs:
            m = jnp.maximum(jnp.max(c["sc"], axis=-1, keepdims=True), c["sk"])
            e = jnp.exp(c["sc"] - m)
            c["den"] = jnp.sum(e, axis=-1, keepdims=True) + jnp.exp(c["sk"] - m)
            c["e"] = e.astype(BF16)

    def weighted(recs):
        outs = []
        for g in range(KV_HEADS):
            both = sum(_dot(c["e"], c["v"]) / c["den"] for c in recs[2 * g:2 * g + 2])
            outs += [both[:blk], both[blk:]]
        return jnp.concatenate(outs, axis=1).astype(BF16)

    quarter = wg_ref.shape[1] // 4
    x = x_ref[0]
    blocks = [scores(0), scores(1)]
    h = _rms(x, gain_ref[...]).astype(BF16)
    gate_pre = [_dot(h, wg_ref[:, :quarter])]
    softmax(blocks[0])
    gate_pre.append(_dot(h, wg_ref[:, quarter:2 * quarter]))
    softmax(blocks[1])
    attn = [weighted(blocks[0])]
    blocks.append(scores(2))
    gates = [_sigmoid(gate_pre[0])]
    gate_pre.append(_dot(h, wg_ref[:, 2 * quarter:3 * quarter]))
    attn.append(weighted(blocks[1]))
    blocks.append(scores(3))
    softmax(blocks[2])
    gate_pre.append(_dot(h, wg_ref[:, 3 * quarter:]))
    gates.append(_sigmoid(gate_pre[1]))
    softmax(blocks[3])
    attn.append(weighted(blocks[2]))

    y = y0_ref[0] + y1_ref[0]
    yc = y - _head_sum(y, bd) * (1.0 / HEAD_DIM)
    var = _head_sum(yc * yc, bd, split=False) * (1.0 / HEAD_DIM)
    gates.append(_sigmoid(gate_pre[2]))
    attn.append(weighted(blocks[3]))
    yn = yc * lax.rsqrt(var + GN_EPS) * lnw_ref[...] + lnb_ref[...]
    rw = ((yn + bonus_ref[0]) * g_ref[0]).astype(BF16)
    up_attn = _dot(jnp.concatenate(attn, axis=0), wua_ref[...])
    gates.append(_sigmoid(gate_pre[3]))
    up_rwkv = _dot(rw, wur_ref[...])
    half = d // 2
    merged = (jnp.concatenate(gates[:2], axis=1) * up_attn + jnp.concatenate(gates[2:], axis=1) * up_rwkv).astype(BF16)
    for col in range(0, d, half):
        o_ref[0, :, col:col + half] = x[:, col:col + half] + _dot(merged, wout_ref[:, col:col + half])


def _mid(sink, q, k2, v2, x, gain, wg, y0, y1, bonus, g, lnw, lnb, bd, wua, wur, wout):
    b, s, d = x.shape
    c = RWKV_WIDTH
    tm = ATTN_BLOCK * MID_BLOCKS
    nb = s // tm
    last = s // ATTN_BLOCK - 1
    const = lambda shape: pl.BlockSpec(shape, lambda bi, n: (0,) * len(shape))
    tok = lambda width: pl.BlockSpec((1, tm, width), lambda bi, n: (bi, n, 0))
    prev = pl.BlockSpec((1, ATTN_BLOCK, 2 * KV_WIDTH), lambda bi, n: (bi, jnp.maximum(n * MID_BLOCKS - 1, 0), 0))
    nxt = pl.BlockSpec((1, ATTN_BLOCK, 2 * KV_WIDTH), lambda bi, n: (bi, jnp.minimum((n + 1) * MID_BLOCKS, last), 0))
    kv = [prev, tok(2 * KV_WIDTH), nxt]
    return pl.pallas_call(
        _mid_kernel,
        grid=(b, nb),
        in_specs=[pl.BlockSpec(memory_space=pltpu.SMEM), tok(ATTN_WIDTH)] + kv + kv
                 + [tok(d), const((1, d)), const((d, 2 * d)), tok(c), tok(c), tok(c), tok(c), const((1, c)),
                    const((1, c)), const((LANES, LANES)), const((ATTN_WIDTH, d)), const((c, d)), const((d, d))],
        out_specs=tok(d),
        out_shape=jax.ShapeDtypeStruct((b, s, d), F32),
        compiler_params=_cparams(("parallel", "parallel")),
        name="mid",
    )(sink, q, k2, k2, k2, v2, v2, v2, x, gain, wg, y0, y1, bonus, g, lnw, lnb, bd, wua, wur, wout)


FF_SLAB = 1024


def _ffn_kernel(x_ref, p_ref, gf_ref, w1_ref, w2_ref, gp_ref, wpg_ref, wple_ref, o_ref):
    x = x_ref[0]
    h = _rms(x, gf_ref[...]).astype(BF16)
    acc = x
    for j in range(0, w1_ref.shape[1], FF_SLAB):
        hid = jnp.maximum(_dot(h, w1_ref[:, j:j + FF_SLAB]), 0.0)
        acc = acc + _dot((hid * hid).astype(BF16), w2_ref[j:j + FF_SLAB, :])
    hp = _rms(acc, gp_ref[...]).astype(BF16)
    o_ref[0] = acc + _dot(p_ref[0].astype(BF16), wple_ref[...]) * _sigmoid(_dot(hp, wpg_ref[...]))


def _ffn(x, p, gf, w1, w2, gp, wpg, wple, tm):
    b, s, d = x.shape
    dff = w1.shape[1]
    pd = p.shape[-1]
    const = lambda shape: pl.BlockSpec(shape, lambda bi, i: (0,) * len(shape), pipeline_mode=pl.Buffered(1))
    tok = lambda width: pl.BlockSpec((1, tm, width), lambda bi, i: (bi, i, 0))
    return pl.pallas_call(
        _ffn_kernel,
        grid=(b, s // tm),
        in_specs=[tok(d), tok(pd), const((1, d)), const((d, dff)), const((dff, d)), const((1, d)),
                  const((d, d)), const((pd, d))],
        out_specs=tok(d),
        out_shape=jax.ShapeDtypeStruct((b, s, d), F32),
        compiler_params=_cparams(("parallel", "parallel")),
        name="ffn",
    )(x, p, gf, w1, w2, gp, wpg, wple)


def _rotary_tables(s):
    half = ROT_DIM // 2
    inv_freq = jnp.power(jnp.float32(ROPE_THETA), -jnp.arange(half, dtype=F32) * 2.0 / ROT_DIM)
    ang = jnp.arange(s).astype(F32)[:, None] * inv_freq[None, :]
    cos, sin = jnp.cos(ang), jnp.sin(ang)
    pad = jnp.zeros((s, HEAD_DIM - ROT_DIM), F32)
    zero = jnp.zeros((s, half), F32)
    c = jnp.concatenate([cos, cos, pad + 1.0], axis=1)
    s1 = jnp.concatenate([-sin, zero, pad], axis=1)
    s2 = jnp.concatenate([zero, sin, pad], axis=1)
    rep = LANES // HEAD_DIM
    return jnp.tile(c, (1, rep)), jnp.tile(s1, (1, rep)), jnp.tile(s2, (1, rep))


def kernel(x, p, norm_mix, w_in, shift_mu, q_norm, k_norm, sink, w0, w2, a0, a2, g2, k_k, k_a, r_k, lnx_w, lnx_b,
           w_up_attn, w_up_rwkv, w_out, norm_ffn, w_ff1, w_ff2, norm_ple, w_ple_gate, w_ple):
    bsz, s, d = x.shape
    depth = w_in.shape[0]
    c = RWKV_WIDTH
    tm = min(512, s)
    cos, s1, s2 = _rotary_tables(s)
    lane = jnp.arange(LANES)
    bd = ((lane[:, None] // HEAD_DIM) == (lane[None, :] // HEAD_DIM)).astype(BF16)
    o_g = PROJ_W
    for i in range(depth):
        wi = w_in[i]
        qkg = jnp.concatenate([jnp.tile(q_norm[i], Q_HEADS), jnp.tile(k_norm[i], KV_HEADS)])[None, :]
        zpad = jnp.zeros((DECAY_RANK, c), F32)
        wlo = jnp.concatenate([jnp.concatenate([w2[i, 0], zpad], axis=0), jnp.concatenate([w2[i, 1], zpad], axis=0),
                               jnp.concatenate([zpad, a2[i, 0]], axis=0), jnp.concatenate([zpad, a2[i, 1]], axis=0)],
                              axis=1)
        wlo = (0.5 * wlo).astype(BF16)
        k_mix = jnp.stack([1.0 - 0.5 * k_a[i], 0.5 * k_a[i]])
        full = lambda w: (w, 0, w.shape[1])
        ((q, k2, v2, r, v, a, lw0, lw1, k0, k1, b0, b1, g, bonus), (wg, wua, wur, wo, wf1, wf2, wpg, wpl)) = _front(
            x, norm_mix[i][None, :], wi, cos, s1, s2, qkg, bd, shift_mu[i][None, :], wlo, g2[i].astype(BF16),
            0.5 * w0[i], 0.5 * a0[i], k_k[i][None, :], k_mix, r_k[i].reshape(2, c), tm,
            [(wi, o_g, wi.shape[1]), full(w_up_attn[i]), full(w_up_rwkv[i]), full(w_out[i]), full(w_ff1[i]),
             full(w_ff2[i]), full(w_ple_gate[i]), full(w_ple[i])])
        y0, y1 = _scan(r, v, a, lw0, lw1, k0, k1, b0, b1)
        x = _mid(sink[i], q, k2, v2, x, norm_mix[i][None, :], wg, y0, y1, bonus, g, lnx_w[i][None, :],
                 lnx_b[i][None, :], bd, wua, wur, wo)
        x = _ffn(x, p[i], norm_ffn[i][None, :], wf1, wf2, norm_ple[i][None, :], wpg, wpl, min(1024, s))
    return x
```

```python
import functools
import math

import jax
import jax.numpy as jnp
from jax import lax
from jax.experimental import pallas as pl
from jax.experimental.pallas import tpu as pltpu

F32 = jnp.float32
BF16 = jnp.bfloat16

LANES = 128
HEAD_DIM = 64
Q_HEADS = 8
KV_HEADS = 2
ATTN_WIDTH = Q_HEADS * HEAD_DIM
KV_WIDTH = KV_HEADS * HEAD_DIM
WINDOW = 128
ATTN_BLOCK = 128
ROPE_THETA = 500000.0
ROT_DIM = HEAD_DIM // 4
RWKV_WIDTH = 512
DECAY_RANK = 64
ICLR_RANK = 64
GATE_RANK = 128
RWKV_COLS = 3 * RWKV_WIDTH + DECAY_RANK + ICLR_RANK + GATE_RANK
NORM_EPS = 1e-6
GN_EPS = 64e-5
CHUNK = 64
GROUP_W = 128
SCAN_CHUNKS = 4
VMEM_LIMIT = 56 * 1024 * 1024


def _cparams(sem):
    return pltpu.CompilerParams(dimension_semantics=sem, vmem_limit_bytes=VMEM_LIMIT)


def _dot(a, b):
    return jnp.dot(a, b, preferred_element_type=F32)


def _dot_nt(a, b):
    return lax.dot_general(a, b, (((1,), (1,)), ((), ())), preferred_element_type=F32)


def _dot_tn(a, b):
    return lax.dot_general(a, b, (((0,), (0,)), ((), ())), preferred_element_type=F32)


def _split_dot(x, w_bf16):
    hi = x.astype(BF16)
    lo = (x - hi.astype(F32)).astype(BF16)
    return _dot(hi, w_bf16) + _dot(lo, w_bf16)


def _head_sum(x, bd, split=True):
    one = _split_dot if split else (lambda t, w: _dot(t.astype(BF16), w))
    cols = [one(x[:, i:i + LANES], bd) for i in range(0, x.shape[1], LANES)]
    return cols[0] if len(cols) == 1 else jnp.concatenate(cols, axis=1)


def _rms(x, gain):
    return x * lax.rsqrt(jnp.mean(x * x, axis=-1, keepdims=True) + NORM_EPS) * gain


def _sigmoid(x):
    return 0.5 * jnp.tanh(0.5 * x) + 0.5


QK_W = ATTN_WIDTH + KV_WIDTH
QKV_W = QK_W + KV_WIDTH
PROJ_W = QKV_W + RWKV_COLS
HALO = 8
FRONT_SUB = 128


def _qk_rows(qk, cos, s1, s2, gains, bd):
    ms = _head_sum(qk * qk, bd, split=False) * (1.0 / HEAD_DIM)
    qk = qk * lax.rsqrt(ms + NORM_EPS) * gains
    tiles = []
    for i in range(0, qk.shape[1], LANES):
        t = qk[:, i:i + LANES]
        t = t * cos + pltpu.roll(t, LANES - ROT_DIM // 2, 1) * s1 + pltpu.roll(t, ROT_DIM // 2, 1) * s2
        tiles.append(t)
    return jnp.concatenate(tiles, axis=1)


def _slab_plan(weights, steps, step_index):
    ins, in_specs, out_specs, out_shapes, cols = [], [], [], [], []
    for w, lo, hi in weights:
        rows, width = w.shape
        assert rows % steps == 0, (rows, steps)
        n = rows // steps
        ins.append(w.reshape(steps, n, width))
        in_specs.append(pl.BlockSpec((1, n, width), lambda *g: (step_index(*g), 0, 0)))
        out_specs.append(pl.BlockSpec((1, n, hi - lo), lambda *g: (step_index(*g), 0, 0)))
        out_shapes.append(jax.ShapeDtypeStruct((steps, n, hi - lo), BF16))
        cols.append((lo, hi))
    return ins, in_specs, out_specs, out_shapes, cols


N_FRONT_IN, N_FRONT_OUT = 18, 14


def _front_kernel(cols, *refs):
    nw = len(cols)
    (x_ref, xp_ref, xn_ref, gain_ref, w_ref, cos_ref, s1_ref, s2_ref, qkg_ref, bd_ref, mu_ref, wlo_ref,
     g2_ref, w0_ref, a0_ref, kk_ref, ka_ref, rk_ref) = refs[:N_FRONT_IN]
    (q_ref, k_ref, v_ref, r_ref, vv_ref, a_ref, lw0_ref, lw1_ref, k0_ref, k1_ref, b0_ref, b1_ref, g_ref,
     bonus_ref) = refs[N_FRONT_IN + nw:N_FRONT_IN + nw + N_FRONT_OUT]
    for src_ref, dst_ref, (lo, hi) in zip(refs[N_FRONT_IN:], refs[N_FRONT_IN + nw + N_FRONT_OUT:], cols):
        dst_ref[...] = src_ref[:, :, lo:hi].astype(BF16)
    i = pl.program_id(1)
    nt = pl.num_programs(1)
    tm = x_ref.shape[1]
    sub = FRONT_SUB
    c = RWKV_WIDTH
    gain = gain_ref[...]
    bd = bd_ref[...]
    o_r, o_k, o_v, o_low = QKV_W, QKV_W + c, QKV_W + 2 * c, QKV_W + 3 * c
    halo = jnp.concatenate([xp_ref[0], xn_ref[0]], axis=0)
    w_low = w_ref[:, o_low:PROJ_W].astype(BF16)
    h_blocks = [_rms(halo, gain).astype(BF16)]
    low_blocks = [_dot(h_blocks[0], w_low)]
    for s in range(0, tm, sub):
        h_blocks.insert(-1, _rms(x_ref[0, s:s + sub, :], gain).astype(BF16))
        low_blocks.insert(-1, _dot(h_blocks[-2], w_low))
    h_ext = jnp.concatenate(h_blocks, axis=0)
    h = h_ext[:tm]
    z_low = jnp.concatenate(low_blocks, axis=0)

    row = lax.broadcasted_iota(jnp.int32, (HALO, 1), 0)
    ri = lax.broadcasted_iota(jnp.int32, (sub, sub), 0)
    ci = lax.broadcasted_iota(jnp.int32, (sub, sub), 1)
    adjacent = ((ri - ci == 1) | (ci - ri == 1)).astype(BF16)

    def project(lo, hi):
        return _dot(h_ext if lo >= QKV_W else h, w_ref[:, lo:hi].astype(BF16))

    def shift(z, lo, hi):
        mu = mu_ref[:, lo:hi]
        mix = 0.5 * mu
        keep = 1.0 - mu
        prev_row = jnp.where(i > 0, z[tm + HALO - 1:tm + HALO], 0.0)
        next_row = jnp.where(i < nt - 1, z[tm + HALO:tm + HALO + 1], 0.0)
        blocks = []
        for s in range(0, tm, sub):
            zs = z[s:s + sub]
            part = zs * keep + _dot(adjacent, zs.astype(BF16)) * mix
            before = z[s - 1:s] if s > 0 else prev_row
            after = z[s + sub:s + sub + 1] if s + sub < tm else next_row
            blocks += [part[:HALO] + jnp.where(row == 0, before * mix, 0.0), part[HALO:sub - HALO],
                       part[sub - HALO:] + jnp.where(row == HALO - 1, after * mix, 0.0)]
        return jnp.concatenate(blocks, axis=0)

    half = c // 2
    cos, s1, s2 = cos_ref[...], s1_ref[...], s2_ref[...]

    def q_rows(p, j):
        q = _qk_rows(p, cos, s1, s2, qkg_ref[:, j * half:(j + 1) * half], bd)
        q_ref[0, :, j * half:(j + 1) * half] = (q * (HEAD_DIM ** -0.5)).astype(BF16)

    z_k0 = project(o_k, o_k + half)
    z_low = shift(z_low, 3 * c, RWKV_COLS)
    lowrank = z_low[:, :LANES]
    left = lax.broadcasted_iota(jnp.int32, (1, LANES), 1) < DECAY_RANK
    lowrank = jnp.where(left, jnp.tanh(lowrank), lowrank).astype(BF16)
    gate_in = _sigmoid(z_low[:, LANES:]).astype(BF16)

    z_k1 = project(o_k + half, o_k + c)
    wa = _dot(lowrank, wlo_ref[...])
    g_ref[0] = _dot(gate_in, g2_ref[...])
    p_q0 = project(0, half)

    lw_scale = -0.5 * math.exp(-0.5)
    lw0_ref[0] = lw_scale * jnp.tanh(w0_ref[0:1, :] + wa[:, :c]) + lw_scale
    p_q1 = project(half, c)
    lw1_ref[0] = lw_scale * jnp.tanh(w0_ref[1:2, :] + wa[:, c:2 * c]) + lw_scale
    z_r0 = project(o_r, o_r + half)

    k = shift(jnp.concatenate([z_k0, z_k1], axis=1), c, 2 * c)
    kk = k * kk_ref[...]
    kk = kk * lax.rsqrt(jnp.maximum(_head_sum(kk * kk, bd, split=False), 1e-24))
    a_ref[0] = kk.astype(BF16)
    kk_half = 0.5 * kk
    z_r1 = project(o_r + half, o_r + c)

    later = [lambda: project(o_v, o_v + half), lambda: project(o_v + half, o_v + c)]
    z_v = []
    dot_rk = None
    for d, (kd_ref, bdir_ref) in enumerate(((k0_ref, b0_ref), (k1_ref, b1_ref))):
        t = jnp.tanh(a0_ref[d:d + 1, :] + wa[:, (2 + d) * c:(3 + d) * c])
        k_dir = k * (ka_ref[0:1, :] + ka_ref[1:2, :] * t)
        kd_ref[0] = k_dir.astype(BF16)
        bdir_ref[0] = (kk_half * t + kk_half).astype(BF16)
        term = k_dir * rk_ref[d:d + 1, :]
        dot_rk = term if dot_rk is None else dot_rk + term
        z_v.append(later[d]())
        q_rows((p_q0, p_q1)[d], d)

    p_kv = project(ATTN_WIDTH, QKV_W)
    r = shift(jnp.concatenate([z_r0, z_r1], axis=1), 0, c)
    r_ref[0] = r.astype(BF16)
    r_dot = _head_sum(r * dot_rk, bd, split=False)
    k_att = _qk_rows(p_kv[:, :KV_WIDTH], cos, s1, s2, qkg_ref[:, ATTN_WIDTH:], bd)
    k_ref[0] = jnp.concatenate([k_att, pltpu.roll(k_att, HEAD_DIM, 1)], axis=1).astype(BF16)
    v = shift(jnp.concatenate(z_v, axis=1), 2 * c, 3 * c)
    vv_ref[0] = v.astype(BF16)
    bonus_ref[0] = r_dot * v
    v_att = p_kv[:, KV_WIDTH:]
    v_ref[0] = jnp.concatenate([v_att, pltpu.roll(v_att, HEAD_DIM, 1)], axis=1).astype(BF16)


def _front(x, gain, w, cos, s1, s2, qkg, bd, mu, wlo, g2, w0, a0, k_k, k_a, r_k, tm, weights):
    b, s, d = x.shape
    nt = s // tm
    w_ins, w_in_specs, w_out_specs, w_out_shapes, cols = _slab_plan(weights, b * nt, lambda bi, i: bi * nt + i)
    c = RWKV_WIDTH
    nh = tm // HALO
    const = lambda shape: pl.BlockSpec(shape, lambda bi, i: (0,) * len(shape))
    tok = lambda width: pl.BlockSpec((1, tm, width), lambda bi, i: (bi, i, 0))
    tab = pl.BlockSpec((tm, LANES), lambda bi, i: (i, 0))
    halo_p = pl.BlockSpec((1, HALO, d), lambda bi, i: (bi, jnp.maximum(i * nh - 1, 0), 0))
    halo_n = pl.BlockSpec((1, HALO, d), lambda bi, i: (bi, jnp.minimum((i + 1) * nh, s // HALO - 1), 0))
    rwkv_dtypes = (BF16,) * 3 + (F32,) * 2 + (BF16,) * 4 + (F32,) * 2
    outs = pl.pallas_call(
        functools.partial(_front_kernel, cols),
        grid=(b, nt),
        in_specs=[tok(d), halo_p, halo_n, const((1, d)), const((d, PROJ_W)), tab, tab, tab, const((1, QK_W)),
                  const((LANES, LANES)), const((1, RWKV_COLS)), const((LANES, 4 * c)), const((GATE_RANK, c)),
                  const((2, c)), const((2, c)), const((1, c)), const((2, c)), const((2, c))] + w_in_specs,
        out_specs=[tok(ATTN_WIDTH), tok(2 * KV_WIDTH), tok(2 * KV_WIDTH)] + [tok(c)] * 11 + w_out_specs,
        out_shape=[jax.ShapeDtypeStruct((b, s, ATTN_WIDTH), BF16),
                   jax.ShapeDtypeStruct((b, s, 2 * KV_WIDTH), BF16),
                   jax.ShapeDtypeStruct((b, s, 2 * KV_WIDTH), BF16)]
                  + [jax.ShapeDtypeStruct((b, s, c), dt) for dt in rwkv_dtypes] + w_out_shapes,
        compiler_params=_cparams(("parallel", "parallel")),
        name="front",
    )(x, x, x, gain, w, cos, s1, s2, qkg, bd, mu, wlo, g2, w0, a0, k_k, k_a, r_k, *w_ins)
    converted = [o.reshape(wt.shape[0], hi - lo) for o, (wt, lo, hi) in zip(outs[N_FRONT_OUT:], weights)]
    return outs[:N_FRONT_OUT], converted


def _scan_kernel(r0_ref, v0_ref, a0_ref, lw0_ref, k0_ref, b0_ref,
                 r1_ref, v1_ref, a1_ref, lw1_ref, k1_ref, b1_ref,
                 y0_ref, y1_ref, ht_ref):
    L, gw, nch = CHUNK, GROUP_W, SCAN_CHUNKS
    rep = gw // L
    ng = RWKV_WIDTH // gw

    @pl.when(pl.program_id(1) == 0)
    def _():
        ht_ref[...] = jnp.zeros_like(ht_ref)

    tau = lax.broadcasted_iota(jnp.int32, (L, gw), 0)
    sig = lax.broadcasted_iota(jnp.int32, (L, gw), 1) % L
    bd = (lax.broadcasted_iota(jnp.int32, (gw, gw), 0) // L) == (lax.broadcasted_iota(jnp.int32, (gw, gw), 1) // L)
    half = 2
    ngrp = nch // half
    hrows = half * L
    ti = lax.broadcasted_iota(jnp.int32, (hrows, hrows), 0)
    si = lax.broadcasted_iota(jnp.int32, (hrows, hrows), 1)
    same_chunk = (ti // L) == (si // L)
    tris = ((same_chunk & (si <= ti)).astype(BF16), (same_chunk & (si >= ti)).astype(BF16))
    masks = ((sig < tau, sig <= tau), (sig > tau, sig >= tau))

    def bdtile(x):
        xb = x.astype(BF16)
        return jnp.where(bd, jnp.concatenate([xb] * rep, axis=0), jnp.zeros((), BF16))

    dirs = ((r0_ref, v0_ref, a0_ref, lw0_ref, k0_ref, b0_ref),
            (r1_ref, v1_ref, a1_ref, lw1_ref, k1_ref, b1_ref))

    def build(grp):
        out = {}
        for d, (r_ref, v_ref, a_ref, lw_ref, k_ref, b_ref) in enumerate(dirs):
            base = (grp if d == 0 else ngrp - 1 - grp) * hrows
            rsl = slice(base, base + hrows)
            strict, incl = masks[d]
            lw = lw_ref[0, rsl, :]
            cl = _split_dot_left(tris[d], lw)
            e_in = jnp.exp(cl)
            e_inv = jnp.exp(-cl)
            e_ex = jnp.exp(cl - lw)
            a_t = -(a_ref[0, rsl, :].astype(F32) * e_ex)
            r_t = r_ref[0, rsl, :].astype(F32) * e_in
            b_t = b_ref[0, rsl, :].astype(F32) * e_inv
            k_t = k_ref[0, rsl, :].astype(F32) * e_inv
            v_all = v_ref[0, rsl, :]
            for jj in range(half):
                rs = slice(jj * L, (jj + 1) * L)
                last = (jj + 1) * L - 1 if d == 0 else jj * L
                d_end = jnp.exp(cl[last:last + 1, :])
                b_h = b_t[rs] * d_end
                k_h = k_t[rs] * d_end
                for g in range(ng):
                    sl = slice(g * gw, (g + 1) * gw)
                    out[(base // L + jj, d, g)] = dict(
                        strict=strict, incl=incl, at=a_t[rs, sl], rt=r_t[rs, sl], vv=v_all[rs, sl], bt=b_t[rs, sl],
                        kt=k_t[rs, sl], bh=b_h[:, sl], kh=k_h[:, sl], dend=d_end[:, sl])
        return out

    def s_gram(par):
        for c in par:
            lhs = jnp.concatenate([c["at"], c["rt"]], axis=0).astype(BF16)
            gram = _dot_nt(lhs, jnp.concatenate([bdtile(c["bt"]), bdtile(c["kt"])], axis=0))
            c["gb"], c["gk"] = gram[:, :gw], gram[:, gw:]

    def s_vprod(par):
        for c in par:
            c["m_rb"] = jnp.where(c["incl"], c["gb"][L:], 0.0).astype(BF16)
            c["p"] = jnp.where(c["strict"], c["gb"][:L], 0.0)
            m_k = jnp.concatenate([jnp.where(c["strict"], c["gk"][:L], 0.0), jnp.where(c["incl"], c["gk"][L:], 0.0)],
                                  axis=0)
            res = _dot(m_k.astype(BF16), bdtile(c["vv"]))
            c["makv"], c["mrkv"] = res[:L], res[L:]

    levels = int(math.log2(L))

    def s_level(lvl):
        def run(par):
            for c in par:
                ops = ([] if lvl == 0 else [bdtile(c["t"])]) + ([] if lvl == levels - 1 else [bdtile(c["p"])])
                c["res"] = _dot(c["p"].astype(BF16), ops[0] if len(ops) == 1 else jnp.concatenate(ops, axis=1))
            for c in par:
                if lvl == 0:
                    c["t"] = jnp.where(sig == tau, 1.0, c["p"])
                    c["p"] = c["res"]
                else:
                    c["t"] = c["t"] + c["res"][:, :gw]
                    if lvl < levels - 1:
                        c["p"] = c["res"][:, gw:]
        return run

    def s_apply(par):
        for c in par:
            res = _dot(c["t"].astype(BF16), jnp.concatenate([bdtile(c["at"]), bdtile(c["makv"])], axis=1))
            c["ah"], c["uloc"] = res[:, :gw], res[:, gw:]

    def s_g(par):
        for c in par:
            c["bkh"] = jnp.concatenate([c["bh"], c["kh"]], axis=0).astype(BF16)
            c["g"] = jnp.where(bd, _dot_tn(c["ah"].astype(BF16), c["bkh"][:L]), 0.0).astype(BF16)

    def s_c(par):
        for c in par:
            c["cst"] = jnp.where(bd, _dot_tn(jnp.concatenate([c["uloc"].astype(BF16), c["vv"]], axis=0), c["bkh"]),
                                 0.0)

    def s_q(par):
        for c in par:
            res = _dot(c["m_rb"], jnp.concatenate([bdtile(c["ah"]), bdtile(c["uloc"])], axis=1))
            c["qh"] = (c["rt"] + res[:, :gw]).astype(BF16)
            c["yloc"] = res[:, gw:] + c["mrkv"]

    stages = [s_gram, s_vprod] + [s_level(lvl) for lvl in range(levels)] + [s_apply, s_g, s_c, s_q]

    state = {(d, g): ht_ref[d, g] for d in range(2) for g in range(ng)}
    recs = {}

    def seq_step(step):
        cur = [(d, g, recs[(step if d == 0 else nch - 1 - step, d, g)]) for d in range(2) for g in range(ng)]
        for d, g, c in cur:
            c["htb"] = state[(d, g)].astype(BF16)
        for d, g, c in cur:
            state[(d, g)] = state[(d, g)] * c["dend"] + c["cst"] + _dot(c["htb"], c["g"])
        for d, g, c in cur:
            c["y"] = c["yloc"] + _dot_nt(c["qh"], c["htb"])

    slots = {(i + 1) * len(stages) // (half + 1): i for i in range(half)}
    groups = [build(0)]
    recs.update(groups[0])
    for k in range(ngrp):
        for s, fn in enumerate(stages):
            if s == 2 and k + 1 < ngrp:
                groups.append(build(k + 1))
                recs.update(groups[k + 1])
            fn(list(groups[k].values()))
            if k > 0 and s in slots:
                seq_step((k - 1) * half + slots[s])
    for step in range(nch - half, nch):
        seq_step(step)
    for (d, g), h in state.items():
        ht_ref[d, g] = h
    for d, y_ref in enumerate((y0_ref, y1_ref)):
        y_ref[0] = jnp.concatenate(
            [jnp.concatenate([recs[(j, d, g)]["y"] for g in range(ng)], axis=1) for j in range(nch)], axis=0)


def _split_dot_left(w_bf16, x):
    hi = x.astype(BF16)
    lo = (x - hi.astype(F32)).astype(BF16)
    return _dot(w_bf16, hi) + _dot(w_bf16, lo)


def _scan(r, v, a, lw0, lw1, k0, k1, b0, b1):
    b, s, c = r.shape
    rows = CHUNK * SCAN_CHUNKS
    nblk = s // rows
    fwd = pl.BlockSpec((1, rows, c), lambda bi, i: (bi, i, 0))
    bwd = pl.BlockSpec((1, rows, c), lambda bi, i: (bi, nblk - 1 - i, 0))
    ng = c // GROUP_W
    return pl.pallas_call(
        _scan_kernel,
        grid=(b, nblk),
        in_specs=[fwd] * 6 + [bwd] * 6,
        out_specs=[fwd, bwd],
        out_shape=[jax.ShapeDtypeStruct((b, s, c), F32)] * 2,
        scratch_shapes=[pltpu.VMEM((2, ng, GROUP_W, GROUP_W), F32)],
        compiler_params=_cparams(("parallel", "arbitrary")),
        name="scan",
    )(r, v, a, lw0, k0, b0, r, v, a, lw1, k1, b1)


MID_BLOCKS = 4


def _mid_kernel(sink_ref, q_ref, kp_ref, kc_ref, kn_ref, vp_ref, vc_ref, vn_ref, x_ref, gain_ref, wg_ref, y0_ref,
                y1_ref, bonus_ref, g_ref, lnw_ref, lnb_ref, bd_ref, wua_ref, wur_ref, wout_ref, o_ref):
    n = pl.program_id(1)
    nb = pl.num_programs(1)
    blk, nsub = ATTN_BLOCK, MID_BLOCKS
    d = x_ref.shape[2]
    k = jnp.concatenate([kp_ref[0], kc_ref[0], kn_ref[0]], axis=0)
    v = jnp.concatenate([vp_ref[0], vc_ref[0], vn_ref[0]], axis=0)
    qi = lax.broadcasted_iota(jnp.int32, (2 * blk, blk), 0) % blk
    ki = lax.broadcasted_iota(jnp.int32, (2 * blk, blk), 1)
    first = lax.broadcasted_iota(jnp.int32, (2 * blk, 1), 0) < blk
    left = lax.broadcasted_iota(jnp.int32, (1, LANES), 1) < HEAD_DIM
    zero = jnp.zeros((), BF16)
    bd = bd_ref[...]

    def scores(j):
        rows = slice(j * blk, (j + 1) * blk)
        win = slice(j * blk, (j + 3) * blk)
        mask_prev = (ki >= qi) & ((n > 0) if j == 0 else True)
        mask_next = (ki <= qi) & ((n < nb - 1) if j == nsub - 1 else True)
        recs = []
        for g in range(KV_HEADS):
            pairs = jnp.concatenate([q_ref[0, rows, p * LANES:(p + 1) * LANES] for p in (2 * g, 2 * g + 1)], axis=0)
            for side in range(2):
                off = LANES * ((g + side) % 2)
                keep = left if side == 0 else jnp.logical_not(left)
                sk = jnp.where(first, sink_ref[4 * g + side], sink_ref[4 * g + 2 + side])
                sc = _dot_nt(jnp.where(keep, pairs, zero), k[win, off:off + LANES])
                sc = jnp.concatenate([jnp.where(mask_prev, sc[:, :blk], -1e30), sc[:, blk:2 * blk],
                                      jnp.where(mask_next, sc[:, 2 * blk:], -1e30)], axis=1)
                recs.append(dict(sc=sc, sk=sk, v=jnp.where(keep, v[win, off:off + LANES], zero)))
        return recs

    def softmax(recs):
        for c in recs:
            m = jnp.maximum(jnp.max(c["sc"], axis=-1, keepdims=True), c["sk"])
            e = jnp.exp(c["sc"] - m)
            c["den"] = jnp.sum(e, axis=-1, keepdims=True) + jnp.exp(c["sk"] - m)
            c["e"] = e.astype(BF16)

    def weighted(recs):
        outs = []
        for g in range(KV_HEADS):
            both = sum(_dot(c["e"], c["v"]) / c["den"] for c in recs[2 * g:2 * g + 2])
            outs += [both[:blk], both[blk:]]
        return jnp.concatenate(outs, axis=1).astype(BF16)

    quarter = wg_ref.shape[1] // 4
    x = x_ref[0]
    blocks = [scores(0), scores(1)]
    h = _rms(x, gain_ref[...]).astype(BF16)
    gate_pre = [_dot(h, wg_ref[:, :quarter])]
    softmax(blocks[0])
    gate_pre.append(_dot(h, wg_ref[:, quarter:2 * quarter]))
    softmax(blocks[1])
    attn = [weighted(blocks[0])]
    blocks.append(scores(2))
    gates = [_sigmoid(gate_pre[0])]
    gate_pre.append(_dot(h, wg_ref[:, 2 * quarter:3 * quarter]))
    attn.append(weighted(blocks[1]))
    blocks.append(scores(3))
    softmax(blocks[2])
    gate_pre.append(_dot(h, wg_ref[:, 3 * quarter:]))
    gates.append(_sigmoid(gate_pre[1]))
    softmax(blocks[3])
    attn.append(weighted(blocks[2]))

    y = y0_ref[0] + y1_ref[0]
    yc = y - _head_sum(y, bd) * (1.0 / HEAD_DIM)
    var = _head_sum(yc * yc, bd, split=False) * (1.0 / HEAD_DIM)
    gates.append(_sigmoid(gate_pre[2]))
    attn.append(weighted(blocks[3]))
    yn = yc * lax.rsqrt(var + GN_EPS) * lnw_ref[...] + lnb_ref[...]
    rw = ((yn + bonus_ref[0]) * g_ref[0]).astype(BF16)
    up_attn = _dot(jnp.concatenate(attn, axis=0), wua_ref[...])
    gates.append(_sigmoid(gate_pre[3]))
    up_rwkv = _dot(rw, wur_ref[...])
    half = d // 2
    merged = (jnp.concatenate(gates[:2], axis=1) * up_attn + jnp.concatenate(gates[2:], axis=1) * up_rwkv).astype(BF16)
    for col in range(0, d, half):
        o_ref[0, :, col:col + half] = x[:, col:col + half] + _dot(merged, wout_ref[:, col:col + half])


def _mid(sink, q, k2, v2, x, gain, wg, y0, y1, bonus, g, lnw, lnb, bd, wua, wur, wout):
    b, s, d = x.shape
    c = RWKV_WIDTH
    tm = ATTN_BLOCK * MID_BLOCKS
    nb = s // tm
    last = s // ATTN_BLOCK - 1
    const = lambda shape: pl.BlockSpec(shape, lambda bi, n: (0,) * len(shape))
    tok = lambda width: pl.BlockSpec((1, tm, width), lambda bi, n: (bi, n, 0))
    prev = pl.BlockSpec((1, ATTN_BLOCK, 2 * KV_WIDTH), lambda bi, n: (bi, jnp.maximum(n * MID_BLOCKS - 1, 0), 0))
    nxt = pl.BlockSpec((1, ATTN_BLOCK, 2 * KV_WIDTH), lambda bi, n: (bi, jnp.minimum((n + 1) * MID_BLOCKS, last), 0))
    kv = [prev, tok(2 * KV_WIDTH), nxt]
    return pl.pallas_call(
        _mid_kernel,
        grid=(b, nb),
        in_specs=[pl.BlockSpec(memory_space=pltpu.SMEM), tok(ATTN_WIDTH)] + kv + kv
                 + [tok(d), const((1, d)), const((d, 2 * d)), tok(c), tok(c), tok(c), tok(c), const((1, c)),
                    const((1, c)), const((LANES, LANES)), const((ATTN_WIDTH, d)), const((c, d)), const((d, d))],
        out_specs=tok(d),
        out_shape=jax.ShapeDtypeStruct((b, s, d), F32),
        compiler_params=_cparams(("parallel", "parallel")),
        name="mid",
    )(sink, q, k2, k2, k2, v2, v2, v2, x, gain, wg, y0, y1, bonus, g, lnw, lnb, bd, wua, wur, wout)


FF_SLAB = 1024


def _ffn_kernel(x_ref, p_ref, gf_ref, w1_ref, w2_ref, gp_ref, wpg_ref, wple_ref, o_ref):
    x = x_ref[0]
    h = _rms(x, gf_ref[...]).astype(BF16)
    acc = x
    for j in range(0, w1_ref.shape[1], FF_SLAB):
        hid = jnp.maximum(_dot(h, w1_ref[:, j:j + FF_SLAB]), 0.0)
        acc = acc + _dot((hid * hid).astype(BF16), w2_ref[j:j + FF_SLAB, :])
    hp = _rms(acc, gp_ref[...]).astype(BF16)
    o_ref[0] = acc + _dot(p_ref[0].astype(BF16), wple_ref[...]) * _sigmoid(_dot(hp, wpg_ref[...]))


def _ffn(x, p, gf, w1, w2, gp, wpg, wple, tm):
    b, s, d = x.shape
    dff = w1.shape[1]
    pd = p.shape[-1]
    const = lambda shape: pl.BlockSpec(shape, lambda bi, i: (0,) * len(shape), pipeline_mode=pl.Buffered(1))
    tok = lambda width: pl.BlockSpec((1, tm, width), lambda bi, i: (bi, i, 0))
    return pl.pallas_call(
        _ffn_kernel,
        grid=(b, s // tm),
        in_specs=[tok(d), tok(pd), const((1, d)), const((d, dff)), const((dff, d)), const((1, d)),
                  const((d, d)), const((pd, d))],
        out_specs=tok(d),
        out_shape=jax.ShapeDtypeStruct((b, s, d), F32),
        compiler_params=_cparams(("parallel", "parallel")),
        name="ffn",
    )(x, p, gf, w1, w2, gp, wpg, wple)


def _rotary_tables(s):
    half = ROT_DIM // 2
    inv_freq = jnp.power(jnp.float32(ROPE_THETA), -jnp.arange(half, dtype=F32) * 2.0 / ROT_DIM)
    ang = jnp.arange(s).astype(F32)[:, None] * inv_freq[None, :]
    cos, sin = jnp.cos(ang), jnp.sin(ang)
    pad = jnp.zeros((s, HEAD_DIM - ROT_DIM), F32)
    zero = jnp.zeros((s, half), F32)
    c = jnp.concatenate([cos, cos, pad + 1.0], axis=1)
    s1 = jnp.concatenate([-sin, zero, pad], axis=1)
    s2 = jnp.concatenate([zero, sin, pad], axis=1)
    rep = LANES // HEAD_DIM
    return jnp.tile(c, (1, rep)), jnp.tile(s1, (1, rep)), jnp.tile(s2, (1, rep))


def kernel(x, p, norm_mix, w_in, shift_mu, q_norm, k_norm, sink, w0, w2, a0, a2, g2, k_k, k_a, r_k, lnx_w, lnx_b,
           w_up_attn, w_up_rwkv, w_out, norm_ffn, w_ff1, w_ff2, norm_ple, w_ple_gate, w_ple):
    bsz, s, d = x.shape
    depth = w_in.shape[0]
    c = RWKV_WIDTH
    tm = min(512, s)
    cos, s1, s2 = _rotary_tables(s)
    lane = jnp.arange(LANES)
    bd = ((lane[:, None] // HEAD_DIM) == (lane[None, :] // HEAD_DIM)).astype(BF16)
    o_g = PROJ_W
    for i in range(depth):
        wi = w_in[i]
        qkg = jnp.concatenate([jnp.tile(q_norm[i], Q_HEADS), jnp.tile(k_norm[i], KV_HEADS)])[None, :]
        zpad = jnp.zeros((DECAY_RANK, c), F32)
        wlo = jnp.concatenate([jnp.concatenate([w2[i, 0], zpad], axis=0), jnp.concatenate([w2[i, 1], zpad], axis=0),
                               jnp.concatenate([zpad, a2[i, 0]], axis=0), jnp.concatenate([zpad, a2[i, 1]], axis=0)],
                              axis=1)
        wlo = (0.5 * wlo).astype(BF16)
        k_mix = jnp.stack([1.0 - 0.5 * k_a[i], 0.5 * k_a[i]])
        full = lambda w: (w, 0, w.shape[1])
        ((q, k2, v2, r, v, a, lw0, lw1, k0, k1, b0, b1, g, bonus), (wg, wua, wur, wo, wf1, wf2, wpg, wpl)) = _front(
            x, norm_mix[i][None, :], wi, cos, s1, s2, qkg, bd, shift_mu[i][None, :], wlo, g2[i].astype(BF16),
            0.5 * w0[i], 0.5 * a0[i], k_k[i][None, :], k_mix, r_k[i].reshape(2, c), tm,
            [(wi, o_g, wi.shape[1]), full(w_up_attn[i]), full(w_up_rwkv[i]), full(w_out[i]), full(w_ff1[i]),
             full(w_ff2[i]), full(w_ple_gate[i]), full(w_ple[i])])
        y0, y1 = _scan(r, v, a, lw0, lw1, k0, k1, b0, b1)
        x = _mid(sink[i], q, k2, v2, x, norm_mix[i][None, :], wg, y0, y1, bonus, g, lnx_w[i][None, :],
                 lnx_b[i][None, :], bd, wua, wur, wo)
        x = _ffn(x, p[i], norm_ffn[i][None, :], wf1, wf2, norm_ple[i][None, :], wpg, wpl, min(1024, s))
    return x
```

```python
import functools
import math

import jax
import jax.numpy as jnp
from jax import lax
from jax.experimental import pallas as pl
from jax.experimental.pallas import tpu as pltpu

F32 = jnp.float32
BF16 = jnp.bfloat16

LANES = 128
HEAD_DIM = 64
Q_HEADS = 8
KV_HEADS = 2
ATTN_WIDTH = Q_HEADS * HEAD_DIM
KV_WIDTH = KV_HEADS * HEAD_DIM
WINDOW = 128
ATTN_BLOCK = 128
ROPE_THETA = 500000.0
ROT_DIM = HEAD_DIM // 4
RWKV_WIDTH = 512
DECAY_RANK = 64
ICLR_RANK = 64
GATE_RANK = 128
RWKV_COLS = 3 * RWKV_WIDTH + DECAY_RANK + ICLR_RANK + GATE_RANK
NORM_EPS = 1e-6
GN_EPS = 64e-5
CHUNK = 64
GROUP_W = 128
SCAN_CHUNKS = 8
VMEM_LIMIT = 63 * 1024 * 1024


def _cparams(sem):
    return pltpu.CompilerParams(dimension_semantics=sem, vmem_limit_bytes=VMEM_LIMIT)


def _dot(a, b):
    return jnp.dot(a, b, preferred_element_type=F32)


def _dot_nt(a, b):
    return lax.dot_general(a, b, (((1,), (1,)), ((), ())), preferred_element_type=F32)


def _dot_tn(a, b):
    return lax.dot_general(a, b, (((0,), (0,)), ((), ())), preferred_element_type=F32)


def _split_dot(x, w_bf16):
    hi = x.astype(BF16)
    lo = (x - hi.astype(F32)).astype(BF16)
    return _dot(hi, w_bf16) + _dot(lo, w_bf16)


def _head_sum(x, bd, split=True):
    one = _split_dot if split else (lambda t, w: _dot(t.astype(BF16), w))
    cols = [one(x[:, i:i + LANES], bd) for i in range(0, x.shape[1], LANES)]
    return cols[0] if len(cols) == 1 else jnp.concatenate(cols, axis=1)


def _rms(x, gain):
    return x * lax.rsqrt(jnp.mean(x * x, axis=-1, keepdims=True) + NORM_EPS) * gain


def _sigmoid(x):
    return 0.5 * jnp.tanh(0.5 * x) + 0.5


QK_W = ATTN_WIDTH + KV_WIDTH
QKV_W = QK_W + KV_WIDTH
PROJ_W = QKV_W + RWKV_COLS
HALO = 8
FRONT_SUB = 128


def _qk_rows(qk, cos, s1, s2, gains, bd):
    ms = _head_sum(qk * qk, bd, split=False) * (1.0 / HEAD_DIM)
    qk = qk * lax.rsqrt(ms + NORM_EPS) * gains
    tiles = []
    for i in range(0, qk.shape[1], LANES):
        t = qk[:, i:i + LANES]
        t = t * cos + pltpu.roll(t, LANES - ROT_DIM // 2, 1) * s1 + pltpu.roll(t, ROT_DIM // 2, 1) * s2
        tiles.append(t)
    return jnp.concatenate(tiles, axis=1)


def _slab_plan(weights, steps, step_index):
    ins, in_specs, out_specs, out_shapes, cols = [], [], [], [], []
    for w, lo, hi in weights:
        rows, width = w.shape
        assert rows % steps == 0, (rows, steps)
        n = rows // steps
        ins.append(w.reshape(steps, n, width))
        in_specs.append(pl.BlockSpec((1, n, width), lambda *g: (step_index(*g), 0, 0)))
        out_specs.append(pl.BlockSpec((1, n, hi - lo), lambda *g: (step_index(*g), 0, 0)))
        out_shapes.append(jax.ShapeDtypeStruct((steps, n, hi - lo), BF16))
        cols.append((lo, hi))
    return ins, in_specs, out_specs, out_shapes, cols


N_FRONT_IN, N_FRONT_OUT = 18, 14


def _front_kernel(cols, *refs):
    nw = len(cols)
    (x_ref, xp_ref, xn_ref, gain_ref, w_ref, cos_ref, s1_ref, s2_ref, qkg_ref, bd_ref, mu_ref, wlo_ref,
     g2_ref, w0_ref, a0_ref, kk_ref, ka_ref, rk_ref) = refs[:N_FRONT_IN]
    (q_ref, k_ref, v_ref, r_ref, vv_ref, a_ref, lw0_ref, lw1_ref, k0_ref, k1_ref, b0_ref, b1_ref, g_ref,
     bonus_ref) = refs[N_FRONT_IN + nw:N_FRONT_IN + nw + N_FRONT_OUT]
    for src_ref, dst_ref, (lo, hi) in zip(refs[N_FRONT_IN:], refs[N_FRONT_IN + nw + N_FRONT_OUT:], cols):
        dst_ref[...] = src_ref[:, :, lo:hi].astype(BF16)
    i = pl.program_id(1)
    nt = pl.num_programs(1)
    tm = x_ref.shape[1]
    sub = FRONT_SUB
    c = RWKV_WIDTH
    gain = gain_ref[...]
    bd = bd_ref[...]
    o_r, o_k, o_v, o_low = QKV_W, QKV_W + c, QKV_W + 2 * c, QKV_W + 3 * c
    halo = jnp.concatenate([xp_ref[0], xn_ref[0]], axis=0)
    w_low = w_ref[:, o_low:PROJ_W].astype(BF16)
    h_blocks = [_rms(halo, gain).astype(BF16)]
    low_blocks = [_dot(h_blocks[0], w_low)]
    for s in range(0, tm, sub):
        h_blocks.insert(-1, _rms(x_ref[0, s:s + sub, :], gain).astype(BF16))
        low_blocks.insert(-1, _dot(h_blocks[-2], w_low))
    h_ext = jnp.concatenate(h_blocks, axis=0)
    h = h_ext[:tm]
    z_low = jnp.concatenate(low_blocks, axis=0)

    row = lax.broadcasted_iota(jnp.int32, (HALO, 1), 0)
    ri = lax.broadcasted_iota(jnp.int32, (sub, sub), 0)
    ci = lax.broadcasted_iota(jnp.int32, (sub, sub), 1)
    adjacent = ((ri - ci == 1) | (ci - ri == 1)).astype(BF16)

    def project(lo, hi):
        return _dot(h_ext if lo >= QKV_W else h, w_ref[:, lo:hi].astype(BF16))

    def shift(z, lo, hi):
        mu = mu_ref[:, lo:hi]
        mix = 0.5 * mu
        keep = 1.0 - mu
        prev_row = jnp.where(i > 0, z[tm + HALO - 1:tm + HALO], 0.0)
        next_row = jnp.where(i < nt - 1, z[tm + HALO:tm + HALO + 1], 0.0)
        blocks = []
        for s in range(0, tm, sub):
            zs = z[s:s + sub]
            part = zs * keep + _dot(adjacent, zs.astype(BF16)) * mix
            before = z[s - 1:s] if s > 0 else prev_row
            after = z[s + sub:s + sub + 1] if s + sub < tm else next_row
            blocks += [part[:HALO] + jnp.where(row == 0, before * mix, 0.0), part[HALO:sub - HALO],
                       part[sub - HALO:] + jnp.where(row == HALO - 1, after * mix, 0.0)]
        return jnp.concatenate(blocks, axis=0)

    half = c // 2
    cos, s1, s2 = cos_ref[...], s1_ref[...], s2_ref[...]

    def q_rows(p, j):
        q = _qk_rows(p, cos, s1, s2, qkg_ref[:, j * half:(j + 1) * half], bd)
        q_ref[0, :, j * half:(j + 1) * half] = (q * (HEAD_DIM ** -0.5)).astype(BF16)

    z_k0 = project(o_k, o_k + half)
    z_low = shift(z_low, 3 * c, RWKV_COLS)
    lowrank = z_low[:, :LANES]
    left = lax.broadcasted_iota(jnp.int32, (1, LANES), 1) < DECAY_RANK
    lowrank = jnp.where(left, jnp.tanh(lowrank), lowrank).astype(BF16)
    gate_in = _sigmoid(z_low[:, LANES:]).astype(BF16)

    z_k1 = project(o_k + half, o_k + c)
    wa = _dot(lowrank, wlo_ref[...])
    g_ref[0] = _dot(gate_in, g2_ref[...])
    p_q0 = project(0, half)

    lw_scale = -0.5 * math.exp(-0.5)
    lw0_ref[0] = lw_scale * jnp.tanh(w0_ref[0:1, :] + wa[:, :c]) + lw_scale
    p_q1 = project(half, c)
    lw1_ref[0] = lw_scale * jnp.tanh(w0_ref[1:2, :] + wa[:, c:2 * c]) + lw_scale
    z_r0 = project(o_r, o_r + half)

    k = shift(jnp.concatenate([z_k0, z_k1], axis=1), c, 2 * c)
    kk = k * kk_ref[...]
    kk = kk * lax.rsqrt(jnp.maximum(_head_sum(kk * kk, bd, split=False), 1e-24))
    a_ref[0] = kk.astype(BF16)
    kk_half = 0.5 * kk
    z_r1 = project(o_r + half, o_r + c)

    later = [lambda: project(o_v, o_v + half), lambda: project(o_v + half, o_v + c)]
    z_v = []
    dot_rk = None
    for d, (kd_ref, bdir_ref) in enumerate(((k0_ref, b0_ref), (k1_ref, b1_ref))):
        t = jnp.tanh(a0_ref[d:d + 1, :] + wa[:, (2 + d) * c:(3 + d) * c])
        k_dir = k * (ka_ref[0:1, :] + ka_ref[1:2, :] * t)
        kd_ref[0] = k_dir.astype(BF16)
        bdir_ref[0] = (kk_half * t + kk_half).astype(BF16)
        term = k_dir * rk_ref[d:d + 1, :]
        dot_rk = term if dot_rk is None else dot_rk + term
        z_v.append(later[d]())
        q_rows((p_q0, p_q1)[d], d)

    p_kv = project(ATTN_WIDTH, QKV_W)
    r = shift(jnp.concatenate([z_r0, z_r1], axis=1), 0, c)
    r_ref[0] = r.astype(BF16)
    r_dot = _head_sum(r * dot_rk, bd, split=False)
    k_att = _qk_rows(p_kv[:, :KV_WIDTH], cos, s1, s2, qkg_ref[:, ATTN_WIDTH:], bd)
    k_ref[0] = jnp.concatenate([k_att, pltpu.roll(k_att, HEAD_DIM, 1)], axis=1).astype(BF16)
    v = shift(jnp.concatenate(z_v, axis=1), 2 * c, 3 * c)
    vv_ref[0] = v.astype(BF16)
    bonus_ref[0] = r_dot * v
    v_att = p_kv[:, KV_WIDTH:]
    v_ref[0] = jnp.concatenate([v_att, pltpu.roll(v_att, HEAD_DIM, 1)], axis=1).astype(BF16)


def _front(x, gain, w, cos, s1, s2, qkg, bd, mu, wlo, g2, w0, a0, k_k, k_a, r_k, tm, weights):
    b, s, d = x.shape
    nt = s // tm
    w_ins, w_in_specs, w_out_specs, w_out_shapes, cols = _slab_plan(weights, b * nt, lambda bi, i: bi * nt + i)
    c = RWKV_WIDTH
    nh = tm // HALO
    const = lambda shape: pl.BlockSpec(shape, lambda bi, i: (0,) * len(shape))
    tok = lambda width: pl.BlockSpec((1, tm, width), lambda bi, i: (bi, i, 0))
    tab = pl.BlockSpec((tm, LANES), lambda bi, i: (i, 0))
    halo_p = pl.BlockSpec((1, HALO, d), lambda bi, i: (bi, jnp.maximum(i * nh - 1, 0), 0))
    halo_n = pl.BlockSpec((1, HALO, d), lambda bi, i: (bi, jnp.minimum((i + 1) * nh, s // HALO - 1), 0))
    rwkv_dtypes = (BF16,) * 3 + (F32,) * 2 + (BF16,) * 4 + (F32,) * 2
    outs = pl.pallas_call(
        functools.partial(_front_kernel, cols),
        grid=(b, nt),
        in_specs=[tok(d), halo_p, halo_n, const((1, d)), const((d, PROJ_W)), tab, tab, tab, const((1, QK_W)),
                  const((LANES, LANES)), const((1, RWKV_COLS)), const((LANES, 4 * c)), const((GATE_RANK, c)),
                  const((2, c)), const((2, c)), const((1, c)), const((2, c)), const((2, c))] + w_in_specs,
        out_specs=[tok(ATTN_WIDTH), tok(2 * KV_WIDTH), tok(2 * KV_WIDTH)] + [tok(c)] * 11 + w_out_specs,
        out_shape=[jax.ShapeDtypeStruct((b, s, ATTN_WIDTH), BF16),
                   jax.ShapeDtypeStruct((b, s, 2 * KV_WIDTH), BF16),
                   jax.ShapeDtypeStruct((b, s, 2 * KV_WIDTH), BF16)]
                  + [jax.ShapeDtypeStruct((b, s, c), dt) for dt in rwkv_dtypes] + w_out_shapes,
        compiler_params=_cparams(("parallel", "parallel")),
        name="front",
    )(x, x, x, gain, w, cos, s1, s2, qkg, bd, mu, wlo, g2, w0, a0, k_k, k_a, r_k, *w_ins)
    converted = [o.reshape(wt.shape[0], hi - lo) for o, (wt, lo, hi) in zip(outs[N_FRONT_OUT:], weights)]
    return outs[:N_FRONT_OUT], converted


def _scan_kernel(r0_ref, v0_ref, a0_ref, lw0_ref, k0_ref, b0_ref,
                 r1_ref, v1_ref, a1_ref, lw1_ref, k1_ref, b1_ref,
                 y0_ref, y1_ref, ht_ref):
    L, gw, nch = CHUNK, GROUP_W, SCAN_CHUNKS
    rep = gw // L
    ng = RWKV_WIDTH // gw

    @pl.when(pl.program_id(1) == 0)
    def _():
        ht_ref[...] = jnp.zeros_like(ht_ref)

    tau = lax.broadcasted_iota(jnp.int32, (L, gw), 0)
    sig = lax.broadcasted_iota(jnp.int32, (L, gw), 1) % L
    bd = (lax.broadcasted_iota(jnp.int32, (gw, gw), 0) // L) == (lax.broadcasted_iota(jnp.int32, (gw, gw), 1) // L)
    half = 2
    ngrp = nch // half
    hrows = half * L
    ti = lax.broadcasted_iota(jnp.int32, (hrows, hrows), 0)
    si = lax.broadcasted_iota(jnp.int32, (hrows, hrows), 1)
    same_chunk = (ti // L) == (si // L)
    tris = ((same_chunk & (si <= ti)).astype(BF16), (same_chunk & (si >= ti)).astype(BF16))
    masks = ((sig < tau, sig <= tau), (sig > tau, sig >= tau))

    def bdtile(x):
        xb = x.astype(BF16)
        return jnp.where(bd, jnp.concatenate([xb] * rep, axis=0), jnp.zeros((), BF16))

    dirs = ((r0_ref, v0_ref, a0_ref, lw0_ref, k0_ref, b0_ref),
            (r1_ref, v1_ref, a1_ref, lw1_ref, k1_ref, b1_ref))

    def build(grp):
        out = {}
        for d, (r_ref, v_ref, a_ref, lw_ref, k_ref, b_ref) in enumerate(dirs):
            base = (grp if d == 0 else ngrp - 1 - grp) * hrows
            rsl = slice(base, base + hrows)
            strict, incl = masks[d]
            lw = lw_ref[0, rsl, :]
            cl = _split_dot_left(tris[d], lw)
            e_in = jnp.exp(cl)
            e_inv = jnp.exp(-cl)
            e_ex = jnp.exp(cl - lw)
            a_t = -(a_ref[0, rsl, :].astype(F32) * e_ex)
            r_t = r_ref[0, rsl, :].astype(F32) * e_in
            b_t = b_ref[0, rsl, :].astype(F32) * e_inv
            k_t = k_ref[0, rsl, :].astype(F32) * e_inv
            v_all = v_ref[0, rsl, :]
            for jj in range(half):
                rs = slice(jj * L, (jj + 1) * L)
                last = (jj + 1) * L - 1 if d == 0 else jj * L
                d_end = jnp.exp(cl[last:last + 1, :])
                b_h = b_t[rs] * d_end
                k_h = k_t[rs] * d_end
                for g in range(ng):
                    sl = slice(g * gw, (g + 1) * gw)
                    out[(base // L + jj, d, g)] = dict(
                        strict=strict, incl=incl, at=a_t[rs, sl], rt=r_t[rs, sl], vv=v_all[rs, sl], bt=b_t[rs, sl],
                        kt=k_t[rs, sl], bh=b_h[:, sl], kh=k_h[:, sl], dend=d_end[:, sl])
        return out

    def s_gram(par):
        for c in par:
            lhs = jnp.concatenate([c["at"], c["rt"]], axis=0).astype(BF16)
            gram = _dot_nt(lhs, jnp.concatenate([bdtile(c["bt"]), bdtile(c["kt"])], axis=0))
            c["gb"], c["gk"] = gram[:, :gw], gram[:, gw:]

    def s_vprod(par):
        for c in par:
            c["m_rb"] = jnp.where(c["incl"], c["gb"][L:], 0.0).astype(BF16)
            c["p"] = jnp.where(c["strict"], c["gb"][:L], 0.0)
            m_k = jnp.concatenate([jnp.where(c["strict"], c["gk"][:L], 0.0), jnp.where(c["incl"], c["gk"][L:], 0.0)],
                                  axis=0)
            res = _dot(m_k.astype(BF16), bdtile(c["vv"]))
            c["makv"], c["mrkv"] = res[:L], res[L:]

    levels = int(math.log2(L))

    def s_level(lvl):
        def run(par):
            for c in par:
                ops = ([] if lvl == 0 else [bdtile(c["t"])]) + ([] if lvl == levels - 1 else [bdtile(c["p"])])
                c["res"] = _dot(c["p"].astype(BF16), ops[0] if len(ops) == 1 else jnp.concatenate(ops, axis=1))
            for c in par:
                if lvl == 0:
                    c["t"] = jnp.where(sig == tau, 1.0, c["p"])
                    c["p"] = c["res"]
                else:
                    c["t"] = c["t"] + c["res"][:, :gw]
                    if lvl < levels - 1:
                        c["p"] = c["res"][:, gw:]
        return run

    def s_apply(par):
        for c in par:
            res = _dot(c["t"].astype(BF16), jnp.concatenate([bdtile(c["at"]), bdtile(c["makv"])], axis=1))
            c["ah"], c["uloc"] = res[:, :gw], res[:, gw:]

    def s_g(par):
        for c in par:
            c["bkh"] = jnp.concatenate([c["bh"], c["kh"]], axis=0).astype(BF16)
            c["g"] = jnp.where(bd, _dot_tn(c["ah"].astype(BF16), c["bkh"][:L]), 0.0).astype(BF16)

    def s_c(par):
        for c in par:
            c["cst"] = jnp.where(bd, _dot_tn(jnp.concatenate([c["uloc"].astype(BF16), c["vv"]], axis=0), c["bkh"]),
                                 0.0)

    def s_q(par):
        for c in par:
            res = _dot(c["m_rb"], jnp.concatenate([bdtile(c["ah"]), bdtile(c["uloc"])], axis=1))
            c["qh"] = (c["rt"] + res[:, :gw]).astype(BF16)
            c["yloc"] = res[:, gw:] + c["mrkv"]

    stages = [s_gram, s_vprod] + [s_level(lvl) for lvl in range(levels)] + [s_apply, s_g, s_c, s_q]

    state = {(d, g): ht_ref[d, g] for d in range(2) for g in range(ng)}
    recs = {}

    def seq_step(step):
        cur = [(d, g, recs[(step if d == 0 else nch - 1 - step, d, g)]) for d in range(2) for g in range(ng)]
        for d, g, c in cur:
            c["htb"] = state[(d, g)].astype(BF16)
        for d, g, c in cur:
            state[(d, g)] = state[(d, g)] * c["dend"] + c["cst"] + _dot(c["htb"], c["g"])
        for d, g, c in cur:
            c["y"] = c["yloc"] + _dot_nt(c["qh"], c["htb"])

    slots = {(i + 1) * len(stages) // (half + 1): i for i in range(half)}
    groups = [build(0)]
    recs.update(groups[0])
    for k in range(ngrp):
        for s, fn in enumerate(stages):
            if s == 2 and k + 1 < ngrp:
                groups.append(build(k + 1))
                recs.update(groups[k + 1])
            fn(list(groups[k].values()))
            if k > 0 and s in slots:
                seq_step((k - 1) * half + slots[s])
    for step in range(nch - half, nch):
        seq_step(step)
    for (d, g), h in state.items():
        ht_ref[d, g] = h
    for d, y_ref in enumerate((y0_ref, y1_ref)):
        y_ref[0] = jnp.concatenate(
            [jnp.concatenate([recs[(j, d, g)]["y"] for g in range(ng)], axis=1) for j in range(nch)], axis=0)


def _split_dot_left(w_bf16, x):
    hi = x.astype(BF16)
    lo = (x - hi.astype(F32)).astype(BF16)
    return _dot(w_bf16, hi) + _dot(w_bf16, lo)


def _scan(r, v, a, lw0, lw1, k0, k1, b0, b1):
    b, s, c = r.shape
    rows = CHUNK * SCAN_CHUNKS
    nblk = s // rows
    fwd = pl.BlockSpec((1, rows, c), lambda bi, i: (bi, i, 0))
    bwd = pl.BlockSpec((1, rows, c), lambda bi, i: (bi, nblk - 1 - i, 0))
    ng = c // GROUP_W
    return pl.pallas_call(
        _scan_kernel,
        grid=(b, nblk),
        in_specs=[fwd] * 6 + [bwd] * 6,
        out_specs=[fwd, bwd],
        out_shape=[jax.ShapeDtypeStruct((b, s, c), F32)] * 2,
        scratch_shapes=[pltpu.VMEM((2, ng, GROUP_W, GROUP_W), F32)],
        compiler_params=_cparams(("parallel", "arbitrary")),
        name="scan",
    )(r, v, a, lw0, k0, b0, r, v, a, lw1, k1, b1)


MID_BLOCKS = 4


def _mid_kernel(sink_ref, q_ref, kp_ref, kc_ref, kn_ref, vp_ref, vc_ref, vn_ref, x_ref, gain_ref, wg_ref, y0_ref,
                y1_ref, bonus_ref, g_ref, lnw_ref, lnb_ref, bd_ref, wua_ref, wur_ref, wout_ref,
                p_ref, gf_ref, w1_ref, w2_ref, gp_ref, wpg_ref, wple_ref, o_ref):
    n = pl.program_id(1)
    nb = pl.num_programs(1)
    blk, nsub = ATTN_BLOCK, MID_BLOCKS
    d = x_ref.shape[2]
    k = jnp.concatenate([kp_ref[0], kc_ref[0], kn_ref[0]], axis=0)
    v = jnp.concatenate([vp_ref[0], vc_ref[0], vn_ref[0]], axis=0)
    qi = lax.broadcasted_iota(jnp.int32, (2 * blk, blk), 0) % blk
    ki = lax.broadcasted_iota(jnp.int32, (2 * blk, blk), 1)
    first = lax.broadcasted_iota(jnp.int32, (2 * blk, 1), 0) < blk
    left = lax.broadcasted_iota(jnp.int32, (1, LANES), 1) < HEAD_DIM
    zero = jnp.zeros((), BF16)
    bd = bd_ref[...]

    def scores(j):
        rows = slice(j * blk, (j + 1) * blk)
        win = slice(j * blk, (j + 3) * blk)
        mask_prev = (ki >= qi) & ((n > 0) if j == 0 else True)
        mask_next = (ki <= qi) & ((n < nb - 1) if j == nsub - 1 else True)
        recs = []
        for g in range(KV_HEADS):
            pairs = jnp.concatenate([q_ref[0, rows, p * LANES:(p + 1) * LANES] for p in (2 * g, 2 * g + 1)], axis=0)
            for side in range(2):
                off = LANES * ((g + side) % 2)
                keep = left if side == 0 else jnp.logical_not(left)
                sk = jnp.where(first, sink_ref[4 * g + side], sink_ref[4 * g + 2 + side])
                sc = _dot_nt(jnp.where(keep, pairs, zero), k[win, off:off + LANES])
                sc = jnp.concatenate([jnp.where(mask_prev, sc[:, :blk], -1e30), sc[:, blk:2 * blk],
                                      jnp.where(mask_next, sc[:, 2 * blk:], -1e30)], axis=1)
                recs.append(dict(sc=sc, sk=sk, v=jnp.where(keep, v[win, off:off + LANES], zero)))
        return recs

    def softmax(recs):
        for c in recs:
            m = jnp.maximum(jnp.max(c["sc"], axis=-1, keepdims=True), c["sk"])
            e = jnp.exp(c["sc"] - m)
            c["den"] = jnp.sum(e, axis=-1, keepdims=True) + jnp.exp(c["sk"] - m)
            c["e"] = e.astype(BF16)

    def weighted(recs):
        outs = []
        for g in range(KV_HEADS):
            both = sum(_dot(c["e"], c["v"]) / c["den"] for c in recs[2 * g:2 * g + 2])
            outs += [both[:blk], both[blk:]]
        return jnp.concatenate(outs, axis=1).astype(BF16)

    quarter = wg_ref.shape[1] // 4
    x = x_ref[0]
    blocks = [scores(0), scores(1)]
    h = _rms(x, gain_ref[...]).astype(BF16)
    gate_pre = [_dot(h, wg_ref[:, :quarter])]
    softmax(blocks[0])
    gate_pre.append(_dot(h, wg_ref[:, quarter:2 * quarter]))
    softmax(blocks[1])
    attn = [weighted(blocks[0])]
    blocks.append(scores(2))
    gates = [_sigmoid(gate_pre[0])]
    gate_pre.append(_dot(h, wg_ref[:, 2 * quarter:3 * quarter]))
    attn.append(weighted(blocks[1]))
    blocks.append(scores(3))
    softmax(blocks[2])
    gate_pre.append(_dot(h, wg_ref[:, 3 * quarter:]))
    gates.append(_sigmoid(gate_pre[1]))
    softmax(blocks[3])
    attn.append(weighted(blocks[2]))

    y = y0_ref[0] + y1_ref[0]
    yc = y - _head_sum(y, bd) * (1.0 / HEAD_DIM)
    var = _head_sum(yc * yc, bd, split=False) * (1.0 / HEAD_DIM)
    gates.append(_sigmoid(gate_pre[2]))
    attn.append(weighted(blocks[3]))
    yn = yc * lax.rsqrt(var + GN_EPS) * lnw_ref[...] + lnb_ref[...]
    rw = ((yn + bonus_ref[0]) * g_ref[0]).astype(BF16)
    up_attn = _dot(jnp.concatenate(attn, axis=0), wua_ref[...])
    gates.append(_sigmoid(gate_pre[3]))
    up_rwkv = _dot(rw, wur_ref[...])
    half = d // 2
    merged = (jnp.concatenate(gates[:2], axis=1) * up_attn + jnp.concatenate(gates[2:], axis=1) * up_rwkv).astype(BF16)
    x1 = jnp.concatenate([x[:, col:col + half] + _dot(merged, wout_ref[:, col:col + half])
                          for col in range(0, d, half)], axis=1)
    h2 = _rms(x1, gf_ref[...]).astype(BF16)
    acc = x1
    for j in range(0, w1_ref.shape[1], FF_SLAB):
        hid = jnp.maximum(_dot(h2, w1_ref[:, j:j + FF_SLAB]), 0.0)
        acc = acc + _dot((hid * hid).astype(BF16), w2_ref[j:j + FF_SLAB, :])
    hp = _rms(acc, gp_ref[...]).astype(BF16)
    o_ref[0] = acc + _dot(p_ref[0].astype(BF16), wple_ref[...]) * _sigmoid(_dot(hp, wpg_ref[...]))


def _mid(sink, q, k2, v2, x, gain, wg, y0, y1, bonus, g, lnw, lnb, bd, wua, wur, wout, p, gf, w1, w2, gp, wpg, wple):
    b, s, d = x.shape
    c = RWKV_WIDTH
    tm = ATTN_BLOCK * MID_BLOCKS
    nb = s // tm
    last = s // ATTN_BLOCK - 1
    dff, pd = w1.shape[1], p.shape[-1]
    const = lambda shape: pl.BlockSpec(shape, lambda bi, n: (0,) * len(shape), pipeline_mode=pl.Buffered(1))
    tok = lambda width: pl.BlockSpec((1, tm, width), lambda bi, n: (bi, n, 0))
    prev = pl.BlockSpec((1, ATTN_BLOCK, 2 * KV_WIDTH), lambda bi, n: (bi, jnp.maximum(n * MID_BLOCKS - 1, 0), 0))
    nxt = pl.BlockSpec((1, ATTN_BLOCK, 2 * KV_WIDTH), lambda bi, n: (bi, jnp.minimum((n + 1) * MID_BLOCKS, last), 0))
    kv = [prev, tok(2 * KV_WIDTH), nxt]
    return pl.pallas_call(
        _mid_kernel,
        grid=(b, nb),
        in_specs=[pl.BlockSpec(memory_space=pltpu.SMEM), tok(ATTN_WIDTH)] + kv + kv
                 + [tok(d), const((1, d)), const((d, 2 * d)), tok(c), tok(c), tok(c), tok(c), const((1, c)),
                    const((1, c)), const((LANES, LANES)), const((ATTN_WIDTH, d)), const((c, d)), const((d, d)),
                    tok(pd), const((1, d)), const((d, dff)), const((dff, d)), const((1, d)), const((d, d)),
                    const((pd, d))],
        out_specs=tok(d),
        out_shape=jax.ShapeDtypeStruct((b, s, d), F32),
        compiler_params=_cparams(("parallel", "parallel")),
        name="mid",
    )(sink, q, k2, k2, k2, v2, v2, v2, x, gain, wg, y0, y1, bonus, g, lnw, lnb, bd, wua, wur, wout,
      p, gf, w1, w2, gp, wpg, wple)


FF_SLAB = 1024


def _ffn_kernel(x_ref, p_ref, gf_ref, w1_ref, w2_ref, gp_ref, wpg_ref, wple_ref, o_ref):
    x = x_ref[0]
    h = _rms(x, gf_ref[...]).astype(BF16)
    acc = x
    for j in range(0, w1_ref.shape[1], FF_SLAB):
        hid = jnp.maximum(_dot(h, w1_ref[:, j:j + FF_SLAB]), 0.0)
        acc = acc + _dot((hid * hid).astype(BF16), w2_ref[j:j + FF_SLAB, :])
    hp = _rms(acc, gp_ref[...]).astype(BF16)
    o_ref[0] = acc + _dot(p_ref[0].astype(BF16), wple_ref[...]) * _sigmoid(_dot(hp, wpg_ref[...]))


def _ffn(x, p, gf, w1, w2, gp, wpg, wple, tm):
    b, s, d = x.shape
    dff = w1.shape[1]
    pd = p.shape[-1]
    const = lambda shape: pl.BlockSpec(shape, lambda bi, i: (0,) * len(shape), pipeline_mode=pl.Buffered(1))
    tok = lambda width: pl.BlockSpec((1, tm, width), lambda bi, i: (bi, i, 0))
    return pl.pallas_call(
        _ffn_kernel,
        grid=(b, s // tm),
        in_specs=[tok(d), tok(pd), const((1, d)), const((d, dff)), const((dff, d)), const((1, d)),
                  const((d, d)), const((pd, d))],
        out_specs=tok(d),
        out_shape=jax.ShapeDtypeStruct((b, s, d), F32),
        compiler_params=_cparams(("parallel", "parallel")),
        name="ffn",
    )(x, p, gf, w1, w2, gp, wpg, wple)


def _rotary_tables(s):
    half = ROT_DIM // 2
    inv_freq = jnp.power(jnp.float32(ROPE_THETA), -jnp.arange(half, dtype=F32) * 2.0 / ROT_DIM)
    ang = jnp.arange(s).astype(F32)[:, None] * inv_freq[None, :]
    cos, sin = jnp.cos(ang), jnp.sin(ang)
    pad = jnp.zeros((s, HEAD_DIM - ROT_DIM), F32)
    zero = jnp.zeros((s, half), F32)
    c = jnp.concatenate([cos, cos, pad + 1.0], axis=1)
    s1 = jnp.concatenate([-sin, zero, pad], axis=1)
    s2 = jnp.concatenate([zero, sin, pad], axis=1)
    rep = LANES // HEAD_DIM
    return jnp.tile(c, (1, rep)), jnp.tile(s1, (1, rep)), jnp.tile(s2, (1, rep))


def kernel(x, p, norm_mix, w_in, shift_mu, q_norm, k_norm, sink, w0, w2, a0, a2, g2, k_k, k_a, r_k, lnx_w, lnx_b,
           w_up_attn, w_up_rwkv, w_out, norm_ffn, w_ff1, w_ff2, norm_ple, w_ple_gate, w_ple):
    bsz, s, d = x.shape
    depth = w_in.shape[0]
    c = RWKV_WIDTH
    tm = min(512, s)
    cos, s1, s2 = _rotary_tables(s)
    lane = jnp.arange(LANES)
    bd = ((lane[:, None] // HEAD_DIM) == (lane[None, :] // HEAD_DIM)).astype(BF16)
    o_g = PROJ_W
    for i in range(depth):
        wi = w_in[i]
        qkg = jnp.concatenate([jnp.tile(q_norm[i], Q_HEADS), jnp.tile(k_norm[i], KV_HEADS)])[None, :]
        zpad = jnp.zeros((DECAY_RANK, c), F32)
        wlo = jnp.concatenate([jnp.concatenate([w2[i, 0], zpad], axis=0), jnp.concatenate([w2[i, 1], zpad], axis=0),
                               jnp.concatenate([zpad, a2[i, 0]], axis=0), jnp.concatenate([zpad, a2[i, 1]], axis=0)],
                              axis=1)
        wlo = (0.5 * wlo).astype(BF16)
        k_mix = jnp.stack([1.0 - 0.5 * k_a[i], 0.5 * k_a[i]])
        full = lambda w: (w, 0, w.shape[1])
        ((q, k2, v2, r, v, a, lw0, lw1, k0, k1, b0, b1, g, bonus), (wg, wua, wur, wo, wf1, wf2, wpg, wpl)) = _front(
            x, norm_mix[i][None, :], wi, cos, s1, s2, qkg, bd, shift_mu[i][None, :], wlo, g2[i].astype(BF16),
            0.5 * w0[i], 0.5 * a0[i], k_k[i][None, :], k_mix, r_k[i].reshape(2, c), tm,
            [(wi, o_g, wi.shape[1]), full(w_up_attn[i]), full(w_up_rwkv[i]), full(w_out[i]), full(w_ff1[i]),
             full(w_ff2[i]), full(w_ple_gate[i]), full(w_ple[i])])
        y0, y1 = _scan(r, v, a, lw0, lw1, k0, k1, b0, b1)
        x = _mid(sink[i], q, k2, v2, x, norm_mix[i][None, :], wg, y0, y1, bonus, g, lnx_w[i][None, :],
                 lnx_b[i][None, :], bd, wua, wur, wo, p[i], norm_ffn[i][None, :], wf1, wf2, norm_ple[i][None, :],
                 wpg, wpl)
    return x
```
